```python
import math
import jax, jax.numpy as jnp
from jax import lax
import numpy as np

D_MODEL = 1024
BATCH = 16
SEQ = 4096
DEPTH = 4

HEAD_DIM = 64
MEM_LEN = 256
NORM_EPS = 1e-6
Q_BLOCK = 128
NEG_INF = -1e30
FORCE_SCORE = 1e9

NSA_HEADS = 8
NSA_KV_HEADS = 2
NSA_GROUP = NSA_HEADS // NSA_KV_HEADS
NSA_CMP_LEN = 32
NSA_CMP_STRIDE = 16
NSA_SEL_LEN = 64
NSA_TOP_N = 8
NSA_WINDOW = 512
NSA_Q_BLOCK = 64

SWA_HEADS = 8
SWA_KV_HEADS = 2
SWA_GROUP = SWA_HEADS // SWA_KV_HEADS
SWA_WINDOW = 128

MLA_HEADS = 8
MLA_Q_RANK = 256
MLA_KV_RANK = 128
MLA_NOPE_DIM = 64
MLA_ROPE_DIM = 32
MLA_V_DIM = 64
ROPE_THETA = 10000.0

FOX_HEADS = 8

XA_HEADS = 4

N_BRANCH = 4
BRANCH_WIDTH = 512
D_FF = ((8 * D_MODEL + 3 * 256 - 1) // (3 * 256)) * 256

NSA_Q_COLS = NSA_HEADS * HEAD_DIM
NSA_KV_COLS = NSA_KV_HEADS * HEAD_DIM
SWA_Q_COLS = SWA_HEADS * HEAD_DIM
SWA_KV_COLS = SWA_KV_HEADS * HEAD_DIM
FOX_COLS = FOX_HEADS * HEAD_DIM
IN_SPLITS = (NSA_Q_COLS, NSA_KV_COLS, NSA_KV_COLS, NSA_KV_COLS, NSA_KV_COLS, NSA_KV_COLS, NSA_KV_COLS, 3 * NSA_HEADS,
             SWA_Q_COLS, SWA_KV_COLS, SWA_KV_COLS,
             MLA_Q_RANK, MLA_KV_RANK, MLA_ROPE_DIM,
             FOX_COLS, FOX_COLS, FOX_COLS, FOX_HEADS)
IN_COLS = sum(IN_SPLITS)

kernel_name = "hybrid_nsa_swa_mla_fox_gated_trunk"


def rmsnorm(x, g):
    xf = x.astype(jnp.float32)
    y = xf * lax.rsqrt(jnp.mean(xf * xf, axis=-1, keepdims=True) + NORM_EPS)
    return (y * g.astype(jnp.float32)).astype(x.dtype)


def alibi_slopes(n):
    return jnp.asarray(2.0 ** (-8.0 * np.arange(1, n + 1) / n), dtype=jnp.float32)


def split_columns(z):
    offsets = np.cumsum(np.array(IN_SPLITS))[:-1].tolist()
    return jnp.split(z, offsets, axis=-1)


def rope(x, pos):
    half = x.shape[-1] // 2
    inv = ROPE_THETA ** (-jnp.arange(half, dtype=jnp.float32) / half)
    ang = pos[:, :, None, None] * inv
    cos, sin = jnp.cos(ang), jnp.sin(ang)
    xf = x.astype(jnp.float32)
    x1, x2 = xf[..., :half], xf[..., half:]
    return jnp.concatenate([x1 * cos - x2 * sin, x2 * cos + x1 * sin], axis=-1).astype(x.dtype)


def banded_attention(q, k, v, pos, window, slopes, sinks=None):
    B, S, Hk, G, Dh = q.shape
    nb = S // Q_BLOCK
    pad = window
    klen = pad + Q_BLOCK
    kp = jnp.pad(k, ((0, 0), (pad, 0), (0, 0), (0, 0)))
    vp = jnp.pad(v, ((0, 0), (pad, 0), (0, 0), (0, 0)))
    pp = jnp.pad(pos, ((0, 0), (pad, 0)))
    rel = np.arange(Q_BLOCK)[:, None] + pad - np.arange(klen)[None, :]
    band = jnp.asarray((rel >= 0) & (rel < window))
    scale = Dh ** -0.5

    def block(n):
        start = n * Q_BLOCK
        qn = lax.dynamic_slice_in_dim(q, start, Q_BLOCK, axis=1)
        pq = lax.dynamic_slice_in_dim(pos, start, Q_BLOCK, axis=1)
        kn = lax.dynamic_slice_in_dim(kp, start, klen, axis=1)
        vn = lax.dynamic_slice_in_dim(vp, start, klen, axis=1)
        pk = lax.dynamic_slice_in_dim(pp, start, klen, axis=1)
        valid = band & ((start + jnp.arange(klen)) >= pad)[None, :]
        s = jnp.einsum('bqhgd,bkhd->bhgqk', qn, kn).astype(jnp.float32) * scale
        s = s - slopes[None, :, :, None, None] * jnp.abs(pq[:, :, None] - pk[:, None, :])[:, None, None]
        s = jnp.where(valid, s, NEG_INF)
        if sinks is not None:
            sink = jnp.broadcast_to(sinks.astype(jnp.float32)[None, :, :, None, None], s.shape[:-1] + (1,))
            p = jax.nn.softmax(jnp.concatenate([s, sink], axis=-1), axis=-1)[..., :-1]
        else:
            p = jax.nn.softmax(s, axis=-1)
        return jnp.einsum('bhgqk,bkhd->bqhgd', p.astype(vn.dtype), vn)

    out = lax.map(block, jnp.arange(nb))
    return out.swapaxes(0, 1).reshape(B, S, Hk, G, Dh)


def causal_block_attention(q, k, v, scale, cum_logf=None):
    B, S, H, _ = q.shape
    nb = S // Q_BLOCK
    kpos = jnp.arange(S)

    def block(n):
        start = n * Q_BLOCK
        qn = lax.dynamic_slice_in_dim(q, start, Q_BLOCK, axis=1)
        s = jnp.einsum('bqhd,bkhd->bhqk', qn, k).astype(jnp.float32) * scale
        if cum_logf is not None:
            cq = lax.dynamic_slice_in_dim(cum_logf, start, Q_BLOCK, axis=1)
            s = s + cq.transpose(0, 2, 1)[:, :, :, None] - cum_logf.transpose(0, 2, 1)[:, :, None, :]
        tq = start + jnp.arange(Q_BLOCK)
        s = jnp.where(tq[:, None] >= kpos[None, :], s, NEG_INF)
        p = jax.nn.softmax(s, axis=-1)
        return jnp.einsum('bhqk,bkhd->bqhd', p.astype(v.dtype), v)

    out = lax.map(block, jnp.arange(nb))
    return out.swapaxes(0, 1).reshape(B, S, H, v.shape[-1])


def nsa_compress(x, cidx, w, pe):
    blocks = x[:, cidx] + pe[:, None, :].astype(x.dtype)
    return jnp.einsum('bclhd,lde->bche', blocks, w)


def nsa_mixer(q, k_c, v_c, k_s, v_s, k_w, v_w, gates, pos, cmp_w, cmp_pe):
    B, S, Hk, G, Dh = q.shape
    scale = Dh ** -0.5
    slopes = alibi_slopes(NSA_HEADS).reshape(Hk, G)
    n_c = (S - NSA_CMP_LEN) // NSA_CMP_STRIDE + 1
    cidx = np.arange(n_c)[:, None] * NSA_CMP_STRIDE + np.arange(NSA_CMP_LEN)[None, :]
    c_start, c_end = cidx[:, 0], cidx[:, -1]
    kc = nsa_compress(k_c, cidx, cmp_w[0], cmp_pe[0])
    vc = nsa_compress(v_c, cidx, cmp_w[1], cmp_pe[1])
    pc = pos[:, c_end]
    c_end_j = jnp.asarray(c_end)
    n_sel = S // NSA_SEL_LEN
    n_top = min(NSA_TOP_N, n_sel)
    sel_start = np.arange(n_sel) * NSA_SEL_LEN
    overlap = jnp.asarray(((c_start[:, None] < sel_start[None, :] + NSA_SEL_LEN) &
                           (c_end[:, None] >= sel_start[None, :])).astype(np.float32))
    ks_blk = k_s.reshape(B, n_sel, NSA_SEL_LEN, Hk, Dh).transpose(0, 3, 1, 2, 4)
    vs_blk = v_s.reshape(B, n_sel, NSA_SEL_LEN, Hk, Dh).transpose(0, 3, 1, 2, 4)
    ps_blk = pos.reshape(B, n_sel, NSA_SEL_LEN)
    b_ix = jnp.arange(B)[:, None, None, None]
    h_ix = jnp.arange(Hk)[None, None, :, None]
    jsel = jnp.arange(n_sel)
    Qb = NSA_Q_BLOCK

    def block(n):
        start = n * Qb
        qn = lax.dynamic_slice_in_dim(q, start, Qb, axis=1)
        pq = lax.dynamic_slice_in_dim(pos, start, Qb, axis=1)
        tq = start + jnp.arange(Qb)
        s = jnp.einsum('bqhgd,bchd->bqhgc', qn, kc).astype(jnp.float32) * scale
        s = s - slopes[None, None, :, :, None] * jnp.abs(pq[:, :, None] - pc[:, None, :])[:, :, None, None, :]
        cvis = (c_end_j[None, :] <= tq[:, None])[None, :, None, None, :]
        p = jax.nn.softmax(jnp.where(cvis, s, NEG_INF), axis=-1) * cvis
        o_c = jnp.einsum('bqhgc,bchd->bqhgd', p.astype(vc.dtype), vc)
        imp = jnp.einsum('bqhgc,cj->bqhj', p, overlap)
        jt = tq // NSA_SEL_LEN
        forced = (jsel[None, :] == 0) | (jsel[None, :] == jt[:, None]) | (jsel[None, :] == jt[:, None] - 1)
        causal = jsel[None, :] <= jt[:, None]
        imp = jnp.where(forced[None, :, None, :], FORCE_SCORE, imp)
        imp = jnp.where(causal[None, :, None, :], imp, NEG_INF)
        top_val, top_idx = lax.top_k(imp, n_top)
        ksel = ks_blk[b_ix, h_ix, top_idx]
        vsel = vs_blk[b_ix, h_ix, top_idx]
        psel = ps_blk[b_ix, top_idx]
        tok = top_idx[..., None] * NSA_SEL_LEN + jnp.arange(NSA_SEL_LEN)
        svis = (tok <= tq[None, :, None, None, None]) & (top_val > 0.5 * NEG_INF)[..., None]
        s2 = jnp.einsum('bqhgd,bqhnld->bqhgnl', qn, ksel).astype(jnp.float32) * scale
        s2 = s2 - slopes[None, None, :, :, None, None] * jnp.abs(pq[:, :, None, None, None] - psel)[:, :, :, None]
        s2 = jnp.where(svis[:, :, :, None], s2, NEG_INF)
        p2 = jax.nn.softmax(s2.reshape(B, Qb, Hk, G, n_top * NSA_SEL_LEN), axis=-1).reshape(s2.shape)
        o_s = jnp.einsum('bqhgnl,bqhnld->bqhgd', p2.astype(vsel.dtype), vsel)
        return o_c, o_s

    o_c, o_s = lax.map(block, jnp.arange(S // Qb))
    o_c = o_c.swapaxes(0, 1).reshape(B, S, Hk, G, Dh)
    o_s = o_s.swapaxes(0, 1).reshape(B, S, Hk, G, Dh)
    o_w = banded_attention(q, k_w, v_w, pos, NSA_WINDOW, slopes)
    return gates[..., 0:1] * o_c + gates[..., 1:2] * o_s + gates[..., 2:3] * o_w


def mla_mixer(c_q, c_kv, k_rope, pos, q_norm, w_q_up, kv_norm, w_kv_up):
    B, S, _ = c_q.shape
    q = (rmsnorm(c_q, q_norm) @ w_q_up).reshape(B, S, MLA_HEADS, MLA_NOPE_DIM + MLA_ROPE_DIM)
    kv = (rmsnorm(c_kv, kv_norm) @ w_kv_up).reshape(B, S, MLA_HEADS, MLA_NOPE_DIM + MLA_V_DIM)
    q_nope, q_rot = q[..., :MLA_NOPE_DIM], rope(q[..., MLA_NOPE_DIM:], pos)
    k_nope, v = kv[..., :MLA_NOPE_DIM], kv[..., MLA_NOPE_DIM:]
    k_rot = jnp.broadcast_to(rope(k_rope[:, :, None, :], pos), (B, S, MLA_HEADS, MLA_ROPE_DIM))
    qh = jnp.concatenate([q_nope, q_rot], axis=-1)
    kh = jnp.concatenate([k_nope, k_rot], axis=-1)
    return causal_block_attention(qh, kh, v, (MLA_NOPE_DIM + MLA_ROPE_DIM) ** -0.5)


def fox_mixer(q, k, v, f_logit, f_bias):
    log_f = jax.nn.log_sigmoid(f_logit.astype(jnp.float32) + f_bias.astype(jnp.float32))
    cum = jnp.cumsum(log_f, axis=1)
    return causal_block_attention(q, k, v, HEAD_DIM ** -0.5, cum)


def memory_cross_attention(h, mem_n, w_q, w_kv, w_o):
    B, S, _ = h.shape
    M = mem_n.shape[1]
    q = (h @ w_q).reshape(B, S, XA_HEADS, HEAD_DIM)
    kv = (mem_n @ w_kv).reshape(B, M, 2, XA_HEADS, HEAD_DIM)
    s = jnp.einsum('bshd,bmhd->bhsm', q, kv[:, :, 0]).astype(jnp.float32) * HEAD_DIM ** -0.5
    p = jax.nn.softmax(s, axis=-1)
    o = jnp.einsum('bhsm,bmhd->bshd', p.astype(kv.dtype), kv[:, :, 1])
    return o.reshape(B, S, XA_HEADS * HEAD_DIM) @ w_o


def setup_inputs(seed: int = 0) -> dict:
    key = jax.random.key(seed)
    ks = jax.random.split(key, 32)
    f32 = jnp.float32
    L, D = DEPTH, D_MODEL

    def nrm(k, shape, fan_in):
        return jax.random.normal(k, shape, f32) * fan_in ** -0.5

    def gain(k, shape):
        return 1.0 + 0.05 * jax.random.normal(k, shape, f32)

    offset = jax.random.randint(ks[2], (BATCH, 1), 0, 4096, dtype=jnp.int32)
    return {
        "x": jax.random.normal(ks[0], (BATCH, SEQ, D), f32),
        "mem": jax.random.normal(ks[1], (BATCH, MEM_LEN, D), f32),
        "positions": offset + jnp.arange(SEQ, dtype=jnp.int32)[None, :],
        "mix_norm_pre": gain(ks[3], (L, D)),
        "w_in": nrm(ks[4], (L, D, IN_COLS), D),
        "nsa_cmp_w": nrm(ks[5], (L, 2, NSA_CMP_LEN, HEAD_DIM, HEAD_DIM), NSA_CMP_LEN * HEAD_DIM),
        "nsa_cmp_pe": 0.1 * jax.random.normal(ks[6], (L, 2, NSA_CMP_LEN, HEAD_DIM), f32),
        "swa_sinks": 0.5 * jax.random.normal(ks[7], (L, SWA_HEADS), f32),
        "mla_q_norm": gain(ks[8], (L, MLA_Q_RANK)),
        "mla_w_q_up": nrm(ks[9], (L, MLA_Q_RANK, MLA_HEADS * (MLA_NOPE_DIM + MLA_ROPE_DIM)), MLA_Q_RANK),
        "mla_kv_norm": gain(ks[10], (L, MLA_KV_RANK)),
        "mla_w_kv_up": nrm(ks[11], (L, MLA_KV_RANK, MLA_HEADS * (MLA_NOPE_DIM + MLA_V_DIM)), MLA_KV_RANK),
        "fox_f_bias": 2.0 + 0.5 * jax.random.normal(ks[12], (L, FOX_HEADS), f32),
        "w_branch": nrm(ks[13], (L, N_BRANCH, BRANCH_WIDTH, D), BRANCH_WIDTH),
        "w_gate": nrm(ks[14], (L, N_BRANCH, D, D), D),
        "w_out": nrm(ks[15], (L, D, D), D),
        "mix_norm_post": gain(ks[16], (L, D)),
        "xa_norm_pre": gain(ks[17], (L, D)),
        "xa_mem_norm": gain(ks[18], (L, D)),
        "xa_w_q": nrm(ks[19], (L, D, XA_HEADS * HEAD_DIM), D),
        "xa_w_kv": nrm(ks[20], (L, D, 2 * XA_HEADS * HEAD_DIM), D),
        "xa_w_o": nrm(ks[21], (L, XA_HEADS * HEAD_DIM, D), XA_HEADS * HEAD_DIM),
        "xa_norm_post": gain(ks[22], (L, D)),
        "ffn_norm_pre": gain(ks[23], (L, D)),
        "ffn_w_in": nrm(ks[24], (L, D, 2 * D_FF), D),
        "ffn_w_out": nrm(ks[25], (L, D_FF, D), D_FF),
        "ffn_norm_post": gain(ks[26], (L, D)),
    }


def reference(x, mem, positions, mix_norm_pre, w_in, nsa_cmp_w, nsa_cmp_pe, swa_sinks,
              mla_q_norm, mla_w_q_up, mla_kv_norm, mla_w_kv_up, fox_f_bias, w_branch, w_gate,
              w_out, mix_norm_post, xa_norm_pre, xa_mem_norm, xa_w_q, xa_w_kv, xa_w_o,
              xa_norm_post, ffn_norm_pre, ffn_w_in, ffn_w_out, ffn_norm_post):
    B, S, D = x.shape
    pos = positions.astype(jnp.float32)
    swa_slopes = alibi_slopes(SWA_HEADS).reshape(SWA_KV_HEADS, SWA_GROUP)
    for l in range(DEPTH):
        h = rmsnorm(x, mix_norm_pre[l])
        (nq, nkc, nvc, nks, nvs, nkw, nvw, ng, sq, sk, sv,
         mcq, mckv, mkr, fq, fk, fv, ff) = split_columns(h @ w_in[l])
        kvh = lambda t: t.reshape(B, S, NSA_KV_HEADS, HEAD_DIM)
        y_nsa = nsa_mixer(nq.reshape(B, S, NSA_KV_HEADS, NSA_GROUP, HEAD_DIM),
                          kvh(nkc), kvh(nvc), kvh(nks), kvh(nvs), kvh(nkw), kvh(nvw),
                          jax.nn.sigmoid(ng).reshape(B, S, NSA_KV_HEADS, NSA_GROUP, 3),
                          pos, nsa_cmp_w[l], nsa_cmp_pe[l]).reshape(B, S, BRANCH_WIDTH)
        y_swa = banded_attention(sq.reshape(B, S, SWA_KV_HEADS, SWA_GROUP, HEAD_DIM),
                                 sk.reshape(B, S, SWA_KV_HEADS, HEAD_DIM),
                                 sv.reshape(B, S, SWA_KV_HEADS, HEAD_DIM),
                                 pos, SWA_WINDOW, swa_slopes,
                                 swa_sinks[l].reshape(SWA_KV_HEADS, SWA_GROUP)).reshape(B, S, BRANCH_WIDTH)
        y_mla = mla_mixer(mcq, mckv, mkr, pos, mla_q_norm[l], mla_w_q_up[l],
                          mla_kv_norm[l], mla_w_kv_up[l]).reshape(B, S, BRANCH_WIDTH)
        y_fox = fox_mixer(fq.reshape(B, S, FOX_HEADS, HEAD_DIM), fk.reshape(B, S, FOX_HEADS, HEAD_DIM),
                          fv.reshape(B, S, FOX_HEADS, HEAD_DIM), ff, fox_f_bias[l]).reshape(B, S, BRANCH_WIDTH)
        branches = (y_nsa, y_swa, y_mla, y_fox)
        merged = sum(jax.nn.sigmoid(h @ w_gate[l, i]) * (y @ w_branch[l, i]) for i, y in enumerate(branches))
        x = x + rmsnorm(merged @ w_out[l], mix_norm_post[l])
        hx = rmsnorm(x, xa_norm_pre[l])
        y = memory_cross_attention(hx, rmsnorm(mem, xa_mem_norm[l]), xa_w_q[l], xa_w_kv[l], xa_w_o[l])
        x = x + rmsnorm(y, xa_norm_post[l])
        hf = rmsnorm(x, ffn_norm_pre[l])
        gu = hf @ ffn_w_in[l]
        y = (jax.nn.silu(gu[..., :D_FF]) * gu[..., D_FF:]) @ ffn_w_out[l]
        x = x + rmsnorm(y, ffn_norm_post[l])
    return x
```

```python
import functools
import math
from typing import NamedTuple

import numpy as np
import jax
import jax.numpy as jnp
from jax import lax
from jax.experimental import pallas as pl
from jax.experimental.pallas import tpu as pltpu

F32 = jnp.float32
BF16 = jnp.bfloat16

D_MODEL = 1024
HEAD_DIM = 64
NORM_EPS = 1e-6
NEG_INF = -1e30
FORCE_SCORE = 1e9

NSA_HEADS = 8
NSA_KV_HEADS = 2
NSA_GROUP = NSA_HEADS // NSA_KV_HEADS
NSA_CMP_LEN = 32
NSA_CMP_STRIDE = 16
NSA_SEL_LEN = 64
NSA_TOP_N = 8
NSA_WINDOW = 512

SWA_HEADS = 8
SWA_KV_HEADS = 2
SWA_GROUP = SWA_HEADS // SWA_KV_HEADS
SWA_WINDOW = 128

MLA_HEADS = 8
MLA_Q_RANK = 256
MLA_KV_RANK = 128
MLA_NOPE_DIM = 64
MLA_ROPE_DIM = 32
MLA_V_DIM = 64
MLA_QK_DIM = MLA_NOPE_DIM + MLA_ROPE_DIM
ROPE_THETA = 10000.0

FOX_HEADS = 8
XA_HEADS = 4
N_BRANCH = 4
BRANCH_WIDTH = 512
D_FF = ((8 * D_MODEL + 3 * 256 - 1) // (3 * 256)) * 256

IN_SPLITS = (512, 128, 128, 128, 128, 128, 128, 24,
             512, 128, 128,
             MLA_Q_RANK, MLA_KV_RANK, MLA_ROPE_DIM,
             512, 512, 512, FOX_HEADS)
IN_NAMES = ("nq", "nkc", "nvc", "nks", "nvs", "nkw", "nvw", "ng",
            "sq", "sk", "sv", "mcq", "mckv", "mkr", "fq", "fk", "fv", "ff")

VMEM_LIMIT_BYTES = 56 * 1024 * 1024

MAIN_ORDER = ("nq", "nks", "nvs", "nkw", "nvw", "sq", "sk", "sv", "fq", "fk", "fv")
MAIN_COLS = 3328
SMALL_COLS = 128


def _cparams(sem):
    return pltpu.CompilerParams(dimension_semantics=sem, vmem_limit_bytes=VMEM_LIMIT_BYTES)


def _rms(x, g):
    return x * lax.rsqrt(jnp.mean(x * x, axis=-1, keepdims=True) + NORM_EPS) * g


def _dot(a, b):
    return jnp.dot(a, b, preferred_element_type=F32)


def _dot_nt(a, b):
    return lax.dot_general(a, b, (((1,), (1,)), ((), ())), preferred_element_type=F32)


def _inproj_kernel(x_ref, g_ref, w_ref, *o_refs, widths):
    h = _rms(x_ref[...], g_ref[...]).astype(BF16)
    c0 = 0
    for o_ref, width in zip(o_refs, widths):
        for c in range(0, width, 512):
            w = min(512, width - c)
            o_ref[:, c:c + w] = _dot(h, w_ref[:, c0 + c:c0 + c + w]).astype(o_ref.dtype)
        c0 += width


def _inproj(x2d, g, w, widths, dtypes, tm=512):
    M, D = x2d.shape
    N = w.shape[1]
    assert sum(widths) == N and M % tm == 0
    return pl.pallas_call(
        functools.partial(_inproj_kernel, widths=widths),
        grid=(M // tm,),
        in_specs=[pl.BlockSpec((tm, D), lambda i: (i, 0)),
                  pl.BlockSpec((1, D), lambda i: (0, 0)),
                  pl.BlockSpec((D, N), lambda i: (0, 0))],
        out_specs=[pl.BlockSpec((tm, wd), lambda i: (i, 0)) for wd in widths],
        out_shape=[jax.ShapeDtypeStruct((M, wd), dt) for wd, dt in zip(widths, dtypes)],
        compiler_params=_cparams(("parallel",)),
        name="inproj",
    )(x2d, g, w)


def _rope_table_kernel(pos_ref, inv_ref, cos_ref, sin_ref):
    ang = pos_ref[...] * inv_ref[...]
    cos_ref[...] = jnp.cos(ang)
    sin_ref[...] = jnp.sin(ang)


def _rope_tables(pos_col, inv_row, tm=512):
    M = pos_col.shape[0]
    N = inv_row.shape[1]
    return pl.pallas_call(
        _rope_table_kernel,
        grid=(M // tm,),
        in_specs=[pl.BlockSpec((tm, 1), lambda i: (i, 0)),
                  pl.BlockSpec((1, N), lambda i: (0, 0))],
        out_specs=[pl.BlockSpec((tm, N), lambda i: (i, 0))] * 2,
        out_shape=[jax.ShapeDtypeStruct((M, N), F32)] * 2,
        compiler_params=_cparams(("parallel",)),
        name="rope_tables",
    )(pos_col, inv_row)


def _mla_prep_kernel(z_ref, zs_ref, cos_ref, sin_ref, qg_ref, kvg_ref, wq_ref, wkv_ref,
                     qn_ref, qr_ref, kn_ref, v_ref, kr_ref):
    nq = MLA_HEADS * MLA_NOPE_DIM
    nr = MLA_HEADS * MLA_ROPE_DIM
    scale = MLA_QK_DIM ** -0.5
    z = z_ref[...]
    cq = _rms(z[:, :MLA_Q_RANK], qg_ref[...]).astype(BF16)
    ckv = _rms(z[:, MLA_Q_RANK:], kvg_ref[...]).astype(BF16)
    cos = cos_ref[...]
    sin = sin_ref[...]
    qn_ref[...] = (_dot(cq, wq_ref[:, :nq]) * scale).astype(BF16)
    q_rot = _dot(cq, wq_ref[:, nq:nq + nr])
    q_rot_half = _dot(cq, wq_ref[:, nq + nr:])
    qr_ref[...] = ((q_rot * cos + q_rot_half * sin) * scale).astype(BF16)
    kn_ref[...] = _dot(ckv, wkv_ref[:, :nq]).astype(BF16)
    v_ref[...] = _dot(ckv, wkv_ref[:, nq:]).astype(BF16)
    zs = zs_ref[...]
    r = MLA_ROPE_DIM
    kr_ref[...] = (zs[:, :r] * cos[:, :r] + zs[:, r:2 * r] * sin[:, :r]).astype(BF16)


def _mla_prep(z_mla, z_small, cos, sin, qg, kvg, wq, wkv, tm=512):
    M = z_mla.shape[0]
    nq = MLA_HEADS * MLA_NOPE_DIM
    nr = MLA_HEADS * MLA_ROPE_DIM
    row = lambda n: pl.BlockSpec((tm, n), lambda i: (i, 0))
    full = lambda a: pl.BlockSpec(a.shape, lambda i: (0,) * a.ndim)
    return pl.pallas_call(
        _mla_prep_kernel,
        grid=(M // tm,),
        in_specs=[row(z_mla.shape[1]), row(SMALL_COLS), row(nr), row(nr),
                  full(qg), full(kvg), full(wq), full(wkv)],
        out_specs=[row(nq), row(nr), row(nq), row(nq), row(MLA_ROPE_DIM)],
        out_shape=[jax.ShapeDtypeStruct((M, n), BF16) for n in (nq, nr, nq, nq, MLA_ROPE_DIM)],
        compiler_params=_cparams(("parallel",)),
        name="mla_prep",
    )(z_mla, z_small, cos, sin, qg, kvg, wq, wkv)


CUM_CHUNK = 512


def _fox_cumsum_kernel(f_ref, b_ref, u_ref, o_ref):
    x = f_ref[0] + b_ref[...]
    lf = jnp.minimum(x, 0.0) - jnp.log1p(jnp.exp(-jnp.abs(x)))
    S = lf.shape[1]
    carry = jnp.zeros((lf.shape[0], 1), F32)
    for c in range(0, S, CUM_CHUNK):
        loc = jnp.dot(lf[:, c:c + CUM_CHUNK], u_ref[...], preferred_element_type=F32,
                      precision=lax.Precision.HIGHEST) + carry
        o_ref[0, :, c:c + CUM_CHUNK] = loc
        carry = loc[:, CUM_CHUNK - 1:CUM_CHUNK]


def _fox_cumsum(f_bhs, bias_col):
    B, H, S = f_bhs.shape
    assert S % CUM_CHUNK == 0
    upper = jnp.asarray(np.triu(np.ones((CUM_CHUNK, CUM_CHUNK), np.float32)))
    return pl.pallas_call(
        _fox_cumsum_kernel,
        grid=(B,),
        in_specs=[pl.BlockSpec((1, H, S), lambda b: (b, 0, 0)),
                  pl.BlockSpec((H, 1), lambda b: (0, 0)),
                  pl.BlockSpec((CUM_CHUNK, CUM_CHUNK), lambda b: (0, 0))],
        out_specs=pl.BlockSpec((1, H, S), lambda b: (b, 0, 0)),
        out_shape=jax.ShapeDtypeStruct((B, H, S), F32),
        compiler_params=_cparams(("parallel",)),
        name="fox_cumsum",
    )(f_bhs, bias_col, upper)


def _nsa_compress_kernel(r_ref, pe_ref, w_ref, o_ref):
    r = r_ref[0, 0]
    first = _dot((r + pe_ref[0, 0:1, :]).astype(BF16), w_ref[0, 0])
    second = _dot((r + pe_ref[0, 1:2, :]).astype(BF16), w_ref[0, 1])
    nc = r.shape[0]
    nxt = pltpu.roll(second, nc - 1, 0)
    row = lax.broadcasted_iota(jnp.int32, (nc, 1), 0)
    o_ref[0, 0] = jnp.where(row < nc - 1, first + nxt, 0.0).astype(o_ref.dtype)


def _nsa_compress(rows, pe, w):
    B, F, NC, K = rows.shape
    return pl.pallas_call(
        _nsa_compress_kernel,
        grid=(B, F),
        in_specs=[pl.BlockSpec((1, 1, NC, K), lambda b, f: (b, f, 0, 0)),
                  pl.BlockSpec((1, 2, K), lambda b, f: (f // NSA_KV_HEADS, 0, 0)),
                  pl.BlockSpec((1, 2, K, HEAD_DIM), lambda b, f: (f // NSA_KV_HEADS, 0, 0, 0))],
        out_specs=pl.BlockSpec((1, 1, NC, HEAD_DIM), lambda b, f: (b, f, 0, 0)),
        out_shape=jax.ShapeDtypeStruct((B, F, NC, HEAD_DIM), BF16),
        compiler_params=_cparams(("parallel", "parallel")),
        name="nsa_compress",
    )(rows, pe, w)


NSA_CMP_TQ = 64


def _nsa_cmp_kernel(q_ref, kc_ref, vc_ref, pq_ref, pc_ref, sl_ref, gate_ref, ov_ref,
                    oc_ref, sel_ref):
    G, tq, Dh = q_ref.shape[2:]
    R = G * tq
    i = pl.program_id(2)
    q = q_ref[0, 0].reshape(R, Dh)
    kc = kc_ref[0, 0]
    nc = kc.shape[0]
    n_sel = ov_ref.shape[1]
    s = _dot_nt(q, kc)
    pq = jnp.concatenate([pq_ref[0]] * G, axis=0)
    s = s - sl_ref[0] * jnp.abs(pq - pc_ref[0])
    t3 = i * tq + lax.broadcasted_iota(jnp.int32, (G, tq, 1), 1)
    t_row = t3.reshape(R, 1)
    c_col = lax.broadcasted_iota(jnp.int32, (1, nc), 1)
    c_end = c_col * NSA_CMP_STRIDE + (NSA_CMP_LEN - 1)
    vis = (c_end <= t_row) & (c_col < nc - 1)
    s = jnp.where(vis, s, NEG_INF)
    m = jnp.max(s, axis=-1, keepdims=True)
    e = jnp.where(vis, jnp.exp(s - m), 0.0)
    l = jnp.sum(e, axis=-1, keepdims=True)
    p = e * jnp.where(l > 0.0, 1.0 / l, 0.0)
    o = _dot(p.astype(BF16), vc_ref[0, 0])
    gate = jax.nn.sigmoid(gate_ref[0, 0].reshape(R, 1))
    oc_ref[0, 0] = (gate * o).reshape(G, tq, Dh)
    p_sum = jnp.sum(p.reshape(G, tq, nc), axis=0)
    imp = jnp.dot(p_sum, ov_ref[...], preferred_element_type=F32,
                  precision=lax.Precision.HIGHEST)
    t_q = i * tq + lax.broadcasted_iota(jnp.int32, (tq, 1), 0)
    jt = t_q // NSA_SEL_LEN
    j = lax.broadcasted_iota(jnp.int32, (1, n_sel), 1)
    forced = (j == 0) | (j == jt) | (j == jt - 1)
    causal = j <= jt
    val = jnp.where(forced, FORCE_SCORE, imp)
    val = jnp.where(causal, val, NEG_INF)
    sel = jnp.zeros((tq, n_sel), jnp.int32)
    for _ in range(min(NSA_TOP_N, n_sel)):
        best = jnp.max(val, axis=-1, keepdims=True)
        idx = jnp.min(jnp.where(val == best, j, n_sel), axis=-1, keepdims=True)
        pick = j == idx
        sel = jnp.where(pick, 1, sel)
        val = jnp.where(pick, -jnp.inf, val)
    keep = (sel > 0) & causal
    sel_ref[0, 0] = jnp.where(keep, 0.0, NEG_INF).astype(sel_ref.dtype)


def _nsa_cmp(q, kvc, pos_col, pc_row, slopes, gate, overlap):
    B, Hk, G, S, Dh = q.shape
    NC = kvc.shape[2]
    n_sel = overlap.shape[1]
    tq = NSA_CMP_TQ
    R = G * tq
    return pl.pallas_call(
        _nsa_cmp_kernel,
        grid=(B, Hk, S // tq),
        in_specs=[pl.BlockSpec((1, 1, G, tq, Dh), lambda b, h, i: (b, h, 0, i, 0)),
                  pl.BlockSpec((1, 1, NC, Dh), lambda b, h, i: (b, h, 0, 0)),
                  pl.BlockSpec((1, 1, NC, Dh), lambda b, h, i: (b, h + Hk, 0, 0)),
                  pl.BlockSpec((1, tq, 1), lambda b, h, i: (b, i, 0)),
                  pl.BlockSpec((1, 1, NC), lambda b, h, i: (b, 0, 0)),
                  pl.BlockSpec((1, R, 1), lambda b, h, i: (h, 0, 0)),
                  pl.BlockSpec((1, 1, G, tq, 1), lambda b, h, i: (b, h, 0, i, 0)),
                  pl.BlockSpec((NC, n_sel), lambda b, h, i: (0, 0))],
        out_specs=[pl.BlockSpec((1, 1, G, tq, Dh), lambda b, h, i: (b, h, 0, i, 0)),
                   pl.BlockSpec((1, 1, tq, n_sel), lambda b, h, i: (b, h, i, 0))],
        out_shape=[jax.ShapeDtypeStruct((B, Hk, G, S, Dh), F32),
                   jax.ShapeDtypeStruct((B, Hk, S, n_sel), BF16)],
        compiler_params=_cparams(("parallel", "parallel", "parallel")),
        name="nsa_cmp",
    )(q, kvc, kvc, pos_col, pc_row, slopes, gate, overlap)


class AttnCfg(NamedTuple):
    tq: int
    tk: int
    window: int = 0
    alibi: bool = False
    decay: bool = False
    sink: bool = False
    gate: bool = False
    add: bool = False
    out_dtype: object = BF16


def _attn_kernel(*refs, cfg):
    it = iter(refs)
    q_ref, k_ref, v_ref = next(it), next(it), next(it)
    if cfg.alibi:
        pq_ref, pk_ref, sl_ref = next(it), next(it), next(it)
    if cfg.decay:
        cq_ref, ck_ref = next(it), next(it)
    if cfg.sink:
        sink_ref = next(it)
    if cfg.gate:
        gate_ref = next(it)
    if cfg.add:
        add_ref = next(it)
    o_ref = next(it)
    m_ref, l_ref, acc_ref = next(it), next(it), next(it)

    G, tq, Dq = q_ref.shape[2:]
    Dv = v_ref.shape[3]
    tk = cfg.tk
    R = G * tq
    i = pl.program_id(2)
    q = q_ref[0, 0].reshape(R, Dq)
    t_row = (i * tq + lax.broadcasted_iota(jnp.int32, (G, tq, 1), 1)).reshape(R, 1)
    if cfg.alibi:
        pq = jnp.concatenate([pq_ref[0]] * G, axis=0)
        slope = sl_ref[0]
    if cfg.decay:
        cq = cq_ref[0, 0].reshape(R, 1)

    m_ref[...] = jnp.full((R, 1), NEG_INF, F32)
    l_ref[...] = jnp.zeros((R, 1), F32)
    acc_ref[...] = jnp.zeros((R, Dv), F32)

    def step(j, masked):
        k0 = pl.multiple_of(j * tk, tk)
        s = _dot_nt(q, k_ref[0, 0, pl.ds(k0, tk), :])
        if cfg.alibi:
            s = s - slope * jnp.abs(pq - pk_ref[0, j])
        if cfg.decay:
            s = s + (cq - ck_ref[0, 0, j])
        if masked:
            rel = t_row - (j * tk + lax.broadcasted_iota(jnp.int32, (1, tk), 1))
            valid = rel >= 0
            if cfg.window:
                valid = valid & (rel < cfg.window)
            s = jnp.where(valid, s, NEG_INF)
        m_prev = m_ref[...]
        m_new = jnp.maximum(m_prev, jnp.max(s, axis=-1, keepdims=True))
        alpha = jnp.exp(m_prev - m_new)
        p = jnp.exp(s - m_new)
        l_ref[...] = alpha * l_ref[...] + jnp.sum(p, axis=-1, keepdims=True)
        acc_ref[...] = alpha * acc_ref[...] + _dot(p.astype(BF16), v_ref[0, 0, pl.ds(k0, tk), :])
        m_ref[...] = m_new

    j_hi = ((i + 1) * tq - 1) // tk
    if cfg.window:
        j_lo = jnp.maximum(i * tq - (cfg.window - 1), 0) // tk

        def body(j, c):
            step(j, True)
            return c
        lax.fori_loop(j_lo, j_hi + 1, body, 0)
    else:
        def body(j, c):
            step(j, False)
            return c
        lax.fori_loop(0, j_hi, body, 0)
        step(j_hi, True)

    m = m_ref[...]
    l = l_ref[...]
    acc = acc_ref[...]
    if cfg.sink:
        sink = sink_ref[0]
        m_f = jnp.maximum(m, sink)
        w = jnp.exp(m - m_f)
        out = acc * (w / (l * w + jnp.exp(sink - m_f)))
    else:
        out = acc / l
    if cfg.gate:
        out = out * jax.nn.sigmoid(gate_ref[0, 0].reshape(R, 1))
    if cfg.add:
        out = out + add_ref[0, 0].reshape(R, Dv)
    o_ref[0, 0] = out.reshape(G, tq, Dv).astype(o_ref.dtype)


def _attention(cfg, q, k, v, *, pos_col=None, pos_rows=None, slopes=None, cum_col=None,
               cum_rows=None, sinks=None, gate=None, add=None):
    B, Hk, G, S, Dq = q.shape
    Dv = v.shape[3]
    tq, tk = cfg.tq, cfg.tk
    assert S % tq == 0 and S % tk == 0 and tk % tq == 0
    R = G * tq
    nk = S // tk
    qmap = lambda b, h, i: (b, h, 0, i, 0)
    kvmap = lambda b, h, i: (b, h, 0, 0)
    args = [q, k, v]
    specs = [pl.BlockSpec((1, 1, G, tq, Dq), qmap),
             pl.BlockSpec((1, 1, S, Dq), kvmap),
             pl.BlockSpec((1, 1, S, Dv), kvmap)]
    if cfg.alibi:
        args += [pos_col, pos_rows, slopes]
        specs += [pl.BlockSpec((1, tq, 1), lambda b, h, i: (b, i, 0)),
                  pl.BlockSpec((1, nk, 1, tk), lambda b, h, i: (b, 0, 0, 0)),
                  pl.BlockSpec((1, R, 1), lambda b, h, i: (h, 0, 0))]
    if cfg.decay:
        args += [cum_col, cum_rows]
        specs += [pl.BlockSpec((1, 1, G, tq, 1), qmap),
                  pl.BlockSpec((1, 1, nk, 1, tk), lambda b, h, i: (b, h, 0, 0, 0))]
    if cfg.sink:
        args += [sinks]
        specs += [pl.BlockSpec((1, R, 1), lambda b, h, i: (h, 0, 0))]
    if cfg.gate:
        args += [gate]
        specs += [pl.BlockSpec((1, 1, G, tq, 1), qmap)]
    if cfg.add:
        args += [add]
        specs += [pl.BlockSpec((1, 1, G, tq, Dv), qmap)]
    return pl.pallas_call(
        functools.partial(_attn_kernel, cfg=cfg),
        grid=(B, Hk, S // tq),
        in_specs=specs,
        out_specs=pl.BlockSpec((1, 1, G, tq, Dv), qmap),
        out_shape=jax.ShapeDtypeStruct((B, Hk, G, S, Dv), cfg.out_dtype),
        scratch_shapes=[pltpu.VMEM((R, 1), F32), pltpu.VMEM((R, 1), F32), pltpu.VMEM((R, Dv), F32)],
        compiler_params=_cparams(("parallel", "parallel", "parallel")),
        name="attention",
    )(*args)


def _merge_kernel(x_ref, y_ref, gpre_ref, wg_ref, wb_ref, wo_ref, gpost_ref, o_ref):
    x = x_ref[...]
    h = _rms(x, gpre_ref[...]).astype(BF16)
    merged = None
    for b in range(N_BRANCH):
        gate = jax.nn.sigmoid(_dot(h, wg_ref[b]))
        br = _dot(y_ref[:, b * BRANCH_WIDTH:(b + 1) * BRANCH_WIDTH], wb_ref[b])
        merged = gate * br if merged is None else merged + gate * br
    y = _dot(merged.astype(BF16), wo_ref[...])
    o_ref[...] = x + _rms(y, gpost_ref[...])


def _merge(x2d, y, gpre, wg, wb, wo, gpost, tm=256):
    M, D = x2d.shape
    full = lambda a: pl.BlockSpec(a.shape, lambda i: (0,) * a.ndim)
    return pl.pallas_call(
        _merge_kernel,
        grid=(M // tm,),
        in_specs=[pl.BlockSpec((tm, D), lambda i: (i, 0)),
                  pl.BlockSpec((tm, y.shape[1]), lambda i: (i, 0)),
                  full(gpre), full(wg), full(wb), full(wo), full(gpost)],
        out_specs=pl.BlockSpec((tm, D), lambda i: (i, 0)),
        out_shape=jax.ShapeDtypeStruct((M, D), F32),
        compiler_params=_cparams(("parallel",)),
        name="merge",
    )(x2d, y, gpre, wg, wb, wo, gpost)


def _mem_kv_kernel(mem_ref, g_ref, w_ref, o_ref):
    o_ref[0] = _dot(_rms(mem_ref[0], g_ref[...]).astype(BF16), w_ref[...]).astype(BF16)


def _mem_kv(mem, g, w):
    B, Mm, D = mem.shape
    N = w.shape[1]
    return pl.pallas_call(
        _mem_kv_kernel,
        grid=(B,),
        in_specs=[pl.BlockSpec((1, Mm, D), lambda b: (b, 0, 0)),
                  pl.BlockSpec((1, D), lambda b: (0, 0)),
                  pl.BlockSpec((D, N), lambda b: (0, 0))],
        out_specs=pl.BlockSpec((1, Mm, N), lambda b: (b, 0, 0)),
        out_shape=jax.ShapeDtypeStruct((B, Mm, N), BF16),
        compiler_params=_cparams(("parallel",)),
        name="mem_kv",
    )(mem, g, w)


def _xattn_kernel(x_ref, kv_ref, gpre_ref, wq_ref, wo_ref, gpost_ref, o_ref):
    x = x_ref[0]
    h = _rms(x, gpre_ref[...]).astype(BF16)
    y = None
    for hd in range(XA_HEADS):
        q = _dot(h, wq_ref[hd]).astype(BF16)
        s = _dot_nt(q, kv_ref[0, hd])
        m = jnp.max(s, axis=-1, keepdims=True)
        e = jnp.exp(s - m)
        p = e / jnp.sum(e, axis=-1, keepdims=True)
        o = _dot(p.astype(BF16), kv_ref[0, XA_HEADS + hd]).astype(BF16)
        part = _dot(o, wo_ref[hd])
        y = part if y is None else y + part
    o_ref[0] = x + _rms(y, gpost_ref[...])


def _xattn(x, kv, gpre, wq, wo, gpost, tm=512):
    B, S, D = x.shape
    full = lambda a: pl.BlockSpec(a.shape, lambda b, i: (0,) * a.ndim)
    return pl.pallas_call(
        _xattn_kernel,
        grid=(B, S // tm),
        in_specs=[pl.BlockSpec((1, tm, D), lambda b, i: (b, i, 0)),
                  pl.BlockSpec((1,) + kv.shape[1:], lambda b, i: (b, 0, 0, 0)),
                  full(gpre), full(wq), full(wo), full(gpost)],
        out_specs=pl.BlockSpec((1, tm, D), lambda b, i: (b, i, 0)),
        out_shape=jax.ShapeDtypeStruct((B, S, D), F32),
        compiler_params=_cparams(("parallel", "parallel")),
        name="xattn",
    )(x, kv, gpre, wq, wo, gpost)


FFN_CHUNK = 256


def _ffn_kernel(x_ref, gpre_ref, wg_ref, wu_ref, wo_ref, gpost_ref, o_ref):
    x = x_ref[...]
    h = _rms(x, gpre_ref[...]).astype(BF16)
    y = None
    for c in range(0, D_FF, FFN_CHUNK):
        g = _dot(h, wg_ref[:, c:c + FFN_CHUNK])
        u = _dot(h, wu_ref[:, c:c + FFN_CHUNK])
        a = (g * jax.nn.sigmoid(g) * u).astype(BF16)
        part = _dot(a, wo_ref[c:c + FFN_CHUNK, :])
        y = part if y is None else y + part
    o_ref[...] = x + _rms(y, gpost_ref[...])


def _ffn(x2d, gpre, wg, wu, wo, gpost, tm=256):
    M, D = x2d.shape
    full = lambda a: pl.BlockSpec(a.shape, lambda i: (0,) * a.ndim)
    return pl.pallas_call(
        _ffn_kernel,
        grid=(M // tm,),
        in_specs=[pl.BlockSpec((tm, D), lambda i: (i, 0)),
                  full(gpre), full(wg), full(wu), full(wo), full(gpost)],
        out_specs=pl.BlockSpec((tm, D), lambda i: (i, 0)),
        out_shape=jax.ShapeDtypeStruct((M, D), F32),
        compiler_params=_cparams(("parallel",)),
        name="ffn",
    )(x2d, gpre, wg, wu, wo, gpost)


def _alibi_slopes(n):
    return (2.0 ** (-8.0 * np.arange(1, n + 1) / n)).astype(np.float32)


def _rotate_half_cols(w):
    half = w.shape[-1] // 2
    return jnp.concatenate([-w[..., half:], w[..., :half]], axis=-1)


def _prep_weights(p):
    L = p["w_in"].shape[0]
    offs = np.concatenate([[0], np.cumsum(IN_SPLITS)])
    col = {n: p["w_in"][:, :, offs[k]:offs[k + 1]] for k, n in enumerate(IN_NAMES)}
    qk_scale = HEAD_DIM ** -0.5
    for n in ("nq", "sq", "fq"):
        col[n] = col[n] * qk_scale
    pad = jnp.zeros((L, D_MODEL, SMALL_COLS - 2 * MLA_ROPE_DIM - 24 - FOX_HEADS), F32)
    small = jnp.concatenate([col["mkr"], _rotate_half_cols(col["mkr"]), col["ng"], col["ff"], pad], -1)
    w_all = jnp.concatenate([col["mcq"], col["mckv"], col["nkc"], col["nvc"], small]
                            + [col[n] for n in MAIN_ORDER], axis=-1).astype(BF16)

    wq = p["mla_w_q_up"].reshape(L, MLA_Q_RANK, MLA_HEADS, MLA_QK_DIM)
    wq_nope = wq[..., :MLA_NOPE_DIM].reshape(L, MLA_Q_RANK, -1)
    wq_rot = wq[..., MLA_NOPE_DIM:]
    wq_all = jnp.concatenate([wq_nope, wq_rot.reshape(L, MLA_Q_RANK, -1),
                              _rotate_half_cols(wq_rot).reshape(L, MLA_Q_RANK, -1)], -1).astype(BF16)
    wkv = p["mla_w_kv_up"].reshape(L, MLA_KV_RANK, MLA_HEADS, MLA_NOPE_DIM + MLA_V_DIM)
    wkv_all = jnp.concatenate([wkv[..., :MLA_NOPE_DIM].reshape(L, MLA_KV_RANK, -1),
                               wkv[..., MLA_NOPE_DIM:].reshape(L, MLA_KV_RANK, -1)], -1).astype(BF16)

    half = NSA_CMP_LEN // 2
    cmp_w = p["nsa_cmp_w"].reshape(L, 2, 2, half * HEAD_DIM, HEAD_DIM).astype(BF16)
    cmp_pe = p["nsa_cmp_pe"].reshape(L, 2, 2, half * HEAD_DIM)

    xa_wq = (p["xa_w_q"] * qk_scale).reshape(L, D_MODEL, XA_HEADS, HEAD_DIM).transpose(0, 2, 1, 3).astype(BF16)
    xa_wo = p["xa_w_o"].reshape(L, XA_HEADS, HEAD_DIM, D_MODEL).astype(BF16)

    row = lambda a: a[:, None, :]
    return dict(
        w_all=w_all, mix_norm_pre=row(p["mix_norm_pre"]),
        wq_all=wq_all, wkv_all=wkv_all, mla_q_norm=row(p["mla_q_norm"]), mla_kv_norm=row(p["mla_kv_norm"]),
        cmp_w=cmp_w, cmp_pe=cmp_pe, swa_sinks=p["swa_sinks"], fox_f_bias=p["fox_f_bias"][:, :, None],
        w_gate=p["w_gate"].astype(BF16), w_branch=p["w_branch"].astype(BF16), w_out=p["w_out"].astype(BF16),
        mix_norm_post=row(p["mix_norm_post"]),
        xa_norm_pre=row(p["xa_norm_pre"]), xa_mem_norm=row(p["xa_mem_norm"]), xa_wq=xa_wq,
        xa_wkv=p["xa_w_kv"].astype(BF16), xa_wo=xa_wo, xa_norm_post=row(p["xa_norm_post"]),
        ffn_norm_pre=row(p["ffn_norm_pre"]), ffn_wg=p["ffn_w_in"][:, :, :D_FF].astype(BF16),
        ffn_wu=p["ffn_w_in"][:, :, D_FF:].astype(BF16), ffn_wo=p["ffn_w_out"].astype(BF16),
        ffn_norm_post=row(p["ffn_norm_post"]),
    )


ATTN_TQ = 256
ATTN_TK = 256
NSA_SEL_TQ = 64
BAND_TQ = 64
BAND_TK = 128


def _per_row(vals_hg, tq):
    Hk, G = vals_hg.shape
    return jnp.broadcast_to(vals_hg[:, :, None, None], (Hk, G, tq, 1)).reshape(Hk, G * tq, 1)


def _layer(x, w, c):
    B, S, D = x.shape
    M = B * S
    Hk, G, Dh = NSA_KV_HEADS, NSA_GROUP, HEAD_DIM

    z_mla, z_cmp, z_small, z = _inproj(
        x.reshape(M, D), w["mix_norm_pre"], w["w_all"],
        (MLA_Q_RANK + MLA_KV_RANK, 4 * Dh, SMALL_COLS, MAIN_COLS), (F32, F32, F32, BF16))
    offs = np.concatenate([[0], np.cumsum([512 if n in ("nq", "sq", "fq", "fk", "fv") else 128
                                           for n in MAIN_ORDER])])
    zc = {n: z[:, offs[k]:offs[k + 1]] for k, n in enumerate(MAIN_ORDER)}
    q_heads = lambda t: t.reshape(B, S, Hk, G, Dh).transpose(0, 2, 3, 1, 4)
    kv_heads = lambda t: t.reshape(B, S, Hk, Dh).transpose(0, 2, 1, 3)
    mh_q = lambda t, d: t.reshape(B, S, -1, d).transpose(0, 2, 1, 3)[:, :, None]
    mh_kv = lambda t, d: t.reshape(B, S, -1, d).transpose(0, 2, 1, 3)
    tokens = lambda o: o.transpose(0, 3, 1, 2, 4).reshape(M, -1)

    ng = z_small[:, 64:88].reshape(B, S, Hk, G, 3).transpose(4, 0, 2, 3, 1)[..., None]
    nq = q_heads(zc["nq"])
    rows = z_cmp.reshape(B, S, 2 * Hk, Dh).transpose(0, 2, 1, 3).reshape(B, 2 * Hk, S // NSA_CMP_STRIDE,
                                                                         NSA_CMP_STRIDE * Dh)
    kvc = _nsa_compress(rows, w["cmp_pe"], w["cmp_w"])
    oc, selb = _nsa_cmp(nq, kvc, c["pos_col"], c["pc_row"], c["nsa_slopes_cmp"], ng[0], c["overlap"])
    q_aug = jnp.concatenate(
        [nq, jnp.broadcast_to(selb[:, :, None], (B, Hk, G, S, selb.shape[-1]))], axis=-1)
    k_aug = jnp.concatenate(
        [kv_heads(zc["nks"]), jnp.broadcast_to(c["sel_onehot"], (B, Hk, S, c["sel_onehot"].shape[-1]))], axis=-1)
    o_sel = _attention(
        AttnCfg(tq=NSA_SEL_TQ, tk=ATTN_TK, alibi=True, gate=True, add=True, out_dtype=F32),
        q_aug, k_aug, kv_heads(zc["nvs"]), pos_col=c["pos_col"], pos_rows=c["pos_rows_attn"],
        slopes=c["nsa_slopes_sel"], gate=ng[1], add=oc)
    y_nsa = _attention(
        AttnCfg(tq=BAND_TQ, tk=BAND_TK, window=NSA_WINDOW, alibi=True, gate=True, add=True),
        nq, kv_heads(zc["nkw"]), kv_heads(zc["nvw"]), pos_col=c["pos_col"], pos_rows=c["pos_rows_band"],
        slopes=c["nsa_slopes_band"], gate=ng[2], add=o_sel)

    y_swa = _attention(
        AttnCfg(tq=BAND_TQ, tk=BAND_TK, window=SWA_WINDOW, alibi=True, sink=True),
        q_heads(zc["sq"]), kv_heads(zc["sk"]), kv_heads(zc["sv"]), pos_col=c["pos_col"],
        pos_rows=c["pos_rows_band"], slopes=c["swa_slopes_band"],
        sinks=_per_row(w["swa_sinks"].reshape(SWA_KV_HEADS, SWA_GROUP), BAND_TQ))

    qn, qr, kn, mv, kr = _mla_prep(z_mla, z_small, c["cos"], c["sin"], w["mla_q_norm"], w["mla_kv_norm"],
                                   w["wq_all"], w["wkv_all"])
    q_mla = jnp.concatenate([qn.reshape(B, S, MLA_HEADS, MLA_NOPE_DIM),
                             qr.reshape(B, S, MLA_HEADS, MLA_ROPE_DIM)], -1).transpose(0, 2, 1, 3)[:, :, None]
    k_mla = jnp.concatenate([kn.reshape(B, S, MLA_HEADS, MLA_NOPE_DIM),
                             jnp.broadcast_to(kr.reshape(B, S, 1, MLA_ROPE_DIM),
                                              (B, S, MLA_HEADS, MLA_ROPE_DIM))], -1).transpose(0, 2, 1, 3)
    y_mla = _attention(AttnCfg(tq=ATTN_TQ, tk=ATTN_TK), q_mla, k_mla, mh_kv(mv, MLA_V_DIM))

    f_logit = z_small[:, 88:96].reshape(B, S, FOX_HEADS).transpose(0, 2, 1)
    cum = _fox_cumsum(f_logit, w["fox_f_bias"])
    y_fox = _attention(
        AttnCfg(tq=ATTN_TQ, tk=ATTN_TK, decay=True),
        mh_q(zc["fq"], Dh), mh_kv(zc["fk"], Dh), mh_kv(zc["fv"], Dh),
        cum_col=cum[:, :, None, :, None], cum_rows=cum.reshape(B, FOX_HEADS, S // ATTN_TK, 1, ATTN_TK))

    y_cat = jnp.concatenate([tokens(y_nsa), tokens(y_swa), tokens(y_mla), tokens(y_fox)], axis=-1)
    x1 = _merge(x.reshape(M, D), y_cat, w["mix_norm_pre"], w["w_gate"], w["w_branch"], w["w_out"],
                w["mix_norm_post"])

    kv = _mem_kv(c["mem"], w["xa_mem_norm"], w["xa_wkv"])
    kv = kv.reshape(B, -1, 2 * XA_HEADS, Dh).transpose(0, 2, 1, 3)
    x2 = _xattn(x1.reshape(B, S, D), kv, w["xa_norm_pre"], w["xa_wq"], w["xa_wo"], w["xa_norm_post"])

    x3 = _ffn(x2.reshape(M, D), w["ffn_norm_pre"], w["ffn_wg"], w["ffn_wu"], w["ffn_wo"], w["ffn_norm_post"])
    return x3.reshape(B, S, D)


def _constants(mem, positions, S):
    B = positions.shape[0]
    pos = positions.astype(F32)
    half = MLA_ROPE_DIM // 2
    inv = (ROPE_THETA ** (-np.arange(half, dtype=np.float32) / half)).astype(np.float32)
    inv_row = jnp.asarray(np.tile(inv, 2 * MLA_HEADS)[None, :])
    cos, sin = _rope_tables(pos.reshape(B * S, 1), inv_row)
    nc = S // NSA_CMP_STRIDE
    n_valid = (S - NSA_CMP_LEN) // NSA_CMP_STRIDE + 1
    assert n_valid == nc - 1
    c_end = np.minimum(np.arange(nc) * NSA_CMP_STRIDE + NSA_CMP_LEN - 1, S - 1)
    n_sel = S // NSA_SEL_LEN
    c_start = np.arange(nc) * NSA_CMP_STRIDE
    sel_start = np.arange(n_sel) * NSA_SEL_LEN
    overlap = ((c_start[:, None] < sel_start[None, :] + NSA_SEL_LEN)
               & (c_start[:, None] + NSA_CMP_LEN - 1 >= sel_start[None, :])
               & (np.arange(nc)[:, None] < n_valid)).astype(np.float32)
    sel_onehot = (np.arange(S)[:, None] // NSA_SEL_LEN == np.arange(n_sel)[None, :]).astype(np.float32)
    nsa_slopes = jnp.asarray(_alibi_slopes(NSA_HEADS).reshape(NSA_KV_HEADS, NSA_GROUP))
    swa_slopes = jnp.asarray(_alibi_slopes(SWA_HEADS).reshape(SWA_KV_HEADS, SWA_GROUP))
    return dict(
        mem=mem, cos=cos, sin=sin,
        pos_col=pos[:, :, None],
        pos_rows_attn=pos.reshape(B, S // ATTN_TK, 1, ATTN_TK),
        pos_rows_band=pos.reshape(B, S // BAND_TK, 1, BAND_TK),
        pc_row=pos[:, c_end][:, None, :],
        overlap=jnp.asarray(overlap),
        sel_onehot=jnp.asarray(sel_onehot, dtype=BF16)[None, None],
        nsa_slopes_cmp=_per_row(nsa_slopes, NSA_CMP_TQ),
        nsa_slopes_sel=_per_row(nsa_slopes, NSA_SEL_TQ),
        nsa_slopes_band=_per_row(nsa_slopes, BAND_TQ),
        swa_slopes_band=_per_row(swa_slopes, BAND_TQ),
    )


def kernel(x, mem, positions, mix_norm_pre, w_in, nsa_cmp_w, nsa_cmp_pe, swa_sinks, mla_q_norm, mla_w_q_up,
           mla_kv_norm, mla_w_kv_up, fox_f_bias, w_branch, w_gate, w_out, mix_norm_post, xa_norm_pre,
           xa_mem_norm, xa_w_q, xa_w_kv, xa_w_o, xa_norm_post, ffn_norm_pre, ffn_w_in, ffn_w_out,
           ffn_norm_post):
    params = dict(
        mix_norm_pre=mix_norm_pre, w_in=w_in, nsa_cmp_w=nsa_cmp_w, nsa_cmp_pe=nsa_cmp_pe, swa_sinks=swa_sinks,
        mla_q_norm=mla_q_norm, mla_w_q_up=mla_w_q_up, mla_kv_norm=mla_kv_norm, mla_w_kv_up=mla_w_kv_up,
        fox_f_bias=fox_f_bias, w_branch=w_branch, w_gate=w_gate, w_out=w_out, mix_norm_post=mix_norm_post,
        xa_norm_pre=xa_norm_pre, xa_mem_norm=xa_mem_norm, xa_w_q=xa_w_q, xa_w_kv=xa_w_kv, xa_w_o=xa_w_o,
        xa_norm_post=xa_norm_post, ffn_norm_pre=ffn_norm_pre, ffn_w_in=ffn_w_in, ffn_w_out=ffn_w_out,
        ffn_norm_post=ffn_norm_post)
    S = x.shape[1]
    weights = _prep_weights(params)
    consts = _constants(mem, positions, S)

    def body(xc, w):
        return _layer(xc, w, consts), None

    out, _ = lax.scan(body, x, weights)
    return out
```

```python
import functools
from typing import NamedTuple

import numpy as np
import jax
import jax.numpy as jnp
from jax import lax
from jax.experimental import pallas as pl
from jax.experimental.pallas import tpu as pltpu

F32 = jnp.float32
BF16 = jnp.bfloat16

D_MODEL = 1024
HEAD_DIM = 64
NORM_EPS = 1e-6
NEG_INF = -1e30
FORCE_SCORE = 1e9
LANES = 128

NSA_HEADS = 8
NSA_KV_HEADS = 2
NSA_GROUP = NSA_HEADS // NSA_KV_HEADS
NSA_CMP_LEN = 32
NSA_CMP_STRIDE = 16
NSA_SEL_LEN = 64
NSA_TOP_N = 8
NSA_WINDOW = 512

SWA_HEADS = 8
SWA_KV_HEADS = 2
SWA_GROUP = SWA_HEADS // SWA_KV_HEADS
SWA_WINDOW = 128

MLA_HEADS = 8
MLA_Q_RANK = 256
MLA_KV_RANK = 128
MLA_NOPE_DIM = 64
MLA_ROPE_DIM = 32
MLA_V_DIM = 64
MLA_QK_DIM = MLA_NOPE_DIM + MLA_ROPE_DIM
ROPE_THETA = 10000.0

FOX_HEADS = 8
XA_HEADS = 4
N_BRANCH = 4
BRANCH_WIDTH = 512
D_FF = ((8 * D_MODEL + 3 * 256 - 1) // (3 * 256)) * 256

IN_SPLITS = (512, 128, 128, 128, 128, 128, 128, 24,
             512, 128, 128,
             MLA_Q_RANK, MLA_KV_RANK, MLA_ROPE_DIM,
             512, 512, 512, FOX_HEADS)
IN_NAMES = ("nq", "nkc", "nvc", "nks", "nvs", "nkw", "nvw", "ng",
            "sq", "sk", "sv", "mcq", "mckv", "mkr", "fq", "fk", "fv", "ff")

VMEM_LIMIT_BYTES = 56 * 1024 * 1024

MAIN_ORDER = ("nq", "nks", "nvs", "nkw", "nvw", "sq", "sk", "sv", "fq", "fk", "fv")
MAIN_COLS = 3328
SMALL_COLS = 128


def _cparams(sem):
    return pltpu.CompilerParams(dimension_semantics=sem, vmem_limit_bytes=VMEM_LIMIT_BYTES)


def _rms(x, g):
    return x * lax.rsqrt(jnp.mean(x * x, axis=-1, keepdims=True) + NORM_EPS) * g


def _dot(a, b):
    return jnp.dot(a, b, preferred_element_type=F32)


def _dot_nt(a, b, precision=None):
    return lax.dot_general(a, b, (((1,), (1,)), ((), ())), preferred_element_type=F32,
                           precision=precision)


def _rep(x, n):
    return x if n == 1 else jnp.concatenate([x] * n, axis=1)


def _inproj_kernel(x_ref, g_ref, w_ref, *o_refs, widths):
    h = _rms(x_ref[...], g_ref[...]).astype(BF16)
    c0 = 0
    for o_ref, width in zip(o_refs, widths):
        for c in range(0, width, 512):
            w = min(512, width - c)
            o_ref[:, c:c + w] = _dot(h, w_ref[:, c0 + c:c0 + c + w]).astype(o_ref.dtype)
        c0 += width


def _inproj(x2d, g, w, widths, dtypes, tm=512):
    M, D = x2d.shape
    N = w.shape[1]
    assert sum(widths) == N and M % tm == 0
    return pl.pallas_call(
        functools.partial(_inproj_kernel, widths=widths),
        grid=(M // tm,),
        in_specs=[pl.BlockSpec((tm, D), lambda i: (i, 0)),
                  pl.BlockSpec((1, D), lambda i: (0, 0)),
                  pl.BlockSpec((D, N), lambda i: (0, 0))],
        out_specs=[pl.BlockSpec((tm, wd), lambda i: (i, 0)) for wd in widths],
        out_shape=[jax.ShapeDtypeStruct((M, wd), dt) for wd, dt in zip(widths, dtypes)],
        compiler_params=_cparams(("parallel",)),
        name="inproj",
    )(x2d, g, w)


def _rope_table_kernel(pos_ref, inv_ref, cos_ref, sin_ref):
    ang = pos_ref[...] * inv_ref[...]
    cos_ref[...] = jnp.cos(ang)
    sin_ref[...] = jnp.sin(ang)


def _rope_tables(pos_col, inv_row, tm=512):
    M = pos_col.shape[0]
    N = inv_row.shape[1]
    return pl.pallas_call(
        _rope_table_kernel,
        grid=(M // tm,),
        in_specs=[pl.BlockSpec((tm, 1), lambda i: (i, 0)),
                  pl.BlockSpec((1, N), lambda i: (0, 0))],
        out_specs=[pl.BlockSpec((tm, N), lambda i: (i, 0))] * 2,
        out_shape=[jax.ShapeDtypeStruct((M, N), F32)] * 2,
        compiler_params=_cparams(("parallel",)),
        name="rope_tables",
    )(pos_col, inv_row)


def _mla_prep_kernel(z_ref, zs_ref, cos_ref, sin_ref, qg_ref, kvg_ref, wq_ref, wkv_ref,
                     qn_ref, qr_ref, kn_ref, v_ref, kr_ref):
    nq = MLA_HEADS * MLA_NOPE_DIM
    nr = MLA_HEADS * MLA_ROPE_DIM
    scale = MLA_QK_DIM ** -0.5
    z = z_ref[...]
    cq = _rms(z[:, :MLA_Q_RANK], qg_ref[...]).astype(BF16)
    ckv = _rms(z[:, MLA_Q_RANK:], kvg_ref[...]).astype(BF16)
    cos = cos_ref[...]
    sin = sin_ref[...]
    qn_ref[...] = (_dot(cq, wq_ref[:, :nq]) * scale).astype(BF16)
    q_rot = _dot(cq, wq_ref[:, nq:nq + nr])
    q_rot_half = _dot(cq, wq_ref[:, nq + nr:])
    qr_ref[...] = ((q_rot * cos + q_rot_half * sin) * scale).astype(BF16)
    kn_ref[...] = _dot(ckv, wkv_ref[:, :nq]).astype(BF16)
    v_ref[...] = _dot(ckv, wkv_ref[:, nq:]).astype(BF16)
    zs = zs_ref[...]
    r = MLA_ROPE_DIM
    kr_ref[...] = (zs[:, :r] * cos[:, :r] + zs[:, r:2 * r] * sin[:, :r]).astype(BF16)


def _mla_prep(z_mla, z_small, cos, sin, qg, kvg, wq, wkv, tm=512):
    M = z_mla.shape[0]
    nq = MLA_HEADS * MLA_NOPE_DIM
    nr = MLA_HEADS * MLA_ROPE_DIM
    row = lambda n: pl.BlockSpec((tm, n), lambda i: (i, 0))
    full = lambda a: pl.BlockSpec(a.shape, lambda i: (0,) * a.ndim)
    return pl.pallas_call(
        _mla_prep_kernel,
        grid=(M // tm,),
        in_specs=[row(z_mla.shape[1]), row(SMALL_COLS), row(nr), row(nr),
                  full(qg), full(kvg), full(wq), full(wkv)],
        out_specs=[row(nq), row(nr), row(nq), row(nq), row(MLA_ROPE_DIM)],
        out_shape=[jax.ShapeDtypeStruct((M, n), BF16) for n in (nq, nr, nq, nq, MLA_ROPE_DIM)],
        compiler_params=_cparams(("parallel",)),
        name="mla_prep",
    )(z_mla, z_small, cos, sin, qg, kvg, wq, wkv)


CUM_CHUNK = 512


def _fox_cumsum_kernel(f_ref, b_ref, u_ref, o_ref):
    x = f_ref[0] + b_ref[...]
    lf = jnp.minimum(x, 0.0) - jnp.log1p(jnp.exp(-jnp.abs(x)))
    S = lf.shape[1]
    carry = jnp.zeros((lf.shape[0], 1), F32)
    for c in range(0, S, CUM_CHUNK):
        loc = jnp.dot(lf[:, c:c + CUM_CHUNK], u_ref[...], preferred_element_type=F32,
                      precision=lax.Precision.HIGHEST) + carry
        o_ref[0, :, c:c + CUM_CHUNK] = loc
        carry = loc[:, CUM_CHUNK - 1:CUM_CHUNK]


def _fox_cumsum(f_bhs, bias_col):
    B, H, S = f_bhs.shape
    assert S % CUM_CHUNK == 0
    upper = jnp.asarray(np.triu(np.ones((CUM_CHUNK, CUM_CHUNK), np.float32)))
    return pl.pallas_call(
        _fox_cumsum_kernel,
        grid=(B,),
        in_specs=[pl.BlockSpec((1, H, S), lambda b: (b, 0, 0)),
                  pl.BlockSpec((H, 1), lambda b: (0, 0)),
                  pl.BlockSpec((CUM_CHUNK, CUM_CHUNK), lambda b: (0, 0))],
        out_specs=pl.BlockSpec((1, H, S), lambda b: (b, 0, 0)),
        out_shape=jax.ShapeDtypeStruct((B, H, S), F32),
        compiler_params=_cparams(("parallel",)),
        name="fox_cumsum",
    )(f_bhs, bias_col, upper)


def _nsa_compress_kernel(r_ref, pe_ref, w_ref, o_ref):
    r = r_ref[0, 0]
    first = _dot((r + pe_ref[0, 0:1, :]).astype(BF16), w_ref[0, 0])
    second = _dot((r + pe_ref[0, 1:2, :]).astype(BF16), w_ref[0, 1])
    nc = r.shape[0]
    nxt = pltpu.roll(second, nc - 1, 0)
    row = lax.broadcasted_iota(jnp.int32, (nc, 1), 0)
    o_ref[0, 0] = jnp.where(row < nc - 1, first + nxt, 0.0).astype(o_ref.dtype)


def _nsa_compress(rows, pe, w):
    B, F, NC, K = rows.shape
    return pl.pallas_call(
        _nsa_compress_kernel,
        grid=(B, F),
        in_specs=[pl.BlockSpec((1, 1, NC, K), lambda b, f: (b, f, 0, 0)),
                  pl.BlockSpec((1, 2, K), lambda b, f: (f // NSA_KV_HEADS, 0, 0)),
                  pl.BlockSpec((1, 2, K, HEAD_DIM), lambda b, f: (f // NSA_KV_HEADS, 0, 0, 0))],
        out_specs=pl.BlockSpec((1, 1, NC, HEAD_DIM), lambda b, f: (b, f, 0, 0)),
        out_shape=jax.ShapeDtypeStruct((B, F, NC, HEAD_DIM), BF16),
        compiler_params=_cparams(("parallel", "parallel")),
        name="nsa_compress",
    )(rows, pe, w)


NSA_CMP_TQ = 256


def _nsa_cmp_kernel(q_ref, kc_ref, vc_ref, pq_ref, pc_ref, sl_ref, gate_ref, ovt_ref,
                    oc_ref, selt_ref):
    G, tq, Dh = q_ref.shape[2:]
    R = G * tq
    i = pl.program_id(2)
    q = q_ref[0, 0].reshape(R, Dh)
    kc = kc_ref[0, 0]
    nc = kc.shape[0]
    n_sel = ovt_ref.shape[0]
    s = _dot_nt(q, kc)
    pq = jnp.concatenate([pq_ref[0]] * G, axis=0)
    s = s - sl_ref[0] * jnp.abs(pq - pc_ref[0])
    t_row = (i * tq + lax.broadcasted_iota(jnp.int32, (G, tq, 1), 1)).reshape(R, 1)
    c_col = lax.broadcasted_iota(jnp.int32, (1, nc), 1)
    c_end = jnp.where(c_col < nc - 1, c_col * NSA_CMP_STRIDE + (NSA_CMP_LEN - 1), jnp.int32(2 ** 30))
    s = jnp.where(c_end <= t_row, s, NEG_INF)
    m = jnp.max(s, axis=-1, keepdims=True)
    e = jnp.exp(s - m)
    l = jnp.sum(e, axis=-1, keepdims=True)
    p = e * jnp.where(m > 0.5 * NEG_INF, 1.0 / l, 0.0)
    o = _dot(p.astype(BF16), vc_ref[0, 0])
    gate = jax.nn.sigmoid(gate_ref[0, 0].reshape(R, 1))
    oc_ref[0, 0] = (gate * o).reshape(G, tq, Dh)
    p_sum = jnp.sum(p.reshape(G, tq, nc), axis=0)
    imp = _dot_nt(ovt_ref[...], p_sum, precision=lax.Precision.HIGHEST)
    t_q = i * tq + lax.broadcasted_iota(jnp.int32, (1, tq), 1)
    jt = lax.shift_right_logical(t_q, NSA_SEL_LEN.bit_length() - 1)
    j = lax.broadcasted_iota(jnp.int32, (n_sel, 1), 0)
    forced = (j == 0) | (j == jt) | (j == jt - 1)
    causal = j <= jt
    val = jnp.where(forced, FORCE_SCORE, imp)
    val = jnp.where(causal, val, NEG_INF)
    sel = jnp.zeros((n_sel, tq), jnp.int32)
    for _ in range(min(NSA_TOP_N, n_sel)):
        best = jnp.max(val, axis=0, keepdims=True)
        idx = jnp.min(jnp.where(val == best, j, n_sel), axis=0, keepdims=True)
        pick = j == idx
        sel = jnp.where(pick, 1, sel)
        val = jnp.where(pick, -jnp.inf, val)
    keep = (sel > 0) & causal
    selt_ref[0, 0] = jnp.where(keep, 0.0, NEG_INF).astype(selt_ref.dtype)


def _nsa_cmp(q, kvc, pos_col, pc_row, slopes, gate, overlap_t):
    B, Hk, G, S, Dh = q.shape
    NC = kvc.shape[2]
    n_sel = overlap_t.shape[0]
    assert NSA_SEL_LEN & (NSA_SEL_LEN - 1) == 0
    tq = NSA_CMP_TQ
    R = G * tq
    return pl.pallas_call(
        _nsa_cmp_kernel,
        grid=(B, Hk, S // tq),
        in_specs=[pl.BlockSpec((1, 1, G, tq, Dh), lambda b, h, i: (b, h, 0, i, 0)),
                  pl.BlockSpec((1, 1, NC, Dh), lambda b, h, i: (b, h, 0, 0)),
                  pl.BlockSpec((1, 1, NC, Dh), lambda b, h, i: (b, h + Hk, 0, 0)),
                  pl.BlockSpec((1, tq, 1), lambda b, h, i: (b, i, 0)),
                  pl.BlockSpec((1, 1, NC), lambda b, h, i: (b, 0, 0)),
                  pl.BlockSpec((1, R, 1), lambda b, h, i: (h, 0, 0)),
                  pl.BlockSpec((1, 1, G, tq, 1), lambda b, h, i: (b, h, 0, i, 0)),
                  pl.BlockSpec((n_sel, NC), lambda b, h, i: (0, 0))],
        out_specs=[pl.BlockSpec((1, 1, G, tq, Dh), lambda b, h, i: (b, h, 0, i, 0)),
                   pl.BlockSpec((1, 1, n_sel, tq), lambda b, h, i: (b, h, 0, i))],
        out_shape=[jax.ShapeDtypeStruct((B, Hk, G, S, Dh), F32),
                   jax.ShapeDtypeStruct((B, Hk, n_sel, S), BF16)],
        compiler_params=_cparams(("parallel", "parallel", "parallel")),
        name="nsa_cmp",
    )(q, kvc, kvc, pos_col, pc_row, slopes, gate, overlap_t)


class AttnCfg(NamedTuple):
    tq: int
    tk: int = 0
    window: int = 0
    alibi: bool = False
    decay: bool = False
    sink: bool = False
    gate: bool = False
    add: bool = False
    out_dtype: object = BF16


def _epilogue(cfg, out, gate_ref, add_ref, o_ref, G, tq):
    R, Dv = out.shape
    if cfg.gate:
        out = out * jax.nn.sigmoid(gate_ref[0, 0].reshape(R, 1))
    if cfg.add:
        out = out + add_ref[0, 0].reshape(R, Dv)
    o_ref[0, 0] = out.reshape(G, tq, Dv).astype(o_ref.dtype)


def _flash_kernel(*refs, cfg):
    it = iter(refs)
    q_ref, k_ref, v_ref = next(it), next(it), next(it)
    pq_ref = pk_ref = sl_ref = cq_ref = ck_ref = gate_ref = add_ref = None
    if cfg.alibi:
        pq_ref, pk_ref, sl_ref = next(it), next(it), next(it)
    if cfg.decay:
        cq_ref, ck_ref = next(it), next(it)
    if cfg.gate:
        gate_ref = next(it)
    if cfg.add:
        add_ref = next(it)
    o_ref = next(it)
    m_s, l_s, acc_s = next(it), next(it), next(it)
    pq_s = next(it) if cfg.alibi else None
    cq_s = next(it) if cfg.decay else None

    G, tq, Dq = q_ref.shape[2:]
    Dv = v_ref.shape[3]
    tk = cfg.tk
    nrep = tk // LANES
    R = G * tq
    i = pl.program_id(2)
    q = q_ref[0, 0].reshape(R, Dq)

    m_s[...] = jnp.full((R, LANES), NEG_INF, F32)
    l_s[...] = jnp.zeros((R, LANES), F32)
    acc_s[...] = jnp.zeros((R, Dv), F32)
    if cfg.alibi:
        pq_s[...] = jnp.broadcast_to(jnp.concatenate([pq_ref[0]] * G, axis=0), (R, LANES))
    if cfg.decay:
        cq_s[...] = jnp.broadcast_to(cq_ref[0, 0].reshape(R, 1), (R, LANES))

    def step(j, masked):
        k0 = pl.multiple_of(j * tk, tk)
        s = _dot_nt(q, k_ref[0, 0, pl.ds(k0, tk), :])
        if cfg.alibi:
            s = s - _rep(sl_ref[0], nrep) * jnp.abs(_rep(pq_s[...], nrep) - pk_ref[0, j])
        if cfg.decay:
            s = s + (_rep(cq_s[...], nrep) - ck_ref[0, 0, j])
        if masked:
            t_row = (i * tq + lax.broadcasted_iota(jnp.int32, (G, tq, tk), 1)).reshape(R, tk)
            col = j * tk + lax.broadcasted_iota(jnp.int32, (R, tk), 1)
            s = jnp.where(col <= t_row, s, NEG_INF)
        m_prev = m_s[...]
        m_new = jnp.maximum(m_prev, jnp.max(s, axis=-1, keepdims=True))
        alpha = jnp.exp(m_prev - m_new)
        p = jnp.exp(s - _rep(m_new, nrep))
        l_s[...] = alpha * l_s[...] + jnp.sum(p, axis=-1, keepdims=True)
        acc_s[...] = alpha[:, :Dv] * acc_s[...] + _dot(p.astype(BF16), v_ref[0, 0, pl.ds(k0, tk), :])
        m_s[...] = m_new

    j_hi = ((i + 1) * tq - 1) // tk

    def body(j, c):
        step(j, False)
        return c
    lax.fori_loop(0, j_hi, body, 0)
    step(j_hi, True)

    out = acc_s[...] / l_s[...][:, :Dv]
    _epilogue(cfg, out, gate_ref, add_ref, o_ref, G, tq)


def _flash(cfg, q, k, v, *, pos_col=None, pos_rows=None, slopes=None, cum_col=None, cum_rows=None,
           gate=None, add=None):
    B, Hk, G, S, Dq = q.shape
    Dv = v.shape[3]
    tq, tk = cfg.tq, cfg.tk
    assert S % tq == 0 and S % tk == 0 and tk % tq == 0 and tk % LANES == 0 and Dv <= LANES
    R = G * tq
    nk = S // tk
    qmap = lambda b, h, i: (b, h, 0, i, 0)
    kvmap = lambda b, h, i: (b, h, 0, 0)
    args = [q, k, v]
    specs = [pl.BlockSpec((1, 1, G, tq, Dq), qmap),
             pl.BlockSpec((1, 1, S, Dq), kvmap),
             pl.BlockSpec((1, 1, S, Dv), kvmap)]
    scratch = [pltpu.VMEM((R, LANES), F32), pltpu.VMEM((R, LANES), F32), pltpu.VMEM((R, Dv), F32)]
    if cfg.alibi:
        args += [pos_col, pos_rows, slopes]
        specs += [pl.BlockSpec((1, tq, 1), lambda b, h, i: (b, i, 0)),
                  pl.BlockSpec((1, nk, 1, tk), lambda b, h, i: (b, 0, 0, 0)),
                  pl.BlockSpec((1, R, LANES), lambda b, h, i: (h, 0, 0))]
        scratch.append(pltpu.VMEM((R, LANES), F32))
    if cfg.decay:
        args += [cum_col, cum_rows]
        specs += [pl.BlockSpec((1, 1, G, tq, 1), qmap),
                  pl.BlockSpec((1, 1, nk, 1, tk), lambda b, h, i: (b, h, 0, 0, 0))]
        scratch.append(pltpu.VMEM((R, LANES), F32))
    if cfg.gate:
        args += [gate]
        specs += [pl.BlockSpec((1, 1, G, tq, 1), qmap)]
    if cfg.add:
        args += [add]
        specs += [pl.BlockSpec((1, 1, G, tq, Dv), qmap)]
    return pl.pallas_call(
        functools.partial(_flash_kernel, cfg=cfg),
        grid=(B, Hk, S // tq),
        in_specs=specs,
        out_specs=pl.BlockSpec((1, 1, G, tq, Dv), qmap),
        out_shape=jax.ShapeDtypeStruct((B, Hk, G, S, Dv), cfg.out_dtype),
        scratch_shapes=scratch,
        compiler_params=_cparams(("parallel", "parallel", "parallel")),
        name="flash",
    )(*args)


def _band_kernel(*refs, cfg):
    it = iter(refs)
    q_ref, k_ref, v_ref, pq_ref, pk_ref, sl_ref = (next(it) for _ in range(6))
    sink_ref = next(it) if cfg.sink else None
    gate_ref = next(it) if cfg.gate else None
    add_ref = next(it) if cfg.add else None
    o_ref = next(it)

    G, tq, Dq = q_ref.shape[2:]
    R = G * tq
    W = cfg.window
    span = W + tq
    i = pl.program_id(2)
    start = pl.multiple_of(jnp.maximum(i * tq - W, 0), LANES)
    blk0 = lax.shift_right_logical(start, LANES.bit_length() - 1)
    q = q_ref[0, 0].reshape(R, Dq)
    s = _dot_nt(q, k_ref[0, 0, pl.ds(start, span), :])
    pq = jnp.concatenate([pq_ref[0]] * G, axis=0)
    slope = sl_ref[0]
    d = (lax.broadcasted_iota(jnp.int32, (G, tq, LANES), 1)
         - lax.broadcasted_iota(jnp.int32, (G, tq, LANES), 2)).reshape(R, LANES)
    off = i * tq - start
    chunks = []
    for c in range(span // LANES):
        sc = s[:, c * LANES:(c + 1) * LANES]
        sc = sc - slope * jnp.abs(pq - pk_ref[0, blk0 + c])
        lo = c * LANES - off
        sc = jnp.where(d >= lo, jnp.where(d < lo + W, sc, NEG_INF), NEG_INF)
        chunks.append(sc)
    s = jnp.concatenate(chunks, axis=1)
    m = jnp.max(s, axis=-1, keepdims=True)
    e = jnp.exp(s - m)
    l = jnp.sum(e, axis=-1, keepdims=True)
    acc = _dot(e.astype(BF16), v_ref[0, 0, pl.ds(start, span), :])
    if cfg.sink:
        sink = sink_ref[0]
        m_f = jnp.maximum(m, sink)
        w = jnp.exp(m - m_f)
        out = acc * (w / (l * w + jnp.exp(sink - m_f)))
    else:
        out = acc / l
    _epilogue(cfg, out, gate_ref, add_ref, o_ref, G, tq)


def _band(cfg, q, k, v, *, pos_col, pos_rows, slopes, sinks=None, gate=None, add=None):
    B, Hk, G, S, Dq = q.shape
    Dv = v.shape[3]
    tq = cfg.tq
    assert tq == LANES and cfg.window % LANES == 0 and S % tq == 0 and S >= cfg.window + tq
    R = G * tq
    qmap = lambda b, h, i: (b, h, 0, i, 0)
    kvmap = lambda b, h, i: (b, h, 0, 0)
    args = [q, k, v, pos_col, pos_rows, slopes]
    specs = [pl.BlockSpec((1, 1, G, tq, Dq), qmap),
             pl.BlockSpec((1, 1, S, Dq), kvmap),
             pl.BlockSpec((1, 1, S, Dv), kvmap),
             pl.BlockSpec((1, tq, 1), lambda b, h, i: (b, i, 0)),
             pl.BlockSpec((1, S // LANES, 1, LANES), lambda b, h, i: (b, 0, 0, 0)),
             pl.BlockSpec((1, R, 1), lambda b, h, i: (h, 0, 0))]
    if cfg.sink:
        args += [sinks]
        specs += [pl.BlockSpec((1, R, 1), lambda b, h, i: (h, 0, 0))]
    if cfg.gate:
        args += [gate]
        specs += [pl.BlockSpec((1, 1, G, tq, 1), qmap)]
    if cfg.add:
        args += [add]
        specs += [pl.BlockSpec((1, 1, G, tq, Dv), qmap)]
    return pl.pallas_call(
        functools.partial(_band_kernel, cfg=cfg),
        grid=(B, Hk, S // tq),
        in_specs=specs,
        out_specs=pl.BlockSpec((1, 1, G, tq, Dv), qmap),
        out_shape=jax.ShapeDtypeStruct((B, Hk, G, S, Dv), cfg.out_dtype),
        compiler_params=_cparams(("parallel", "parallel", "parallel")),
        name="band",
    )(*args)


def _merge_kernel(x_ref, y_ref, gpre_ref, wg_ref, wb_ref, wo_ref, gpost_ref, o_ref):
    x = x_ref[...]
    h = _rms(x, gpre_ref[...]).astype(BF16)
    merged = None
    for b in range(N_BRANCH):
        gate = jax.nn.sigmoid(_dot(h, wg_ref[b]))
        br = _dot(y_ref[:, b * BRANCH_WIDTH:(b + 1) * BRANCH_WIDTH], wb_ref[b])
        merged = gate * br if merged is None else merged + gate * br
    y = _dot(merged.astype(BF16), wo_ref[...])
    o_ref[...] = x + _rms(y, gpost_ref[...])


def _merge(x2d, y, gpre, wg, wb, wo, gpost, tm=256):
    M, D = x2d.shape
    full = lambda a: pl.BlockSpec(a.shape, lambda i: (0,) * a.ndim)
    return pl.pallas_call(
        _merge_kernel,
        grid=(M // tm,),
        in_specs=[pl.BlockSpec((tm, D), lambda i: (i, 0)),
                  pl.BlockSpec((tm, y.shape[1]), lambda i: (i, 0)),
                  full(gpre), full(wg), full(wb), full(wo), full(gpost)],
        out_specs=pl.BlockSpec((tm, D), lambda i: (i, 0)),
        out_shape=jax.ShapeDtypeStruct((M, D), F32),
        compiler_params=_cparams(("parallel",)),
        name="merge",
    )(x2d, y, gpre, wg, wb, wo, gpost)


def _mem_kv_kernel(mem_ref, g_ref, w_ref, o_ref):
    o_ref[0] = _dot(_rms(mem_ref[0], g_ref[...]).astype(BF16), w_ref[...]).astype(BF16)


def _mem_kv(mem, g, w):
    B, Mm, D = mem.shape
    N = w.shape[1]
    return pl.pallas_call(
        _mem_kv_kernel,
        grid=(B,),
        in_specs=[pl.BlockSpec((1, Mm, D), lambda b: (b, 0, 0)),
                  pl.BlockSpec((1, D), lambda b: (0, 0)),
                  pl.BlockSpec((D, N), lambda b: (0, 0))],
        out_specs=pl.BlockSpec((1, Mm, N), lambda b: (b, 0, 0)),
        out_shape=jax.ShapeDtypeStruct((B, Mm, N), BF16),
        compiler_params=_cparams(("parallel",)),
        name="mem_kv",
    )(mem, g, w)


def _xattn_kernel(x_ref, kv_ref, gpre_ref, wq_ref, wo_ref, gpost_ref, o_ref):
    x = x_ref[0]
    h = _rms(x, gpre_ref[...]).astype(BF16)
    y = None
    for hd in range(XA_HEADS):
        q = _dot(h, wq_ref[hd]).astype(BF16)
        s = _dot_nt(q, kv_ref[0, hd])
        m = jnp.max(s, axis=-1, keepdims=True)
        e = jnp.exp(s - m)
        p = e / jnp.sum(e, axis=-1, keepdims=True)
        o = _dot(p.astype(BF16), kv_ref[0, XA_HEADS + hd]).astype(BF16)
        part = _dot(o, wo_ref[hd])
        y = part if y is None else y + part
    o_ref[0] = x + _rms(y, gpost_ref[...])


def _xattn(x, kv, gpre, wq, wo, gpost, tm=512):
    B, S, D = x.shape
    full = lambda a: pl.BlockSpec(a.shape, lambda b, i: (0,) * a.ndim)
    return pl.pallas_call(
        _xattn_kernel,
        grid=(B, S // tm),
        in_specs=[pl.BlockSpec((1, tm, D), lambda b, i: (b, i, 0)),
                  pl.BlockSpec((1,) + kv.shape[1:], lambda b, i: (b, 0, 0, 0)),
                  full(gpre), full(wq), full(wo), full(gpost)],
        out_specs=pl.BlockSpec((1, tm, D), lambda b, i: (b, i, 0)),
        out_shape=jax.ShapeDtypeStruct((B, S, D), F32),
        compiler_params=_cparams(("parallel", "parallel")),
        name="xattn",
    )(x, kv, gpre, wq, wo, gpost)


FFN_CHUNK = 256


def _ffn_kernel(x_ref, gpre_ref, wg_ref, wu_ref, wo_ref, gpost_ref, o_ref):
    x = x_ref[...]
    h = _rms(x, gpre_ref[...]).astype(BF16)
    y = None
    for c in range(0, D_FF, FFN_CHUNK):
        g = _dot(h, wg_ref[:, c:c + FFN_CHUNK])
        u = _dot(h, wu_ref[:, c:c + FFN_CHUNK])
        a = (g * jax.nn.sigmoid(g) * u).astype(BF16)
        part = _dot(a, wo_ref[c:c + FFN_CHUNK, :])
        y = part if y is None else y + part
    o_ref[...] = x + _rms(y, gpost_ref[...])


def _ffn(x2d, gpre, wg, wu, wo, gpost, tm=256):
    M, D = x2d.shape
    full = lambda a: pl.BlockSpec(a.shape, lambda i: (0,) * a.ndim)
    return pl.pallas_call(
        _ffn_kernel,
        grid=(M // tm,),
        in_specs=[pl.BlockSpec((tm, D), lambda i: (i, 0)),
                  full(gpre), full(wg), full(wu), full(wo), full(gpost)],
        out_specs=pl.BlockSpec((tm, D), lambda i: (i, 0)),
        out_shape=jax.ShapeDtypeStruct((M, D), F32),
        compiler_params=_cparams(("parallel",)),
        name="ffn",
    )(x2d, gpre, wg, wu, wo, gpost)


def _alibi_slopes(n):
    return (2.0 ** (-8.0 * np.arange(1, n + 1) / n)).astype(np.float32)


def _rotate_half_cols(w):
    half = w.shape[-1] // 2
    return jnp.concatenate([-w[..., half:], w[..., :half]], axis=-1)


def _prep_weights(p):
    L = p["w_in"].shape[0]
    offs = np.concatenate([[0], np.cumsum(IN_SPLITS)])
    col = {n: p["w_in"][:, :, offs[k]:offs[k + 1]] for k, n in enumerate(IN_NAMES)}
    qk_scale = HEAD_DIM ** -0.5
    for n in ("nq", "sq", "fq"):
        col[n] = col[n] * qk_scale
    pad = jnp.zeros((L, D_MODEL, SMALL_COLS - 2 * MLA_ROPE_DIM - 24 - FOX_HEADS), F32)
    small = jnp.concatenate([col["mkr"], _rotate_half_cols(col["mkr"]), col["ng"], col["ff"], pad], -1)
    w_all = jnp.concatenate([col["mcq"], col["mckv"], col["nkc"], col["nvc"], small]
                            + [col[n] for n in MAIN_ORDER], axis=-1).astype(BF16)

    wq = p["mla_w_q_up"].reshape(L, MLA_Q_RANK, MLA_HEADS, MLA_QK_DIM)
    wq_nope = wq[..., :MLA_NOPE_DIM].reshape(L, MLA_Q_RANK, -1)
    wq_rot = wq[..., MLA_NOPE_DIM:]
    wq_all = jnp.concatenate([wq_nope, wq_rot.reshape(L, MLA_Q_RANK, -1),
                              _rotate_half_cols(wq_rot).reshape(L, MLA_Q_RANK, -1)], -1).astype(BF16)
    wkv = p["mla_w_kv_up"].reshape(L, MLA_KV_RANK, MLA_HEADS, MLA_NOPE_DIM + MLA_V_DIM)
    wkv_all = jnp.concatenate([wkv[..., :MLA_NOPE_DIM].reshape(L, MLA_KV_RANK, -1),
                               wkv[..., MLA_NOPE_DIM:].reshape(L, MLA_KV_RANK, -1)], -1).astype(BF16)

    half = NSA_CMP_LEN // 2
    cmp_w = p["nsa_cmp_w"].reshape(L, 2, 2, half * HEAD_DIM, HEAD_DIM).astype(BF16)
    cmp_pe = p["nsa_cmp_pe"].reshape(L, 2, 2, half * HEAD_DIM)

    xa_wq = (p["xa_w_q"] * qk_scale).reshape(L, D_MODEL, XA_HEADS, HEAD_DIM).transpose(0, 2, 1, 3).astype(BF16)
    xa_wo = p["xa_w_o"].reshape(L, XA_HEADS, HEAD_DIM, D_MODEL).astype(BF16)

    row = lambda a: a[:, None, :]
    return dict(
        w_all=w_all, mix_norm_pre=row(p["mix_norm_pre"]),
        wq_all=wq_all, wkv_all=wkv_all, mla_q_norm=row(p["mla_q_norm"]), mla_kv_norm=row(p["mla_kv_norm"]),
        cmp_w=cmp_w, cmp_pe=cmp_pe, swa_sinks=p["swa_sinks"], fox_f_bias=p["fox_f_bias"][:, :, None],
        w_gate=p["w_gate"].astype(BF16), w_branch=p["w_branch"].astype(BF16), w_out=p["w_out"].astype(BF16),
        mix_norm_post=row(p["mix_norm_post"]),
        xa_norm_pre=row(p["xa_norm_pre"]), xa_mem_norm=row(p["xa_mem_norm"]), xa_wq=xa_wq,
        xa_wkv=p["xa_w_kv"].astype(BF16), xa_wo=xa_wo, xa_norm_post=row(p["xa_norm_post"]),
        ffn_norm_pre=row(p["ffn_norm_pre"]), ffn_wg=p["ffn_w_in"][:, :, :D_FF].astype(BF16),
        ffn_wu=p["ffn_w_in"][:, :, D_FF:].astype(BF16), ffn_wo=p["ffn_w_out"].astype(BF16),
        ffn_norm_post=row(p["ffn_norm_post"]),
    )


FLASH_TQ = 512
FLASH_TK = 512
BAND_TQ = LANES


def _per_row(vals_hg, tq, lanes=1):
    Hk, G = vals_hg.shape
    return jnp.broadcast_to(vals_hg[:, :, None, None], (Hk, G, tq, lanes)).reshape(Hk, G * tq, lanes)


def _layer(x, w, c):
    B, S, D = x.shape
    M = B * S
    Hk, G, Dh = NSA_KV_HEADS, NSA_GROUP, HEAD_DIM

    z_mla, z_cmp, z_small, z = _inproj(
        x.reshape(M, D), w["mix_norm_pre"], w["w_all"],
        (MLA_Q_RANK + MLA_KV_RANK, 4 * Dh, SMALL_COLS, MAIN_COLS), (F32, F32, F32, BF16))
    offs = np.concatenate([[0], np.cumsum([512 if n in ("nq", "sq", "fq", "fk", "fv") else 128
                                           for n in MAIN_ORDER])])
    zc = {n: z[:, offs[k]:offs[k + 1]] for k, n in enumerate(MAIN_ORDER)}
    q_heads = lambda t: t.reshape(B, S, Hk, G, Dh).transpose(0, 2, 3, 1, 4)
    kv_heads = lambda t: t.reshape(B, S, Hk, Dh).transpose(0, 2, 1, 3)
    mh_q = lambda t, d: t.reshape(B, S, -1, d).transpose(0, 2, 1, 3)[:, :, None]
    mh_kv = lambda t, d: t.reshape(B, S, -1, d).transpose(0, 2, 1, 3)
    tokens = lambda o: o.transpose(0, 3, 1, 2, 4).reshape(M, -1)

    ng = z_small[:, 64:88].reshape(B, S, Hk, G, 3).transpose(4, 0, 2, 3, 1)[..., None]
    nq = q_heads(zc["nq"])
    rows = z_cmp.reshape(B, S, 2 * Hk, Dh).transpose(0, 2, 1, 3).reshape(B, 2 * Hk, S // NSA_CMP_STRIDE,
                                                                         NSA_CMP_STRIDE * Dh)
    kvc = _nsa_compress(rows, w["cmp_pe"], w["cmp_w"])
    oc, selt = _nsa_cmp(nq, kvc, c["pos_col"], c["pc_row"], c["nsa_slopes_cmp"], ng[0], c["overlap_t"])
    selb = selt.transpose(0, 1, 3, 2)
    q_aug = jnp.concatenate(
        [nq, jnp.broadcast_to(selb[:, :, None], (B, Hk, G, S, selb.shape[-1]))], axis=-1)
    k_aug = jnp.concatenate(
        [kv_heads(zc["nks"]), jnp.broadcast_to(c["sel_onehot"], (B, Hk, S, c["sel_onehot"].shape[-1]))], axis=-1)
    o_sel = _flash(
        AttnCfg(tq=FLASH_TQ // G, tk=FLASH_TK, alibi=True, gate=True, add=True, out_dtype=F32),
        q_aug, k_aug, kv_heads(zc["nvs"]), pos_col=c["pos_col"], pos_rows=c["pos_rows_flash"],
        slopes=c["nsa_slopes_flash"], gate=ng[1], add=oc)
    y_nsa = _band(
        AttnCfg(tq=BAND_TQ, window=NSA_WINDOW, alibi=True, gate=True, add=True),
        nq, kv_heads(zc["nkw"]), kv_heads(zc["nvw"]), pos_col=c["pos_col"], pos_rows=c["pos_rows_band"],
        slopes=c["nsa_slopes_band"], gate=ng[2], add=o_sel)

    y_swa = _band(
        AttnCfg(tq=BAND_TQ, window=SWA_WINDOW, alibi=True, sink=True),
        q_heads(zc["sq"]), kv_heads(zc["sk"]), kv_heads(zc["sv"]), pos_col=c["pos_col"],
        pos_rows=c["pos_rows_band"], slopes=c["swa_slopes_band"],
        sinks=_per_row(w["swa_sinks"].reshape(SWA_KV_HEADS, SWA_GROUP), BAND_TQ))

    qn, qr, kn, mv, kr = _mla_prep(z_mla, z_small, c["cos"], c["sin"], w["mla_q_norm"], w["mla_kv_norm"],
                                   w["wq_all"], w["wkv_all"])
    q_mla = jnp.concatenate([qn.reshape(B, S, MLA_HEADS, MLA_NOPE_DIM),
                             qr.reshape(B, S, MLA_HEADS, MLA_ROPE_DIM)], -1).transpose(0, 2, 1, 3)[:, :, None]
    k_mla = jnp.concatenate([kn.reshape(B, S, MLA_HEADS, MLA_NOPE_DIM),
                             jnp.broadcast_to(kr.reshape(B, S, 1, MLA_ROPE_DIM),
                                              (B, S, MLA_HEADS, MLA_ROPE_DIM))], -1).transpose(0, 2, 1, 3)
    y_mla = _flash(AttnCfg(tq=FLASH_TQ, tk=FLASH_TK), q_mla, k_mla, mh_kv(mv, MLA_V_DIM))

    f_logit = z_small[:, 88:96].reshape(B, S, FOX_HEADS).transpose(0, 2, 1)
    cum = _fox_cumsum(f_logit, w["fox_f_bias"])
    y_fox = _flash(
        AttnCfg(tq=FLASH_TQ, tk=FLASH_TK, decay=True),
        mh_q(zc["fq"], Dh), mh_kv(zc["fk"], Dh), mh_kv(zc["fv"], Dh),
        cum_col=cum[:, :, None, :, None], cum_rows=cum.reshape(B, FOX_HEADS, S // FLASH_TK, 1, FLASH_TK))

    y_cat = jnp.concatenate([tokens(y_nsa), tokens(y_swa), tokens(y_mla), tokens(y_fox)], axis=-1)
    x1 = _merge(x.reshape(M, D), y_cat, w["mix_norm_pre"], w["w_gate"], w["w_branch"], w["w_out"],
                w["mix_norm_post"])

    kv = _mem_kv(c["mem"], w["xa_mem_norm"], w["xa_wkv"])
    kv = kv.reshape(B, -1, 2 * XA_HEADS, Dh).transpose(0, 2, 1, 3)
    x2 = _xattn(x1.reshape(B, S, D), kv, w["xa_norm_pre"], w["xa_wq"], w["xa_wo"], w["xa_norm_post"])

    x3 = _ffn(x2.reshape(M, D), w["ffn_norm_pre"], w["ffn_wg"], w["ffn_wu"], w["ffn_wo"], w["ffn_norm_post"])
    return x3.reshape(B, S, D)


def _constants(mem, positions, S):
    B = positions.shape[0]
    pos = positions.astype(F32)
    half = MLA_ROPE_DIM // 2
    inv = (ROPE_THETA ** (-np.arange(half, dtype=np.float32) / half)).astype(np.float32)
    inv_row = jnp.asarray(np.tile(inv, 2 * MLA_HEADS)[None, :])
    cos, sin = _rope_tables(pos.reshape(B * S, 1), inv_row)
    nc = S // NSA_CMP_STRIDE
    n_valid = (S - NSA_CMP_LEN) // NSA_CMP_STRIDE + 1
    assert n_valid == nc - 1
    c_end = np.minimum(np.arange(nc) * NSA_CMP_STRIDE + NSA_CMP_LEN - 1, S - 1)
    n_sel = S // NSA_SEL_LEN
    c_start = np.arange(nc) * NSA_CMP_STRIDE
    sel_start = np.arange(n_sel) * NSA_SEL_LEN
    overlap = ((c_start[:, None] < sel_start[None, :] + NSA_SEL_LEN)
               & (c_start[:, None] + NSA_CMP_LEN - 1 >= sel_start[None, :])
               & (np.arange(nc)[:, None] < n_valid)).astype(np.float32)
    sel_onehot = (np.arange(S)[:, None] // NSA_SEL_LEN == np.arange(n_sel)[None, :]).astype(np.float32)
    nsa_slopes = jnp.asarray(_alibi_slopes(NSA_HEADS).reshape(NSA_KV_HEADS, NSA_GROUP))
    swa_slopes = jnp.asarray(_alibi_slopes(SWA_HEADS).reshape(SWA_KV_HEADS, SWA_GROUP))
    return dict(
        mem=mem, cos=cos, sin=sin,
        pos_col=pos[:, :, None],
        pos_rows_flash=pos.reshape(B, S // FLASH_TK, 1, FLASH_TK),
        pos_rows_band=pos.reshape(B, S // LANES, 1, LANES),
        pc_row=pos[:, c_end][:, None, :],
        overlap_t=jnp.asarray(overlap.T),
        sel_onehot=jnp.asarray(sel_onehot, dtype=BF16)[None, None],
        nsa_slopes_cmp=_per_row(nsa_slopes, NSA_CMP_TQ),
        nsa_slopes_flash=_per_row(nsa_slopes, FLASH_TQ // NSA_GROUP, LANES),
        nsa_slopes_band=_per_row(nsa_slopes, BAND_TQ),
        swa_slopes_band=_per_row(swa_slopes, BAND_TQ),
    )


def kernel(x, mem, positions, mix_norm_pre, w_in, nsa_cmp_w, nsa_cmp_pe, swa_sinks, mla_q_norm, mla_w_q_up,
           mla_kv_norm, mla_w_kv_up, fox_f_bias, w_branch, w_gate, w_out, mix_norm_post, xa_norm_pre,
           xa_mem_norm, xa_w_q, xa_w_kv, xa_w_o, xa_norm_post, ffn_norm_pre, ffn_w_in, ffn_w_out,
           ffn_norm_post):
    params = dict(
        mix_norm_pre=mix_norm_pre, w_in=w_in, nsa_cmp_w=nsa_cmp_w, nsa_cmp_pe=nsa_cmp_pe, swa_sinks=swa_sinks,
        mla_q_norm=mla_q_norm, mla_w_q_up=mla_w_q_up, mla_kv_norm=mla_kv_norm, mla_w_kv_up=mla_w_kv_up,
        fox_f_bias=fox_f_bias, w_branch=w_branch, w_gate=w_gate, w_out=w_out, mix_norm_post=mix_norm_post,
        xa_norm_pre=xa_norm_pre, xa_mem_norm=xa_mem_norm, xa_w_q=xa_w_q, xa_w_kv=xa_w_kv, xa_w_o=xa_w_o,
        xa_norm_post=xa_norm_post, ffn_norm_pre=ffn_norm_pre, ffn_w_in=ffn_w_in, ffn_w_out=ffn_w_out,
        ffn_norm_post=ffn_norm_post)
    S = x.shape[1]
    weights = _prep_weights(params)
    consts = _constants(mem, positions, S)

    def body(xc, w):
        return _layer(xc, w, consts), None

    out, _ = lax.scan(body, x, weights)
    return out
```

```python
import functools
from typing import NamedTuple

import numpy as np
import jax
import jax.numpy as jnp
from jax import lax
from jax.experimental import pallas as pl
from jax.experimental.pallas import tpu as pltpu

F32 = jnp.float32
BF16 = jnp.bfloat16

D_MODEL = 1024
HEAD_DIM = 64
NORM_EPS = 1e-6
NEG_INF = -1e30
FORCE_SCORE = 1e9
LANES = 128

NSA_HEADS = 8
NSA_KV_HEADS = 2
NSA_GROUP = NSA_HEADS // NSA_KV_HEADS
NSA_CMP_LEN = 32
NSA_CMP_STRIDE = 16
NSA_SEL_LEN = 64
NSA_TOP_N = 8
NSA_WINDOW = 512

SWA_HEADS = 8
SWA_KV_HEADS = 2
SWA_GROUP = SWA_HEADS // SWA_KV_HEADS
SWA_WINDOW = 128

MLA_HEADS = 8
MLA_Q_RANK = 256
MLA_KV_RANK = 128
MLA_NOPE_DIM = 64
MLA_ROPE_DIM = 32
MLA_V_DIM = 64
MLA_QK_DIM = MLA_NOPE_DIM + MLA_ROPE_DIM
ROPE_THETA = 10000.0

FOX_HEADS = 8
XA_HEADS = 4
N_BRANCH = 4
BRANCH_WIDTH = 512
D_FF = ((8 * D_MODEL + 3 * 256 - 1) // (3 * 256)) * 256

IN_SPLITS = (512, 128, 128, 128, 128, 128, 128, 24,
             512, 128, 128,
             MLA_Q_RANK, MLA_KV_RANK, MLA_ROPE_DIM,
             512, 512, 512, FOX_HEADS)
IN_NAMES = ("nq", "nkc", "nvc", "nks", "nvs", "nkw", "nvw", "ng",
            "sq", "sk", "sv", "mcq", "mckv", "mkr", "fq", "fk", "fv", "ff")

VMEM_LIMIT_BYTES = 56 * 1024 * 1024

MAIN_ORDER = ("nq", "nks", "nvs", "nkw", "nvw", "sq", "sk", "sv", "fq", "fk", "fv")
MAIN_COLS = 3328
SMALL_COLS = 128


def _cparams(sem):
    return pltpu.CompilerParams(dimension_semantics=sem, vmem_limit_bytes=VMEM_LIMIT_BYTES)


def _rms(x, g):
    return x * lax.rsqrt(jnp.mean(x * x, axis=-1, keepdims=True) + NORM_EPS) * g


def _dot(a, b):
    return jnp.dot(a, b, preferred_element_type=F32)


def _dot_nt(a, b, precision=None):
    return lax.dot_general(a, b, (((1,), (1,)), ((), ())), preferred_element_type=F32,
                           precision=precision)


def _rep(x, n):
    return x if n == 1 else jnp.concatenate([x] * n, axis=1)


F32_COLS = MLA_Q_RANK + MLA_KV_RANK + 4 * HEAD_DIM + SMALL_COLS
MAIN_HEADS = tuple(8 if n in ("nq", "sq", "fq", "fk", "fv") else 2 for n in MAIN_ORDER)
PROJ_CHUNK = 512


def _inproj_kernel(x_ref, g_ref, w_ref, zmla_ref, zsmall_ref, cmp_ref, *head_refs):
    tm = x_ref.shape[1]
    h = _rms(x_ref[0], g_ref[...]).astype(BF16)
    n_mla = MLA_Q_RANK + MLA_KV_RANK
    zmla_ref[0] = _dot(h, w_ref[:, :n_mla])
    r = _dot(h, w_ref[:, n_mla:F32_COLS])
    for f in range(4):
        cmp_ref[0, f] = r[:, f * HEAD_DIM:(f + 1) * HEAD_DIM]
    zsmall_ref[0] = r[:, 4 * HEAD_DIM:]

    n_sel = head_refs[MAIN_ORDER.index("nks")].shape[3] - HEAD_DIM
    t = pl.program_id(1) * tm + lax.broadcasted_iota(jnp.int32, (tm, n_sel), 0)
    blk = lax.shift_right_logical(t, NSA_SEL_LEN.bit_length() - 1)
    onehot = jnp.where(blk == lax.broadcasted_iota(jnp.int32, (tm, n_sel), 1), 1.0, 0.0)

    dest = [(ref, hd, name == "nks") for ref, nh, name in zip(head_refs, MAIN_HEADS, MAIN_ORDER)
            for hd in range(nh)]
    for c in range(0, MAIN_COLS, PROJ_CHUNK):
        w = min(PROJ_CHUNK, MAIN_COLS - c)
        res = _dot(h, w_ref[:, F32_COLS + c:F32_COLS + c + w])
        for k in range(w // HEAD_DIM):
            ref, hd, with_onehot = dest[c // HEAD_DIM + k]
            piece = res[:, k * HEAD_DIM:(k + 1) * HEAD_DIM]
            if with_onehot:
                piece = jnp.concatenate([piece, onehot], axis=-1)
            ref[0, hd] = piece.astype(ref.dtype)


def _inproj(x, g, w, tm=512):
    B, S, D = x.shape
    N = w.shape[1]
    n_sel = S // NSA_SEL_LEN
    assert N == F32_COLS + MAIN_COLS and S % tm == 0
    tok = lambda n: pl.BlockSpec((1, tm, n), lambda b, i: (b, i, 0))
    heads = lambda nh, d: pl.BlockSpec((1, nh, tm, d), lambda b, i: (b, 0, i, 0))
    head_dims = [HEAD_DIM + n_sel if n == "nks" else HEAD_DIM for n in MAIN_ORDER]
    return pl.pallas_call(
        _inproj_kernel,
        grid=(B, S // tm),
        in_specs=[pl.BlockSpec((1, tm, D), lambda b, i: (b, i, 0)),
                  pl.BlockSpec((1, D), lambda b, i: (0, 0)),
                  pl.BlockSpec((D, N), lambda b, i: (0, 0))],
        out_specs=[tok(MLA_Q_RANK + MLA_KV_RANK), tok(SMALL_COLS), heads(4, HEAD_DIM)]
                  + [heads(nh, d) for nh, d in zip(MAIN_HEADS, head_dims)],
        out_shape=[jax.ShapeDtypeStruct((B, S, MLA_Q_RANK + MLA_KV_RANK), F32),
                   jax.ShapeDtypeStruct((B, S, SMALL_COLS), F32),
                   jax.ShapeDtypeStruct((B, 4, S, HEAD_DIM), F32)]
                  + [jax.ShapeDtypeStruct((B, nh, S, d), BF16) for nh, d in zip(MAIN_HEADS, head_dims)],
        compiler_params=_cparams(("parallel", "parallel")),
        name="inproj",
    )(x, g, w)


def _rope_table_kernel(pos_ref, inv_ref, cos_ref, sin_ref):
    ang = pos_ref[...] * inv_ref[...]
    cos_ref[...] = jnp.cos(ang)
    sin_ref[...] = jnp.sin(ang)


def _rope_tables(pos_col, inv_row, tm=512):
    M = pos_col.shape[0]
    N = inv_row.shape[1]
    return pl.pallas_call(
        _rope_table_kernel,
        grid=(M // tm,),
        in_specs=[pl.BlockSpec((tm, 1), lambda i: (i, 0)),
                  pl.BlockSpec((1, N), lambda i: (0, 0))],
        out_specs=[pl.BlockSpec((tm, N), lambda i: (i, 0))] * 2,
        out_shape=[jax.ShapeDtypeStruct((M, N), F32)] * 2,
        compiler_params=_cparams(("parallel",)),
        name="rope_tables",
    )(pos_col, inv_row)


def _mla_prep_kernel(z_ref, zs_ref, cos_ref, sin_ref, qg_ref, kvg_ref, wq_ref, wkv_ref,
                     q_ref, k_ref, v_ref):
    nq = MLA_HEADS * MLA_NOPE_DIM
    nr = MLA_HEADS * MLA_ROPE_DIM
    scale = MLA_QK_DIM ** -0.5
    z = z_ref[0]
    cq = _rms(z[:, :MLA_Q_RANK], qg_ref[...]).astype(BF16)
    ckv = _rms(z[:, MLA_Q_RANK:], kvg_ref[...]).astype(BF16)
    cos = cos_ref[0]
    sin = sin_ref[0]
    q_nope = _dot(cq, wq_ref[:, :nq]) * scale
    q_rot = (_dot(cq, wq_ref[:, nq:nq + nr]) * cos + _dot(cq, wq_ref[:, nq + nr:]) * sin) * scale
    k_nope = _dot(ckv, wkv_ref[:, :nq])
    v = _dot(ckv, wkv_ref[:, nq:])
    zs = zs_ref[0]
    r, d = MLA_ROPE_DIM, MLA_NOPE_DIM
    k_rot = zs[:, :r] * cos[:, :r] + zs[:, r:2 * r] * sin[:, :r]
    for h in range(MLA_HEADS):
        q_ref[0, h] = jnp.concatenate([q_nope[:, h * d:(h + 1) * d], q_rot[:, h * r:(h + 1) * r]],
                                      axis=-1).astype(BF16)
        k_ref[0, h] = jnp.concatenate([k_nope[:, h * d:(h + 1) * d], k_rot], axis=-1).astype(BF16)
        v_ref[0, h] = v[:, h * MLA_V_DIM:(h + 1) * MLA_V_DIM].astype(BF16)


def _mla_prep(z_mla, z_small, cos, sin, qg, kvg, wq, wkv, tm=512):
    B, S, _ = z_mla.shape
    nr = MLA_HEADS * MLA_ROPE_DIM
    tok = lambda n: pl.BlockSpec((1, tm, n), lambda b, i: (b, i, 0))
    heads = lambda d: pl.BlockSpec((1, MLA_HEADS, tm, d), lambda b, i: (b, 0, i, 0))
    full = lambda a: pl.BlockSpec(a.shape, lambda b, i: (0,) * a.ndim)
    return pl.pallas_call(
        _mla_prep_kernel,
        grid=(B, S // tm),
        in_specs=[tok(z_mla.shape[2]), tok(SMALL_COLS), tok(nr), tok(nr),
                  full(qg), full(kvg), full(wq), full(wkv)],
        out_specs=[heads(MLA_QK_DIM), heads(MLA_QK_DIM), heads(MLA_V_DIM)],
        out_shape=[jax.ShapeDtypeStruct((B, MLA_HEADS, S, d), BF16)
                   for d in (MLA_QK_DIM, MLA_QK_DIM, MLA_V_DIM)],
        compiler_params=_cparams(("parallel", "parallel")),
        name="mla_prep",
    )(z_mla, z_small, cos, sin, qg, kvg, wq, wkv)


CUM_CHUNK = 512


def _fox_cumsum_kernel(f_ref, b_ref, u_ref, o_ref):
    x = f_ref[0] + b_ref[...]
    lf = jnp.minimum(x, 0.0) - jnp.log1p(jnp.exp(-jnp.abs(x)))
    S = lf.shape[1]
    carry = jnp.zeros((lf.shape[0], 1), F32)
    for c in range(0, S, CUM_CHUNK):
        loc = jnp.dot(lf[:, c:c + CUM_CHUNK], u_ref[...], preferred_element_type=F32,
                      precision=lax.Precision.HIGHEST) + carry
        o_ref[0, :, c:c + CUM_CHUNK] = loc
        carry = loc[:, CUM_CHUNK - 1:CUM_CHUNK]


def _fox_cumsum(f_bhs, bias_col):
    B, H, S = f_bhs.shape
    assert S % CUM_CHUNK == 0
    upper = jnp.asarray(np.triu(np.ones((CUM_CHUNK, CUM_CHUNK), np.float32)))
    return pl.pallas_call(
        _fox_cumsum_kernel,
        grid=(B,),
        in_specs=[pl.BlockSpec((1, H, S), lambda b: (b, 0, 0)),
                  pl.BlockSpec((H, 1), lambda b: (0, 0)),
                  pl.BlockSpec((CUM_CHUNK, CUM_CHUNK), lambda b: (0, 0))],
        out_specs=pl.BlockSpec((1, H, S), lambda b: (b, 0, 0)),
        out_shape=jax.ShapeDtypeStruct((B, H, S), F32),
        compiler_params=_cparams(("parallel",)),
        name="fox_cumsum",
    )(f_bhs, bias_col, upper)


def _nsa_compress_kernel(r_ref, pe_ref, w_ref, o_ref):
    r = r_ref[0, 0]
    first = _dot((r + pe_ref[0, 0:1, :]).astype(BF16), w_ref[0, 0])
    second = _dot((r + pe_ref[0, 1:2, :]).astype(BF16), w_ref[0, 1])
    nc = r.shape[0]
    nxt = pltpu.roll(second, nc - 1, 0)
    row = lax.broadcasted_iota(jnp.int32, (nc, 1), 0)
    o_ref[0, 0] = jnp.where(row < nc - 1, first + nxt, 0.0).astype(o_ref.dtype)


def _nsa_compress(rows, pe, w):
    B, F, NC, K = rows.shape
    return pl.pallas_call(
        _nsa_compress_kernel,
        grid=(B, F),
        in_specs=[pl.BlockSpec((1, 1, NC, K), lambda b, f: (b, f, 0, 0)),
                  pl.BlockSpec((1, 2, K), lambda b, f: (f // NSA_KV_HEADS, 0, 0)),
                  pl.BlockSpec((1, 2, K, HEAD_DIM), lambda b, f: (f // NSA_KV_HEADS, 0, 0, 0))],
        out_specs=pl.BlockSpec((1, 1, NC, HEAD_DIM), lambda b, f: (b, f, 0, 0)),
        out_shape=jax.ShapeDtypeStruct((B, F, NC, HEAD_DIM), BF16),
        compiler_params=_cparams(("parallel", "parallel")),
        name="nsa_compress",
    )(rows, pe, w)


NSA_CMP_TQ = 256


def _nsa_cmp_kernel(q_ref, kc_ref, vc_ref, pq_ref, pc_ref, sl_ref, gate_ref, ovt_ref,
                    oc_ref, selt_ref):
    G, tq, Dh = q_ref.shape[2:]
    R = G * tq
    i = pl.program_id(2)
    q = q_ref[0, 0].reshape(R, Dh)
    kc = kc_ref[0, 0]
    nc = kc.shape[0]
    n_sel = ovt_ref.shape[0]
    s = _dot_nt(q, kc)
    pq = jnp.concatenate([pq_ref[0]] * G, axis=0)
    s = s - sl_ref[0] * jnp.abs(pq - pc_ref[0])
    t_row = (i * tq + lax.broadcasted_iota(jnp.int32, (G, tq, 1), 1)).reshape(R, 1)
    c_col = lax.broadcasted_iota(jnp.int32, (1, nc), 1)
    c_end = jnp.where(c_col < nc - 1, c_col * NSA_CMP_STRIDE + (NSA_CMP_LEN - 1), jnp.int32(2 ** 30))
    s = jnp.where(c_end <= t_row, s, NEG_INF)
    m = jnp.max(s, axis=-1, keepdims=True)
    e = jnp.exp(s - m)
    l = jnp.sum(e, axis=-1, keepdims=True)
    p = e * jnp.where(m > 0.5 * NEG_INF, 1.0 / l, 0.0)
    o = _dot(p.astype(BF16), vc_ref[0, 0])
    gate = jax.nn.sigmoid(gate_ref[0, 0].reshape(R, 1))
    oc_ref[0, 0] = (gate * o).reshape(G, tq, Dh)
    p_sum = jnp.sum(p.reshape(G, tq, nc), axis=0)
    imp = _dot_nt(ovt_ref[...], p_sum, precision=lax.Precision.HIGHEST)
    t_q = i * tq + lax.broadcasted_iota(jnp.int32, (1, tq), 1)
    jt = lax.shift_right_logical(t_q, NSA_SEL_LEN.bit_length() - 1)
    j = lax.broadcasted_iota(jnp.int32, (n_sel, 1), 0)
    forced = (j == 0) | (j == jt) | (j == jt - 1)
    causal = j <= jt
    val = jnp.where(forced, FORCE_SCORE, imp)
    val = jnp.where(causal, val, NEG_INF)
    sel = jnp.zeros((n_sel, tq), jnp.int32)
    for _ in range(min(NSA_TOP_N, n_sel)):
        best = jnp.max(val, axis=0, keepdims=True)
        idx = jnp.min(jnp.where(val == best, j, n_sel), axis=0, keepdims=True)
        pick = j == idx
        sel = jnp.where(pick, 1, sel)
        val = jnp.where(pick, -jnp.inf, val)
    keep = (sel > 0) & causal
    selt_ref[0, 0] = jnp.where(keep, 0.0, NEG_INF).astype(selt_ref.dtype)


def _nsa_cmp(q, kvc, pos_col, pc_row, slopes, gate, overlap_t):
    B, Hk, G, S, Dh = q.shape
    NC = kvc.shape[2]
    n_sel = overlap_t.shape[0]
    assert NSA_SEL_LEN & (NSA_SEL_LEN - 1) == 0
    tq = NSA_CMP_TQ
    R = G * tq
    return pl.pallas_call(
        _nsa_cmp_kernel,
        grid=(B, Hk, S // tq),
        in_specs=[pl.BlockSpec((1, 1, G, tq, Dh), lambda b, h, i: (b, h, 0, i, 0)),
                  pl.BlockSpec((1, 1, NC, Dh), lambda b, h, i: (b, h, 0, 0)),
                  pl.BlockSpec((1, 1, NC, Dh), lambda b, h, i: (b, h + Hk, 0, 0)),
                  pl.BlockSpec((1, tq, 1), lambda b, h, i: (b, i, 0)),
                  pl.BlockSpec((1, 1, NC), lambda b, h, i: (b, 0, 0)),
                  pl.BlockSpec((1, R, 1), lambda b, h, i: (h, 0, 0)),
                  pl.BlockSpec((1, 1, G, tq, 1), lambda b, h, i: (b, h, 0, i, 0)),
                  pl.BlockSpec((n_sel, NC), lambda b, h, i: (0, 0))],
        out_specs=[pl.BlockSpec((1, 1, G, tq, Dh), lambda b, h, i: (b, h, 0, i, 0)),
                   pl.BlockSpec((1, 1, n_sel, tq), lambda b, h, i: (b, h, 0, i))],
        out_shape=[jax.ShapeDtypeStruct((B, Hk, G, S, Dh), F32),
                   jax.ShapeDtypeStruct((B, Hk, n_sel, S), BF16)],
        compiler_params=_cparams(("parallel", "parallel", "parallel")),
        name="nsa_cmp",
    )(q, kvc, kvc, pos_col, pc_row, slopes, gate, overlap_t)


class AttnCfg(NamedTuple):
    tq: int
    tk: int = 0
    window: int = 0
    alibi: bool = False
    decay: bool = False
    sink: bool = False
    gate: bool = False
    add: bool = False
    selaug: bool = False
    out_dtype: object = BF16


def _epilogue(cfg, out, gate_ref, add_ref, o_ref, G, tq):
    R, Dv = out.shape
    if cfg.gate:
        out = out * jax.nn.sigmoid(gate_ref[0, 0].reshape(R, 1))
    if cfg.add:
        out = out + add_ref[0, 0].reshape(R, Dv)
    o_ref[0, 0] = out.reshape(G, tq, Dv).astype(o_ref.dtype)


def _flash_kernel(*refs, cfg):
    it = iter(refs)
    q_ref, k_ref, v_ref = next(it), next(it), next(it)
    pq_ref = pk_ref = sl_ref = cq_ref = ck_ref = gate_ref = add_ref = None
    selb_ref = next(it) if cfg.selaug else None
    if cfg.alibi:
        pq_ref, pk_ref, sl_ref = next(it), next(it), next(it)
    if cfg.decay:
        cq_ref, ck_ref = next(it), next(it)
    if cfg.gate:
        gate_ref = next(it)
    if cfg.add:
        add_ref = next(it)
    o_ref = next(it)
    m_s, l_s, acc_s = next(it), next(it), next(it)
    pq_s = next(it) if cfg.alibi else None
    cq_s = next(it) if cfg.decay else None

    G, tq, Dq = q_ref.shape[2:]
    Dv = v_ref.shape[3]
    tk = cfg.tk
    nrep = tk // LANES
    R = G * tq
    i = pl.program_id(2)
    q = q_ref[0, 0].reshape(R, Dq)
    if cfg.selaug:
        bias = jnp.concatenate([selb_ref[0, 0].astype(F32)] * G, axis=0)
        q = jnp.concatenate([q.astype(F32), bias], axis=-1).astype(BF16)

    m_s[...] = jnp.full((R, LANES), NEG_INF, F32)
    l_s[...] = jnp.zeros((R, LANES), F32)
    acc_s[...] = jnp.zeros((R, Dv), F32)
    if cfg.alibi:
        pq_s[...] = jnp.broadcast_to(jnp.concatenate([pq_ref[0]] * G, axis=0), (R, LANES))
    if cfg.decay:
        cq_s[...] = jnp.broadcast_to(cq_ref[0, 0].reshape(R, 1), (R, LANES))

    def step(j, masked):
        k0 = pl.multiple_of(j * tk, tk)
        s = _dot_nt(q, k_ref[0, 0, pl.ds(k0, tk), :])
        if cfg.alibi:
            s = s - _rep(sl_ref[0], nrep) * jnp.abs(_rep(pq_s[...], nrep) - pk_ref[0, j])
        if cfg.decay:
            s = s + (_rep(cq_s[...], nrep) - ck_ref[0, 0, j])
        if masked:
            t_row = (i * tq + lax.broadcasted_iota(jnp.int32, (G, tq, tk), 1)).reshape(R, tk)
            col = j * tk + lax.broadcasted_iota(jnp.int32, (R, tk), 1)
            s = jnp.where(col <= t_row, s, NEG_INF)
        m_prev = m_s[...]
        m_new = jnp.maximum(m_prev, jnp.max(s, axis=-1, keepdims=True))
        alpha = jnp.exp(m_prev - m_new)
        p = jnp.exp(s - _rep(m_new, nrep))
        l_s[...] = alpha * l_s[...] + jnp.sum(p, axis=-1, keepdims=True)
        acc_s[...] = alpha[:, :Dv] * acc_s[...] + _dot(p.astype(BF16), v_ref[0, 0, pl.ds(k0, tk), :])
        m_s[...] = m_new

    j_hi = ((i + 1) * tq - 1) // tk

    def body(j, c):
        step(j, False)
        return c
    lax.fori_loop(0, j_hi, body, 0)
    step(j_hi, True)

    out = acc_s[...] / l_s[...][:, :Dv]
    _epilogue(cfg, out, gate_ref, add_ref, o_ref, G, tq)


def _flash(cfg, q, k, v, *, selb=None, pos_col=None, pos_rows=None, slopes=None, cum_col=None,
           cum_rows=None, gate=None, add=None):
    B, Hk, G, S, Dq = q.shape
    Dk, Dv = k.shape[3], v.shape[3]
    tq, tk = cfg.tq, cfg.tk
    assert S % tq == 0 and S % tk == 0 and tk % tq == 0 and tk % LANES == 0 and Dv <= LANES
    assert Dk == Dq + (selb.shape[3] if cfg.selaug else 0)
    R = G * tq
    nk = S // tk
    qmap = lambda b, h, i: (b, h, 0, i, 0)
    kvmap = lambda b, h, i: (b, h, 0, 0)
    args = [q, k, v]
    specs = [pl.BlockSpec((1, 1, G, tq, Dq), qmap),
             pl.BlockSpec((1, 1, S, Dk), kvmap),
             pl.BlockSpec((1, 1, S, Dv), kvmap)]
    scratch = [pltpu.VMEM((R, LANES), F32), pltpu.VMEM((R, LANES), F32), pltpu.VMEM((R, Dv), F32)]
    if cfg.selaug:
        args += [selb]
        specs += [pl.BlockSpec((1, 1, tq, Dk - Dq), lambda b, h, i: (b, h, i, 0))]
    if cfg.alibi:
        args += [pos_col, pos_rows, slopes]
        specs += [pl.BlockSpec((1, tq, 1), lambda b, h, i: (b, i, 0)),
                  pl.BlockSpec((1, nk, 1, tk), lambda b, h, i: (b, 0, 0, 0)),
                  pl.BlockSpec((1, R, LANES), lambda b, h, i: (h, 0, 0))]
        scratch.append(pltpu.VMEM((R, LANES), F32))
    if cfg.decay:
        args += [cum_col, cum_rows]
        specs += [pl.BlockSpec((1, 1, G, tq, 1), qmap),
                  pl.BlockSpec((1, 1, nk, 1, tk), lambda b, h, i: (b, h, 0, 0, 0))]
        scratch.append(pltpu.VMEM((R, LANES), F32))
    if cfg.gate:
        args += [gate]
        specs += [pl.BlockSpec((1, 1, G, tq, 1), qmap)]
    if cfg.add:
        args += [add]
        specs += [pl.BlockSpec((1, 1, G, tq, Dv), qmap)]
    return pl.pallas_call(
        functools.partial(_flash_kernel, cfg=cfg),
        grid=(B, Hk, S // tq),
        in_specs=specs,
        out_specs=pl.BlockSpec((1, 1, G, tq, Dv), qmap),
        out_shape=jax.ShapeDtypeStruct((B, Hk, G, S, Dv), cfg.out_dtype),
        scratch_shapes=scratch,
        compiler_params=_cparams(("parallel", "parallel", "parallel")),
        name="flash",
    )(*args)


def _band_kernel(*refs, cfg):
    it = iter(refs)
    q_ref, k_ref, v_ref, pq_ref, pk_ref, sl_ref = (next(it) for _ in range(6))
    sink_ref = next(it) if cfg.sink else None
    gate_ref = next(it) if cfg.gate else None
    add_ref = next(it) if cfg.add else None
    o_ref = next(it)

    G, tq, Dq = q_ref.shape[2:]
    R = G * tq
    W = cfg.window
    span = W + tq
    i = pl.program_id(2)
    start = pl.multiple_of(jnp.maximum(i * tq - W, 0), LANES)
    blk0 = lax.shift_right_logical(start, LANES.bit_length() - 1)
    q = q_ref[0, 0].reshape(R, Dq)
    s = _dot_nt(q, k_ref[0, 0, pl.ds(start, span), :])
    pq = jnp.concatenate([pq_ref[0]] * G, axis=0)
    slope = sl_ref[0]
    d = (lax.broadcasted_iota(jnp.int32, (G, tq, LANES), 1)
         - lax.broadcasted_iota(jnp.int32, (G, tq, LANES), 2)).reshape(R, LANES)
    off = i * tq - start
    chunks = []
    for c in range(span // LANES):
        sc = s[:, c * LANES:(c + 1) * LANES]
        sc = sc - slope * jnp.abs(pq - pk_ref[0, blk0 + c])
        lo = c * LANES - off
        sc = jnp.where(d >= lo, jnp.where(d < lo + W, sc, NEG_INF), NEG_INF)
        chunks.append(sc)
    s = jnp.concatenate(chunks, axis=1)
    m = jnp.max(s, axis=-1, keepdims=True)
    e = jnp.exp(s - m)
    l = jnp.sum(e, axis=-1, keepdims=True)
    acc = _dot(e.astype(BF16), v_ref[0, 0, pl.ds(start, span), :])
    if cfg.sink:
        sink = sink_ref[0]
        m_f = jnp.maximum(m, sink)
        w = jnp.exp(m - m_f)
        out = acc * (w / (l * w + jnp.exp(sink - m_f)))
    else:
        out = acc / l
    _epilogue(cfg, out, gate_ref, add_ref, o_ref, G, tq)


def _band(cfg, q, k, v, *, pos_col, pos_rows, slopes, sinks=None, gate=None, add=None):
    B, Hk, G, S, Dq = q.shape
    Dv = v.shape[3]
    tq = cfg.tq
    assert tq == LANES and cfg.window % LANES == 0 and S % tq == 0 and S >= cfg.window + tq
    R = G * tq
    qmap = lambda b, h, i: (b, h, 0, i, 0)
    kvmap = lambda b, h, i: (b, h, 0, 0)
    args = [q, k, v, pos_col, pos_rows, slopes]
    specs = [pl.BlockSpec((1, 1, G, tq, Dq), qmap),
             pl.BlockSpec((1, 1, S, Dq), kvmap),
             pl.BlockSpec((1, 1, S, Dv), kvmap),
             pl.BlockSpec((1, tq, 1), lambda b, h, i: (b, i, 0)),
             pl.BlockSpec((1, S // LANES, 1, LANES), lambda b, h, i: (b, 0, 0, 0)),
             pl.BlockSpec((1, R, 1), lambda b, h, i: (h, 0, 0))]
    if cfg.sink:
        args += [sinks]
        specs += [pl.BlockSpec((1, R, 1), lambda b, h, i: (h, 0, 0))]
    if cfg.gate:
        args += [gate]
        specs += [pl.BlockSpec((1, 1, G, tq, 1), qmap)]
    if cfg.add:
        args += [add]
        specs += [pl.BlockSpec((1, 1, G, tq, Dv), qmap)]
    return pl.pallas_call(
        functools.partial(_band_kernel, cfg=cfg),
        grid=(B, Hk, S // tq),
        in_specs=specs,
        out_specs=pl.BlockSpec((1, 1, G, tq, Dv), qmap),
        out_shape=jax.ShapeDtypeStruct((B, Hk, G, S, Dv), cfg.out_dtype),
        compiler_params=_cparams(("parallel", "parallel", "parallel")),
        name="band",
    )(*args)


def _merge_kernel(x_ref, y0_ref, y1_ref, y2_ref, y3_ref, gpre_ref, wg_ref, wb_ref, wo_ref, gpost_ref,
                  o_ref):
    x = x_ref[0]
    h = _rms(x, gpre_ref[...]).astype(BF16)
    merged = None
    for b, y_ref in enumerate((y0_ref, y1_ref, y2_ref, y3_ref)):
        gate = jax.nn.sigmoid(_dot(h, wg_ref[b]))
        y = jnp.concatenate([y_ref[0, hd] for hd in range(y_ref.shape[1])], axis=-1)
        br = _dot(y, wb_ref[b])
        merged = gate * br if merged is None else merged + gate * br
    y = _dot(merged.astype(BF16), wo_ref[...])
    o_ref[0] = x + _rms(y, gpost_ref[...])


def _merge(x, ys, gpre, wg, wb, wo, gpost, tm=256):
    B, S, D = x.shape
    full = lambda a: pl.BlockSpec(a.shape, lambda b, i: (0,) * a.ndim)
    yspec = lambda y: pl.BlockSpec((1, y.shape[1], tm, y.shape[3]), lambda b, i: (b, 0, i, 0))
    return pl.pallas_call(
        _merge_kernel,
        grid=(B, S // tm),
        in_specs=[pl.BlockSpec((1, tm, D), lambda b, i: (b, i, 0))] + [yspec(y) for y in ys]
                 + [full(gpre), full(wg), full(wb), full(wo), full(gpost)],
        out_specs=pl.BlockSpec((1, tm, D), lambda b, i: (b, i, 0)),
        out_shape=jax.ShapeDtypeStruct((B, S, D), F32),
        compiler_params=_cparams(("parallel", "parallel")),
        name="merge",
    )(x, *ys, gpre, wg, wb, wo, gpost)


def _mem_kv_kernel(mem_ref, g_ref, w_ref, o_ref):
    o_ref[0] = _dot(_rms(mem_ref[0], g_ref[...]).astype(BF16), w_ref[...]).astype(BF16)


def _mem_kv(mem, g, w):
    B, Mm, D = mem.shape
    N = w.shape[1]
    return pl.pallas_call(
        _mem_kv_kernel,
        grid=(B,),
        in_specs=[pl.BlockSpec((1, Mm, D), lambda b: (b, 0, 0)),
                  pl.BlockSpec((1, D), lambda b: (0, 0)),
                  pl.BlockSpec((D, N), lambda b: (0, 0))],
        out_specs=pl.BlockSpec((1, Mm, N), lambda b: (b, 0, 0)),
        out_shape=jax.ShapeDtypeStruct((B, Mm, N), BF16),
        compiler_params=_cparams(("parallel",)),
        name="mem_kv",
    )(mem, g, w)


def _xattn_kernel(x_ref, kv_ref, gpre_ref, wq_ref, wo_ref, gpost_ref, o_ref):
    x = x_ref[0]
    h = _rms(x, gpre_ref[...]).astype(BF16)
    y = None
    for hd in range(XA_HEADS):
        q = _dot(h, wq_ref[hd]).astype(BF16)
        s = _dot_nt(q, kv_ref[0, hd])
        m = jnp.max(s, axis=-1, keepdims=True)
        e = jnp.exp(s - m)
        p = e / jnp.sum(e, axis=-1, keepdims=True)
        o = _dot(p.astype(BF16), kv_ref[0, XA_HEADS + hd]).astype(BF16)
        part = _dot(o, wo_ref[hd])
        y = part if y is None else y + part
    o_ref[0] = x + _rms(y, gpost_ref[...])


def _xattn(x, kv, gpre, wq, wo, gpost, tm=512):
    B, S, D = x.shape
    full = lambda a: pl.BlockSpec(a.shape, lambda b, i: (0,) * a.ndim)
    return pl.pallas_call(
        _xattn_kernel,
        grid=(B, S // tm),
        in_specs=[pl.BlockSpec((1, tm, D), lambda b, i: (b, i, 0)),
                  pl.BlockSpec((1,) + kv.shape[1:], lambda b, i: (b, 0, 0, 0)),
                  full(gpre), full(wq), full(wo), full(gpost)],
        out_specs=pl.BlockSpec((1, tm, D), lambda b, i: (b, i, 0)),
        out_shape=jax.ShapeDtypeStruct((B, S, D), F32),
        compiler_params=_cparams(("parallel", "parallel")),
        name="xattn",
    )(x, kv, gpre, wq, wo, gpost)


FFN_CHUNK = 256


def _ffn_kernel(x_ref, gpre_ref, wg_ref, wu_ref, wo_ref, gpost_ref, o_ref):
    x = x_ref[...]
    h = _rms(x, gpre_ref[...]).astype(BF16)
    y = None
    for c in range(0, D_FF, FFN_CHUNK):
        g = _dot(h, wg_ref[:, c:c + FFN_CHUNK])
        u = _dot(h, wu_ref[:, c:c + FFN_CHUNK])
        a = (g * jax.nn.sigmoid(g) * u).astype(BF16)
        part = _dot(a, wo_ref[c:c + FFN_CHUNK, :])
        y = part if y is None else y + part
    o_ref[...] = x + _rms(y, gpost_ref[...])


def _ffn(x2d, gpre, wg, wu, wo, gpost, tm=256):
    M, D = x2d.shape
    full = lambda a: pl.BlockSpec(a.shape, lambda i: (0,) * a.ndim)
    return pl.pallas_call(
        _ffn_kernel,
        grid=(M // tm,),
        in_specs=[pl.BlockSpec((tm, D), lambda i: (i, 0)),
                  full(gpre), full(wg), full(wu), full(wo), full(gpost)],
        out_specs=pl.BlockSpec((tm, D), lambda i: (i, 0)),
        out_shape=jax.ShapeDtypeStruct((M, D), F32),
        compiler_params=_cparams(("parallel",)),
        name="ffn",
    )(x2d, gpre, wg, wu, wo, gpost)


def _alibi_slopes(n):
    return (2.0 ** (-8.0 * np.arange(1, n + 1) / n)).astype(np.float32)


def _rotate_half_cols(w):
    half = w.shape[-1] // 2
    return jnp.concatenate([-w[..., half:], w[..., :half]], axis=-1)


def _prep_weights(p):
    L = p["w_in"].shape[0]
    offs = np.concatenate([[0], np.cumsum(IN_SPLITS)])
    col = {n: p["w_in"][:, :, offs[k]:offs[k + 1]] for k, n in enumerate(IN_NAMES)}
    qk_scale = HEAD_DIM ** -0.5
    for n in ("nq", "sq", "fq"):
        col[n] = col[n] * qk_scale
    pad = jnp.zeros((L, D_MODEL, SMALL_COLS - 2 * MLA_ROPE_DIM - 24 - FOX_HEADS), F32)
    small = jnp.concatenate([col["mkr"], _rotate_half_cols(col["mkr"]), col["ng"], col["ff"], pad], -1)
    w_all = jnp.concatenate([col["mcq"], col["mckv"], col["nkc"], col["nvc"], small]
                            + [col[n] for n in MAIN_ORDER], axis=-1).astype(BF16)

    wq = p["mla_w_q_up"].reshape(L, MLA_Q_RANK, MLA_HEADS, MLA_QK_DIM)
    wq_nope = wq[..., :MLA_NOPE_DIM].reshape(L, MLA_Q_RANK, -1)
    wq_rot = wq[..., MLA_NOPE_DIM:]
    wq_all = jnp.concatenate([wq_nope, wq_rot.reshape(L, MLA_Q_RANK, -1),
                              _rotate_half_cols(wq_rot).reshape(L, MLA_Q_RANK, -1)], -1).astype(BF16)
    wkv = p["mla_w_kv_up"].reshape(L, MLA_KV_RANK, MLA_HEADS, MLA_NOPE_DIM + MLA_V_DIM)
    wkv_all = jnp.concatenate([wkv[..., :MLA_NOPE_DIM].reshape(L, MLA_KV_RANK, -1),
                               wkv[..., MLA_NOPE_DIM:].reshape(L, MLA_KV_RANK, -1)], -1).astype(BF16)

    half = NSA_CMP_LEN // 2
    cmp_w = p["nsa_cmp_w"].reshape(L, 2, 2, half * HEAD_DIM, HEAD_DIM).astype(BF16)
    cmp_pe = p["nsa_cmp_pe"].reshape(L, 2, 2, half * HEAD_DIM)

    xa_wq = (p["xa_w_q"] * qk_scale).reshape(L, D_MODEL, XA_HEADS, HEAD_DIM).transpose(0, 2, 1, 3).astype(BF16)
    xa_wo = p["xa_w_o"].reshape(L, XA_HEADS, HEAD_DIM, D_MODEL).astype(BF16)

    row = lambda a: a[:, None, :]
    return dict(
        w_all=w_all, mix_norm_pre=row(p["mix_norm_pre"]),
        wq_all=wq_all, wkv_all=wkv_all, mla_q_norm=row(p["mla_q_norm"]), mla_kv_norm=row(p["mla_kv_norm"]),
        cmp_w=cmp_w, cmp_pe=cmp_pe, swa_sinks=p["swa_sinks"], fox_f_bias=p["fox_f_bias"][:, :, None],
        w_gate=p["w_gate"].astype(BF16), w_branch=p["w_branch"].astype(BF16), w_out=p["w_out"].astype(BF16),
        mix_norm_post=row(p["mix_norm_post"]),
        xa_norm_pre=row(p["xa_norm_pre"]), xa_mem_norm=row(p["xa_mem_norm"]), xa_wq=xa_wq,
        xa_wkv=p["xa_w_kv"].astype(BF16), xa_wo=xa_wo, xa_norm_post=row(p["xa_norm_post"]),
        ffn_norm_pre=row(p["ffn_norm_pre"]), ffn_wg=p["ffn_w_in"][:, :, :D_FF].astype(BF16),
        ffn_wu=p["ffn_w_in"][:, :, D_FF:].astype(BF16), ffn_wo=p["ffn_w_out"].astype(BF16),
        ffn_norm_post=row(p["ffn_norm_post"]),
    )


FLASH_TQ = 512
FLASH_TK = 512
BAND_TQ = LANES


def _per_row(vals_hg, tq, lanes=1):
    Hk, G = vals_hg.shape
    return jnp.broadcast_to(vals_hg[:, :, None, None], (Hk, G, tq, lanes)).reshape(Hk, G * tq, lanes)


def _layer(x, w, c):
    B, S, D = x.shape
    M = B * S
    Hk, G, Dh = NSA_KV_HEADS, NSA_GROUP, HEAD_DIM

    z_mla, z_small, z_cmp, *heads = _inproj(x, w["mix_norm_pre"], w["w_all"])
    hm = dict(zip(MAIN_ORDER, heads))
    grouped = lambda t: t.reshape(B, Hk, G, S, t.shape[-1])
    single = lambda t: t[:, :, None]
    flat = lambda o: o.reshape(B, -1, S, o.shape[-1])

    ng = z_small[:, :, 64:88].reshape(B, S, Hk, G, 3).transpose(4, 0, 2, 3, 1)[..., None]
    nq = grouped(hm["nq"])
    rows = z_cmp.reshape(B, 2 * Hk, S // NSA_CMP_STRIDE, NSA_CMP_STRIDE * Dh)
    kvc = _nsa_compress(rows, w["cmp_pe"], w["cmp_w"])
    oc, selt = _nsa_cmp(nq, kvc, c["pos_col"], c["pc_row"], c["nsa_slopes_cmp"], ng[0], c["overlap_t"])
    selb = selt.transpose(0, 1, 3, 2)
    o_sel = _flash(
        AttnCfg(tq=FLASH_TQ // G, tk=FLASH_TK, alibi=True, gate=True, add=True, selaug=True, out_dtype=F32),
        nq, hm["nks"], hm["nvs"], selb=selb, pos_col=c["pos_col"], pos_rows=c["pos_rows_flash"],
        slopes=c["nsa_slopes_flash"], gate=ng[1], add=oc)
    y_nsa = _band(
        AttnCfg(tq=BAND_TQ, window=NSA_WINDOW, alibi=True, gate=True, add=True),
        nq, hm["nkw"], hm["nvw"], pos_col=c["pos_col"], pos_rows=c["pos_rows_band"],
        slopes=c["nsa_slopes_band"], gate=ng[2], add=o_sel)

    y_swa = _band(
        AttnCfg(tq=BAND_TQ, window=SWA_WINDOW, alibi=True, sink=True),
        grouped(hm["sq"]), hm["sk"], hm["sv"], pos_col=c["pos_col"],
        pos_rows=c["pos_rows_band"], slopes=c["swa_slopes_band"],
        sinks=_per_row(w["swa_sinks"].reshape(SWA_KV_HEADS, SWA_GROUP), BAND_TQ))

    q_mla, k_mla, v_mla = _mla_prep(z_mla, z_small, c["cos"], c["sin"], w["mla_q_norm"], w["mla_kv_norm"],
                                    w["wq_all"], w["wkv_all"])
    y_mla = _flash(AttnCfg(tq=FLASH_TQ, tk=FLASH_TK), single(q_mla), k_mla, v_mla)

    f_logit = z_small[:, :, 88:96].transpose(0, 2, 1)
    cum = _fox_cumsum(f_logit, w["fox_f_bias"])
    y_fox = _flash(
        AttnCfg(tq=FLASH_TQ, tk=FLASH_TK, decay=True), single(hm["fq"]), hm["fk"], hm["fv"],
        cum_col=cum[:, :, None, :, None], cum_rows=cum.reshape(B, FOX_HEADS, S // FLASH_TK, 1, FLASH_TK))

    x1 = _merge(x, [flat(y_nsa), flat(y_swa), flat(y_mla), flat(y_fox)], w["mix_norm_pre"], w["w_gate"],
                w["w_branch"], w["w_out"], w["mix_norm_post"])

    kv = _mem_kv(c["mem"], w["xa_mem_norm"], w["xa_wkv"])
    kv = kv.reshape(B, -1, 2 * XA_HEADS, Dh).transpose(0, 2, 1, 3)
    x2 = _xattn(x1, kv, w["xa_norm_pre"], w["xa_wq"], w["xa_wo"], w["xa_norm_post"])

    x3 = _ffn(x2.reshape(M, D), w["ffn_norm_pre"], w["ffn_wg"], w["ffn_wu"], w["ffn_wo"], w["ffn_norm_post"])
    return x3.reshape(B, S, D)


def _constants(mem, positions, S):
    B = positions.shape[0]
    pos = positions.astype(F32)
    half = MLA_ROPE_DIM // 2
    inv = (ROPE_THETA ** (-np.arange(half, dtype=np.float32) / half)).astype(np.float32)
    inv_row = jnp.asarray(np.tile(inv, 2 * MLA_HEADS)[None, :])
    cos, sin = _rope_tables(pos.reshape(B * S, 1), inv_row)
    nc = S // NSA_CMP_STRIDE
    n_valid = (S - NSA_CMP_LEN) // NSA_CMP_STRIDE + 1
    assert n_valid == nc - 1
    c_end = np.minimum(np.arange(nc) * NSA_CMP_STRIDE + NSA_CMP_LEN - 1, S - 1)
    n_sel = S // NSA_SEL_LEN
    c_start = np.arange(nc) * NSA_CMP_STRIDE
    sel_start = np.arange(n_sel) * NSA_SEL_LEN
    overlap = ((c_start[:, None] < sel_start[None, :] + NSA_SEL_LEN)
               & (c_start[:, None] + NSA_CMP_LEN - 1 >= sel_start[None, :])
               & (np.arange(nc)[:, None] < n_valid)).astype(np.float32)
    nsa_slopes = jnp.asarray(_alibi_slopes(NSA_HEADS).reshape(NSA_KV_HEADS, NSA_GROUP))
    swa_slopes = jnp.asarray(_alibi_slopes(SWA_HEADS).reshape(SWA_KV_HEADS, SWA_GROUP))
    return dict(
        mem=mem, cos=cos.reshape(B, S, -1), sin=sin.reshape(B, S, -1),
        pos_col=pos[:, :, None],
        pos_rows_flash=pos.reshape(B, S // FLASH_TK, 1, FLASH_TK),
        pos_rows_band=pos.reshape(B, S // LANES, 1, LANES),
        pc_row=pos[:, c_end][:, None, :],
        overlap_t=jnp.asarray(overlap.T),
        nsa_slopes_cmp=_per_row(nsa_slopes, NSA_CMP_TQ),
        nsa_slopes_flash=_per_row(nsa_slopes, FLASH_TQ // NSA_GROUP, LANES),
        nsa_slopes_band=_per_row(nsa_slopes, BAND_TQ),
        swa_slopes_band=_per_row(swa_slopes, BAND_TQ),
    )


def kernel(x, mem, positions, mix_norm_pre, w_in, nsa_cmp_w, nsa_cmp_pe, swa_sinks, mla_q_norm, mla_w_q_up,
           mla_kv_norm, mla_w_kv_up, fox_f_bias, w_branch, w_gate, w_out, mix_norm_post, xa_norm_pre,
           xa_mem_norm, xa_w_q, xa_w_kv, xa_w_o, xa_norm_post, ffn_norm_pre, ffn_w_in, ffn_w_out,
           ffn_norm_post):
    params = dict(
        mix_norm_pre=mix_norm_pre, w_in=w_in, nsa_cmp_w=nsa_cmp_w, nsa_cmp_pe=nsa_cmp_pe, swa_sinks=swa_sinks,
        mla_q_norm=mla_q_norm, mla_w_q_up=mla_w_q_up, mla_kv_norm=mla_kv_norm, mla_w_kv_up=mla_w_kv_up,
        fox_f_bias=fox_f_bias, w_branch=w_branch, w_gate=w_gate, w_out=w_out, mix_norm_post=mix_norm_post,
        xa_norm_pre=xa_norm_pre, xa_mem_norm=xa_mem_norm, xa_w_q=xa_w_q, xa_w_kv=xa_w_kv, xa_w_o=xa_w_o,
        xa_norm_post=xa_norm_post, ffn_norm_pre=ffn_norm_pre, ffn_w_in=ffn_w_in, ffn_w_out=ffn_w_out,
        ffn_norm_post=ffn_norm_post)
    S = x.shape[1]
    weights = _prep_weights(params)
    consts = _constants(mem, positions, S)

    def body(xc, w):
        return _layer(xc, w, consts), None

    out, _ = lax.scan(body, x, weights)
    return out
```

```python
import functools
from typing import NamedTuple

import numpy as np
import jax
import jax.numpy as jnp
from jax import lax
from jax.experimental import pallas as pl
from jax.experimental.pallas import tpu as pltpu

F32 = jnp.float32
BF16 = jnp.bfloat16

D_MODEL = 1024
HEAD_DIM = 64
NORM_EPS = 1e-6
NEG_INF = -1e30
FORCE_SCORE = 1e9
LANES = 128

NSA_HEADS = 8
NSA_KV_HEADS = 2
NSA_GROUP = NSA_HEADS // NSA_KV_HEADS
NSA_CMP_LEN = 32
NSA_CMP_STRIDE = 16
NSA_SEL_LEN = 64
NSA_TOP_N = 8
NSA_WINDOW = 512

SWA_HEADS = 8
SWA_KV_HEADS = 2
SWA_GROUP = SWA_HEADS // SWA_KV_HEADS
SWA_WINDOW = 128

MLA_HEADS = 8
MLA_Q_RANK = 256
MLA_KV_RANK = 128
MLA_NOPE_DIM = 64
MLA_ROPE_DIM = 32
MLA_V_DIM = 64
MLA_QK_DIM = MLA_NOPE_DIM + MLA_ROPE_DIM
ROPE_THETA = 10000.0

FOX_HEADS = 8
XA_HEADS = 4
N_BRANCH = 4
BRANCH_WIDTH = 512
D_FF = ((8 * D_MODEL + 3 * 256 - 1) // (3 * 256)) * 256

IN_SPLITS = (512, 128, 128, 128, 128, 128, 128, 24,
             512, 128, 128,
             MLA_Q_RANK, MLA_KV_RANK, MLA_ROPE_DIM,
             512, 512, 512, FOX_HEADS)
IN_NAMES = ("nq", "nkc", "nvc", "nks", "nvs", "nkw", "nvw", "ng",
            "sq", "sk", "sv", "mcq", "mckv", "mkr", "fq", "fk", "fv", "ff")

VMEM_LIMIT_BYTES = 56 * 1024 * 1024

MAIN_ORDER = ("nq", "nks", "nvs", "nkw", "nvw", "sq", "sk", "sv", "fq", "fk", "fv")
MAIN_COLS = 3328
SMALL_COLS = 128


def _cparams(sem):
    return pltpu.CompilerParams(dimension_semantics=sem, vmem_limit_bytes=VMEM_LIMIT_BYTES)


def _rms(x, g):
    return x * lax.rsqrt(jnp.mean(x * x, axis=-1, keepdims=True) + NORM_EPS) * g


def _dot(a, b):
    return jnp.dot(a, b, preferred_element_type=F32)


def _dot_nt(a, b, precision=None):
    return lax.dot_general(a, b, (((1,), (1,)), ((), ())), preferred_element_type=F32,
                           precision=precision)


def _rep(x, n):
    return x if n == 1 else jnp.concatenate([x] * n, axis=1)


F32_COLS = MLA_Q_RANK + MLA_KV_RANK + 4 * HEAD_DIM + SMALL_COLS
MAIN_HEADS = tuple(8 if n in ("nq", "sq", "fq", "fk", "fv") else 2 for n in MAIN_ORDER)
PROJ_CHUNK = 512
FLASH_V_NAMES = ("nvs", "fv")


def _ones_column(rows):
    return jnp.where(lax.broadcasted_iota(jnp.int32, (rows, HEAD_DIM), 1) == 0, 1.0, 0.0)


def _inproj_kernel(x_ref, g_ref, w_ref, zmla_ref, zsmall_ref, cmp_ref, *head_refs):
    tm = x_ref.shape[1]
    h = _rms(x_ref[0], g_ref[...]).astype(BF16)
    n_mla = MLA_Q_RANK + MLA_KV_RANK
    zmla_ref[0] = _dot(h, w_ref[:, :n_mla])
    r = _dot(h, w_ref[:, n_mla:F32_COLS])
    for f in range(4):
        cmp_ref[0, f] = r[:, f * HEAD_DIM:(f + 1) * HEAD_DIM]
    zsmall_ref[0] = r[:, 4 * HEAD_DIM:]

    n_sel = head_refs[MAIN_ORDER.index("nks")].shape[3] - HEAD_DIM
    t = pl.program_id(1) * tm + lax.broadcasted_iota(jnp.int32, (tm, n_sel), 0)
    blk = lax.shift_right_logical(t, NSA_SEL_LEN.bit_length() - 1)
    onehot = jnp.where(blk == lax.broadcasted_iota(jnp.int32, (tm, n_sel), 1), 1.0, 0.0)

    ones_col = _ones_column(tm)
    extra = {"nks": onehot, **{n: ones_col for n in FLASH_V_NAMES}}
    dest = [(ref, hd, extra.get(name)) for ref, nh, name in zip(head_refs, MAIN_HEADS, MAIN_ORDER)
            for hd in range(nh)]
    for c in range(0, MAIN_COLS, PROJ_CHUNK):
        w = min(PROJ_CHUNK, MAIN_COLS - c)
        res = _dot(h, w_ref[:, F32_COLS + c:F32_COLS + c + w])
        for k in range(w // HEAD_DIM):
            ref, hd, appended = dest[c // HEAD_DIM + k]
            piece = res[:, k * HEAD_DIM:(k + 1) * HEAD_DIM]
            if appended is not None:
                piece = jnp.concatenate([piece, appended], axis=-1)
            ref[0, hd] = piece.astype(ref.dtype)


def _inproj(x, g, w, tm=512):
    B, S, D = x.shape
    N = w.shape[1]
    n_sel = S // NSA_SEL_LEN
    assert N == F32_COLS + MAIN_COLS and S % tm == 0
    tok = lambda n: pl.BlockSpec((1, tm, n), lambda b, i: (b, i, 0))
    heads = lambda nh, d: pl.BlockSpec((1, nh, tm, d), lambda b, i: (b, 0, i, 0))
    head_dims = [HEAD_DIM + n_sel if n == "nks" else 2 * HEAD_DIM if n in FLASH_V_NAMES else HEAD_DIM
                 for n in MAIN_ORDER]
    return pl.pallas_call(
        _inproj_kernel,
        grid=(B, S // tm),
        in_specs=[pl.BlockSpec((1, tm, D), lambda b, i: (b, i, 0)),
                  pl.BlockSpec((1, D), lambda b, i: (0, 0)),
                  pl.BlockSpec((D, N), lambda b, i: (0, 0))],
        out_specs=[tok(MLA_Q_RANK + MLA_KV_RANK), tok(SMALL_COLS), heads(4, HEAD_DIM)]
                  + [heads(nh, d) for nh, d in zip(MAIN_HEADS, head_dims)],
        out_shape=[jax.ShapeDtypeStruct((B, S, MLA_Q_RANK + MLA_KV_RANK), F32),
                   jax.ShapeDtypeStruct((B, S, SMALL_COLS), F32),
                   jax.ShapeDtypeStruct((B, 4, S, HEAD_DIM), F32)]
                  + [jax.ShapeDtypeStruct((B, nh, S, d), BF16) for nh, d in zip(MAIN_HEADS, head_dims)],
        compiler_params=_cparams(("parallel", "parallel")),
        name="inproj",
    )(x, g, w)


def _rope_table_kernel(pos_ref, inv_ref, cos_ref, sin_ref):
    ang = pos_ref[...] * inv_ref[...]
    cos_ref[...] = jnp.cos(ang)
    sin_ref[...] = jnp.sin(ang)


def _rope_tables(pos_col, inv_row, tm=512):
    M = pos_col.shape[0]
    N = inv_row.shape[1]
    return pl.pallas_call(
        _rope_table_kernel,
        grid=(M // tm,),
        in_specs=[pl.BlockSpec((tm, 1), lambda i: (i, 0)),
                  pl.BlockSpec((1, N), lambda i: (0, 0))],
        out_specs=[pl.BlockSpec((tm, N), lambda i: (i, 0))] * 2,
        out_shape=[jax.ShapeDtypeStruct((M, N), F32)] * 2,
        compiler_params=_cparams(("parallel",)),
        name="rope_tables",
    )(pos_col, inv_row)


def _mla_prep_kernel(z_ref, zs_ref, cos_ref, sin_ref, qg_ref, kvg_ref, wq_ref, wkv_ref,
                     q_ref, k_ref, v_ref):
    nq = MLA_HEADS * MLA_NOPE_DIM
    nr = MLA_HEADS * MLA_ROPE_DIM
    scale = MLA_QK_DIM ** -0.5
    z = z_ref[0]
    cq = _rms(z[:, :MLA_Q_RANK], qg_ref[...]).astype(BF16)
    ckv = _rms(z[:, MLA_Q_RANK:], kvg_ref[...]).astype(BF16)
    cos = cos_ref[0]
    sin = sin_ref[0]
    q_nope = _dot(cq, wq_ref[:, :nq]) * scale
    q_rot = (_dot(cq, wq_ref[:, nq:nq + nr]) * cos + _dot(cq, wq_ref[:, nq + nr:]) * sin) * scale
    k_nope = _dot(ckv, wkv_ref[:, :nq])
    v = _dot(ckv, wkv_ref[:, nq:])
    zs = zs_ref[0]
    r, d = MLA_ROPE_DIM, MLA_NOPE_DIM
    k_rot = zs[:, :r] * cos[:, :r] + zs[:, r:2 * r] * sin[:, :r]
    ones_col = _ones_column(z.shape[0])
    for h in range(MLA_HEADS):
        q_ref[0, h] = jnp.concatenate([q_nope[:, h * d:(h + 1) * d], q_rot[:, h * r:(h + 1) * r]],
                                      axis=-1).astype(BF16)
        k_ref[0, h] = jnp.concatenate([k_nope[:, h * d:(h + 1) * d], k_rot], axis=-1).astype(BF16)
        v_ref[0, h] = jnp.concatenate([v[:, h * MLA_V_DIM:(h + 1) * MLA_V_DIM], ones_col],
                                      axis=-1).astype(BF16)


def _mla_prep(z_mla, z_small, cos, sin, qg, kvg, wq, wkv, tm=512):
    B, S, _ = z_mla.shape
    nr = MLA_HEADS * MLA_ROPE_DIM
    tok = lambda n: pl.BlockSpec((1, tm, n), lambda b, i: (b, i, 0))
    heads = lambda d: pl.BlockSpec((1, MLA_HEADS, tm, d), lambda b, i: (b, 0, i, 0))
    full = lambda a: pl.BlockSpec(a.shape, lambda b, i: (0,) * a.ndim)
    return pl.pallas_call(
        _mla_prep_kernel,
        grid=(B, S // tm),
        in_specs=[tok(z_mla.shape[2]), tok(SMALL_COLS), tok(nr), tok(nr),
                  full(qg), full(kvg), full(wq), full(wkv)],
        out_specs=[heads(MLA_QK_DIM), heads(MLA_QK_DIM), heads(2 * MLA_V_DIM)],
        out_shape=[jax.ShapeDtypeStruct((B, MLA_HEADS, S, d), BF16)
                   for d in (MLA_QK_DIM, MLA_QK_DIM, 2 * MLA_V_DIM)],
        compiler_params=_cparams(("parallel", "parallel")),
        name="mla_prep",
    )(z_mla, z_small, cos, sin, qg, kvg, wq, wkv)


CUM_CHUNK = 512


def _fox_cumsum_kernel(f_ref, b_ref, u_ref, o_ref):
    x = f_ref[0] + b_ref[...]
    lf = jnp.minimum(x, 0.0) - jnp.log1p(jnp.exp(-jnp.abs(x)))
    S = lf.shape[1]
    carry = jnp.zeros((lf.shape[0], 1), F32)
    for c in range(0, S, CUM_CHUNK):
        loc = jnp.dot(lf[:, c:c + CUM_CHUNK], u_ref[...], preferred_element_type=F32,
                      precision=lax.Precision.HIGHEST) + carry
        o_ref[0, :, c:c + CUM_CHUNK] = loc
        carry = loc[:, CUM_CHUNK - 1:CUM_CHUNK]


def _fox_cumsum(f_bhs, bias_col):
    B, H, S = f_bhs.shape
    assert S % CUM_CHUNK == 0
    upper = jnp.asarray(np.triu(np.ones((CUM_CHUNK, CUM_CHUNK), np.float32)))
    return pl.pallas_call(
        _fox_cumsum_kernel,
        grid=(B,),
        in_specs=[pl.BlockSpec((1, H, S), lambda b: (b, 0, 0)),
                  pl.BlockSpec((H, 1), lambda b: (0, 0)),
                  pl.BlockSpec((CUM_CHUNK, CUM_CHUNK), lambda b: (0, 0))],
        out_specs=pl.BlockSpec((1, H, S), lambda b: (b, 0, 0)),
        out_shape=jax.ShapeDtypeStruct((B, H, S), F32),
        compiler_params=_cparams(("parallel",)),
        name="fox_cumsum",
    )(f_bhs, bias_col, upper)


def _nsa_compress_kernel(r_ref, pe_ref, w_ref, o_ref):
    r = r_ref[0, 0]
    first = _dot((r + pe_ref[0, 0:1, :]).astype(BF16), w_ref[0, 0])
    second = _dot((r + pe_ref[0, 1:2, :]).astype(BF16), w_ref[0, 1])
    nc = r.shape[0]
    nxt = pltpu.roll(second, nc - 1, 0)
    row = lax.broadcasted_iota(jnp.int32, (nc, 1), 0)
    o_ref[0, 0] = jnp.where(row < nc - 1, first + nxt, 0.0).astype(o_ref.dtype)


def _nsa_compress(rows, pe, w):
    B, F, NC, K = rows.shape
    return pl.pallas_call(
        _nsa_compress_kernel,
        grid=(B, F),
        in_specs=[pl.BlockSpec((1, 1, NC, K), lambda b, f: (b, f, 0, 0)),
                  pl.BlockSpec((1, 2, K), lambda b, f: (f // NSA_KV_HEADS, 0, 0)),
                  pl.BlockSpec((1, 2, K, HEAD_DIM), lambda b, f: (f // NSA_KV_HEADS, 0, 0, 0))],
        out_specs=pl.BlockSpec((1, 1, NC, HEAD_DIM), lambda b, f: (b, f, 0, 0)),
        out_shape=jax.ShapeDtypeStruct((B, F, NC, HEAD_DIM), BF16),
        compiler_params=_cparams(("parallel", "parallel")),
        name="nsa_compress",
    )(rows, pe, w)


NSA_CMP_TQ = 256


def _nsa_cmp_kernel(q_ref, kc_ref, vc_ref, pq_ref, pc_ref, sl_ref, gate_ref, ovt_ref,
                    oc_ref, selt_ref):
    G, tq, Dh = q_ref.shape[2:]
    R = G * tq
    i = pl.program_id(2)
    q = q_ref[0, 0].reshape(R, Dh)
    kc = kc_ref[0, 0]
    nc = kc.shape[0]
    n_sel = ovt_ref.shape[0]
    s = _dot_nt(q, kc)
    pq = jnp.concatenate([pq_ref[0]] * G, axis=0)
    s = s - sl_ref[0] * jnp.abs(pq - pc_ref[0])
    t_row = (i * tq + lax.broadcasted_iota(jnp.int32, (G, tq, 1), 1)).reshape(R, 1)
    c_col = lax.broadcasted_iota(jnp.int32, (1, nc), 1)
    c_end = jnp.where(c_col < nc - 1, c_col * NSA_CMP_STRIDE + (NSA_CMP_LEN - 1), jnp.int32(2 ** 30))
    s = jnp.where(c_end <= t_row, s, NEG_INF)
    m = jnp.max(s, axis=-1, keepdims=True)
    e = jnp.exp(s - m)
    l = jnp.sum(e, axis=-1, keepdims=True)
    p = e * jnp.where(m > 0.5 * NEG_INF, 1.0 / l, 0.0)
    o = _dot(p.astype(BF16), vc_ref[0, 0])
    gate = jax.nn.sigmoid(gate_ref[0, 0].reshape(R, 1))
    oc_ref[0, 0] = (gate * o).reshape(G, tq, Dh)
    p_sum = jnp.sum(p.reshape(G, tq, nc), axis=0)
    imp = _dot_nt(ovt_ref[...], p_sum, precision=lax.Precision.HIGHEST)
    t_q = i * tq + lax.broadcasted_iota(jnp.int32, (1, tq), 1)
    jt = lax.shift_right_logical(t_q, NSA_SEL_LEN.bit_length() - 1)
    j = lax.broadcasted_iota(jnp.int32, (n_sel, 1), 0)
    forced = (j == 0) | (j == jt) | (j == jt - 1)
    causal = j <= jt
    val = jnp.where(forced, FORCE_SCORE, imp)
    val = jnp.where(causal, val, NEG_INF)
    sel = jnp.zeros((n_sel, tq), jnp.int32)
    for _ in range(min(NSA_TOP_N, n_sel)):
        best = jnp.max(val, axis=0, keepdims=True)
        idx = jnp.min(jnp.where(val == best, j, n_sel), axis=0, keepdims=True)
        pick = j == idx
        sel = jnp.where(pick, 1, sel)
        val = jnp.where(pick, -jnp.inf, val)
    keep = (sel > 0) & causal
    selt_ref[0, 0] = jnp.where(keep, 0.0, NEG_INF).astype(selt_ref.dtype)


def _nsa_cmp(q, kvc, pos_col, pc_row, slopes, gate, overlap_t):
    B, Hk, G, S, Dh = q.shape
    NC = kvc.shape[2]
    n_sel = overlap_t.shape[0]
    assert NSA_SEL_LEN & (NSA_SEL_LEN - 1) == 0
    tq = NSA_CMP_TQ
    R = G * tq
    return pl.pallas_call(
        _nsa_cmp_kernel,
        grid=(B, Hk, S // tq),
        in_specs=[pl.BlockSpec((1, 1, G, tq, Dh), lambda b, h, i: (b, h, 0, i, 0)),
                  pl.BlockSpec((1, 1, NC, Dh), lambda b, h, i: (b, h, 0, 0)),
                  pl.BlockSpec((1, 1, NC, Dh), lambda b, h, i: (b, h + Hk, 0, 0)),
                  pl.BlockSpec((1, tq, 1), lambda b, h, i: (b, i, 0)),
                  pl.BlockSpec((1, 1, NC), lambda b, h, i: (b, 0, 0)),
                  pl.BlockSpec((1, R, 1), lambda b, h, i: (h, 0, 0)),
                  pl.BlockSpec((1, 1, G, tq, 1), lambda b, h, i: (b, h, 0, i, 0)),
                  pl.BlockSpec((n_sel, NC), lambda b, h, i: (0, 0))],
        out_specs=[pl.BlockSpec((1, 1, G, tq, Dh), lambda b, h, i: (b, h, 0, i, 0)),
                   pl.BlockSpec((1, 1, n_sel, tq), lambda b, h, i: (b, h, 0, i))],
        out_shape=[jax.ShapeDtypeStruct((B, Hk, G, S, Dh), F32),
                   jax.ShapeDtypeStruct((B, Hk, n_sel, S), BF16)],
        compiler_params=_cparams(("parallel", "parallel", "parallel")),
        name="nsa_cmp",
    )(q, kvc, kvc, pos_col, pc_row, slopes, gate, overlap_t)


class AttnCfg(NamedTuple):
    tq: int
    tk: int = 0
    window: int = 0
    alibi: bool = False
    decay: bool = False
    sink: bool = False
    gate: bool = False
    add: bool = False
    selaug: bool = False
    out_dtype: object = BF16


def _epilogue(cfg, out, gate_ref, add_ref, o_ref, G, tq):
    R, Dv = out.shape
    if cfg.gate:
        out = out * jax.nn.sigmoid(gate_ref[0, 0].reshape(R, 1))
    if cfg.add:
        out = out + add_ref[0, 0].reshape(R, Dv)
    o_ref[0, 0] = out.reshape(G, tq, Dv).astype(o_ref.dtype)


def _flash_kernel(*refs, cfg):
    it = iter(refs)
    q_ref, k_ref, v_ref = next(it), next(it), next(it)
    pq_ref = pk_ref = sl_ref = cq_ref = ck_ref = gate_ref = add_ref = None
    selb_ref = next(it) if cfg.selaug else None
    if cfg.alibi:
        pq_ref, pk_ref, sl_ref = next(it), next(it), next(it)
    if cfg.decay:
        cq_ref, ck_ref = next(it), next(it)
    if cfg.gate:
        gate_ref = next(it)
    if cfg.add:
        add_ref = next(it)
    o_ref = next(it)
    m_s, acc_s, sa_s, sb_s = (next(it) for _ in range(4))
    pq_s = next(it) if cfg.alibi else None
    cq_s = next(it) if cfg.decay else None

    G, tq, Dq = q_ref.shape[2:]
    Da = v_ref.shape[3]
    Dv = Da // 2
    tk = cfg.tk
    nrep = tk // LANES
    R = G * tq
    i = pl.program_id(2)
    q = q_ref[0, 0].reshape(R, Dq)
    if cfg.selaug:
        bias = jnp.concatenate([selb_ref[0, 0].astype(F32)] * G, axis=0)
        q = jnp.concatenate([q.astype(F32), bias], axis=-1).astype(BF16)

    m_s[...] = jnp.full((R, LANES), NEG_INF, F32)
    acc_s[...] = jnp.zeros((R, Da), F32)
    if cfg.alibi:
        pq_s[...] = jnp.broadcast_to(jnp.concatenate([pq_ref[0]] * G, axis=0), (R, LANES))
    if cfg.decay:
        cq_s[...] = jnp.broadcast_to(cq_ref[0, 0].reshape(R, 1), (R, LANES))

    def scores(j, s_ref):
        k0 = pl.multiple_of(j * tk, tk)
        s = _dot_nt(q, k_ref[0, 0, pl.ds(k0, tk), :])
        if cfg.alibi:
            s = s - _rep(sl_ref[0], nrep) * jnp.abs(_rep(pq_s[...], nrep) - pk_ref[0, j])
        if cfg.decay:
            s = s + (_rep(cq_s[...], nrep) - ck_ref[0, 0, j])
        s_ref[...] = s

    def update(j, s_ref, masked):
        s = s_ref[...]
        if masked:
            t_row = (i * tq + lax.broadcasted_iota(jnp.int32, (G, tq, tk), 1)).reshape(R, tk)
            col = j * tk + lax.broadcasted_iota(jnp.int32, (R, tk), 1)
            s = jnp.where(col <= t_row, s, NEG_INF)
        m_prev = m_s[...]
        m_new = jnp.maximum(m_prev, jnp.max(s, axis=-1, keepdims=True))
        alpha = jnp.exp(m_prev - m_new)
        p = jnp.exp((s - _rep(m_new, nrep)).astype(BF16))
        k0 = pl.multiple_of(j * tk, tk)
        acc_s[...] = _rep(alpha, Da // LANES) * acc_s[...] + _dot(p, v_ref[0, 0, pl.ds(k0, tk), :])
        m_s[...] = m_new

    n = ((i + 1) * tq - 1) // tk
    scores(0, sa_s)

    def pair(pp, c):
        j = 2 * pp
        scores(j + 1, sb_s)
        update(j, sa_s, False)
        scores(j + 2, sa_s)
        update(j + 1, sb_s, False)
        return c
    lax.fori_loop(0, n // 2, pair, 0)

    @pl.when(n % 2 == 1)
    def _():
        scores(n, sb_s)
        update(n - 1, sa_s, False)
        update(n, sb_s, True)

    @pl.when(n % 2 == 0)
    def _():
        update(n, sa_s, True)

    acc = acc_s[...]
    out = acc[:, :Dv] / acc[:, Dv:Dv + 1]
    _epilogue(cfg, out, gate_ref, add_ref, o_ref, G, tq)


def _flash(cfg, q, k, v, *, selb=None, pos_col=None, pos_rows=None, slopes=None, cum_col=None,
           cum_rows=None, gate=None, add=None):
    B, Hk, G, S, Dq = q.shape
    Dk, Da = k.shape[3], v.shape[3]
    Dv = Da // 2
    tq, tk = cfg.tq, cfg.tk
    assert S % tq == 0 and S % tk == 0 and tk % tq == 0 and tk % LANES == 0 and Da % LANES == 0
    assert Dk == Dq + (selb.shape[3] if cfg.selaug else 0)
    R = G * tq
    nk = S // tk
    qmap = lambda b, h, i: (b, h, 0, i, 0)
    kvmap = lambda b, h, i: (b, h, 0, 0)
    args = [q, k, v]
    specs = [pl.BlockSpec((1, 1, G, tq, Dq), qmap),
             pl.BlockSpec((1, 1, S, Dk), kvmap),
             pl.BlockSpec((1, 1, S, Da), kvmap)]
    scratch = [pltpu.VMEM((R, LANES), F32), pltpu.VMEM((R, Da), F32),
               pltpu.VMEM((R, tk), F32), pltpu.VMEM((R, tk), F32)]
    if cfg.selaug:
        args += [selb]
        specs += [pl.BlockSpec((1, 1, tq, Dk - Dq), lambda b, h, i: (b, h, i, 0))]
    if cfg.alibi:
        args += [pos_col, pos_rows, slopes]
        specs += [pl.BlockSpec((1, tq, 1), lambda b, h, i: (b, i, 0)),
                  pl.BlockSpec((1, nk, 1, tk), lambda b, h, i: (b, 0, 0, 0)),
                  pl.BlockSpec((1, R, LANES), lambda b, h, i: (h, 0, 0))]
        scratch.append(pltpu.VMEM((R, LANES), F32))
    if cfg.decay:
        args += [cum_col, cum_rows]
        specs += [pl.BlockSpec((1, 1, G, tq, 1), qmap),
                  pl.BlockSpec((1, 1, nk, 1, tk), lambda b, h, i: (b, h, 0, 0, 0))]
        scratch.append(pltpu.VMEM((R, LANES), F32))
    if cfg.gate:
        args += [gate]
        specs += [pl.BlockSpec((1, 1, G, tq, 1), qmap)]
    if cfg.add:
        args += [add]
        specs += [pl.BlockSpec((1, 1, G, tq, Dv), qmap)]
    return pl.pallas_call(
        functools.partial(_flash_kernel, cfg=cfg),
        grid=(B, Hk, S // tq),
        in_specs=specs,
        out_specs=pl.BlockSpec((1, 1, G, tq, Dv), qmap),
        out_shape=jax.ShapeDtypeStruct((B, Hk, G, S, Dv), cfg.out_dtype),
        scratch_shapes=scratch,
        compiler_params=_cparams(("parallel", "parallel", "parallel")),
        name="flash",
    )(*args)


def _band_kernel(*refs, cfg):
    it = iter(refs)
    q_ref, k_ref, v_ref, pq_ref, pk_ref, sl_ref = (next(it) for _ in range(6))
    sink_ref = next(it) if cfg.sink else None
    gate_ref = next(it) if cfg.gate else None
    add_ref = next(it) if cfg.add else None
    o_ref = next(it)

    G, tq, Dq = q_ref.shape[2:]
    R = G * tq
    W = cfg.window
    span = W + tq
    i = pl.program_id(2)
    start = pl.multiple_of(jnp.maximum(i * tq - W, 0), LANES)
    blk0 = lax.shift_right_logical(start, LANES.bit_length() - 1)
    q = q_ref[0, 0].reshape(R, Dq)
    s = _dot_nt(q, k_ref[0, 0, pl.ds(start, span), :])
    pq = jnp.concatenate([pq_ref[0]] * G, axis=0)
    slope = sl_ref[0]
    d = (lax.broadcasted_iota(jnp.int32, (G, tq, LANES), 1)
         - lax.broadcasted_iota(jnp.int32, (G, tq, LANES), 2)).reshape(R, LANES)
    off = i * tq - start
    chunks = []
    for c in range(span // LANES):
        sc = s[:, c * LANES:(c + 1) * LANES]
        sc = sc - slope * jnp.abs(pq - pk_ref[0, blk0 + c])
        lo = c * LANES - off
        sc = jnp.where(d >= lo, jnp.where(d < lo + W, sc, NEG_INF), NEG_INF)
        chunks.append(sc)
    s = jnp.concatenate(chunks, axis=1)
    m = jnp.max(s, axis=-1, keepdims=True)
    e = jnp.exp(s - m)
    l = jnp.sum(e, axis=-1, keepdims=True)
    acc = _dot(e.astype(BF16), v_ref[0, 0, pl.ds(start, span), :])
    if cfg.sink:
        sink = sink_ref[0]
        m_f = jnp.maximum(m, sink)
        w = jnp.exp(m - m_f)
        out = acc * (w / (l * w + jnp.exp(sink - m_f)))
    else:
        out = acc / l
    _epilogue(cfg, out, gate_ref, add_ref, o_ref, G, tq)


def _band(cfg, q, k, v, *, pos_col, pos_rows, slopes, sinks=None, gate=None, add=None):
    B, Hk, G, S, Dq = q.shape
    Dv = v.shape[3]
    tq = cfg.tq
    assert tq == LANES and cfg.window % LANES == 0 and S % tq == 0 and S >= cfg.window + tq
    R = G * tq
    qmap = lambda b, h, i: (b, h, 0, i, 0)
    kvmap = lambda b, h, i: (b, h, 0, 0)
    args = [q, k, v, pos_col, pos_rows, slopes]
    specs = [pl.BlockSpec((1, 1, G, tq, Dq), qmap),
             pl.BlockSpec((1, 1, S, Dq), kvmap),
             pl.BlockSpec((1, 1, S, Dv), kvmap),
             pl.BlockSpec((1, tq, 1), lambda b, h, i: (b, i, 0)),
             pl.BlockSpec((1, S // LANES, 1, LANES), lambda b, h, i: (b, 0, 0, 0)),
             pl.BlockSpec((1, R, 1), lambda b, h, i: (h, 0, 0))]
    if cfg.sink:
        args += [sinks]
        specs += [pl.BlockSpec((1, R, 1), lambda b, h, i: (h, 0, 0))]
    if cfg.gate:
        args += [gate]
        specs += [pl.BlockSpec((1, 1, G, tq, 1), qmap)]
    if cfg.add:
        args += [add]
        specs += [pl.BlockSpec((1, 1, G, tq, Dv), qmap)]
    return pl.pallas_call(
        functools.partial(_band_kernel, cfg=cfg),
        grid=(B, Hk, S // tq),
        in_specs=specs,
        out_specs=pl.BlockSpec((1, 1, G, tq, Dv), qmap),
        out_shape=jax.ShapeDtypeStruct((B, Hk, G, S, Dv), cfg.out_dtype),
        compiler_params=_cparams(("parallel", "parallel", "parallel")),
        name="band",
    )(*args)


def _merge_kernel(x_ref, y0_ref, y1_ref, y2_ref, y3_ref, gpre_ref, wg_ref, wb_ref, wo_ref, gpost_ref,
                  o_ref):
    x = x_ref[0]
    h = _rms(x, gpre_ref[...]).astype(BF16)
    merged = None
    for b, y_ref in enumerate((y0_ref, y1_ref, y2_ref, y3_ref)):
        gate = jax.nn.sigmoid(_dot(h, wg_ref[b]))
        y = jnp.concatenate([y_ref[0, hd] for hd in range(y_ref.shape[1])], axis=-1)
        br = _dot(y, wb_ref[b])
        merged = gate * br if merged is None else merged + gate * br
    y = _dot(merged.astype(BF16), wo_ref[...])
    o_ref[0] = x + _rms(y, gpost_ref[...])


def _merge(x, ys, gpre, wg, wb, wo, gpost, tm=512):
    B, S, D = x.shape
    full = lambda a: pl.BlockSpec(a.shape, lambda b, i: (0,) * a.ndim, pipeline_mode=pl.Buffered(1))
    yspec = lambda y: pl.BlockSpec((1, y.shape[1], tm, y.shape[3]), lambda b, i: (b, 0, i, 0))
    return pl.pallas_call(
        _merge_kernel,
        grid=(B, S // tm),
        in_specs=[pl.BlockSpec((1, tm, D), lambda b, i: (b, i, 0))] + [yspec(y) for y in ys]
                 + [full(gpre), full(wg), full(wb), full(wo), full(gpost)],
        out_specs=pl.BlockSpec((1, tm, D), lambda b, i: (b, i, 0)),
        out_shape=jax.ShapeDtypeStruct((B, S, D), F32),
        compiler_params=_cparams(("parallel", "parallel")),
        name="merge",
    )(x, *ys, gpre, wg, wb, wo, gpost)


def _mem_kv_kernel(mem_ref, g_ref, w_ref, o_ref):
    o_ref[0] = _dot(_rms(mem_ref[0], g_ref[...]).astype(BF16), w_ref[...]).astype(BF16)


def _mem_kv(mem, g, w):
    B, Mm, D = mem.shape
    N = w.shape[1]
    return pl.pallas_call(
        _mem_kv_kernel,
        grid=(B,),
        in_specs=[pl.BlockSpec((1, Mm, D), lambda b: (b, 0, 0)),
                  pl.BlockSpec((1, D), lambda b: (0, 0)),
                  pl.BlockSpec((D, N), lambda b: (0, 0))],
        out_specs=pl.BlockSpec((1, Mm, N), lambda b: (b, 0, 0)),
        out_shape=jax.ShapeDtypeStruct((B, Mm, N), BF16),
        compiler_params=_cparams(("parallel",)),
        name="mem_kv",
    )(mem, g, w)


def _xattn_kernel(x_ref, kv_ref, gpre_ref, wq_ref, wo_ref, gpost_ref, o_ref):
    x = x_ref[0]
    h = _rms(x, gpre_ref[...]).astype(BF16)
    q_all = _dot(h, wq_ref[...])
    outs = []
    for hd in range(XA_HEADS):
        q = q_all[:, hd * HEAD_DIM:(hd + 1) * HEAD_DIM].astype(BF16)
        s = _dot_nt(q, kv_ref[0, hd])
        m = jnp.max(s, axis=-1, keepdims=True)
        e = jnp.exp(s - m)
        p = e / jnp.sum(e, axis=-1, keepdims=True)
        outs.append(_dot(p.astype(BF16), kv_ref[0, XA_HEADS + hd]))
    y = _dot(jnp.concatenate(outs, axis=-1).astype(BF16), wo_ref[...])
    o_ref[0] = x + _rms(y, gpost_ref[...])


def _xattn(x, kv, gpre, wq, wo, gpost, tm=512):
    B, S, D = x.shape
    full = lambda a: pl.BlockSpec(a.shape, lambda b, i: (0,) * a.ndim)
    return pl.pallas_call(
        _xattn_kernel,
        grid=(B, S // tm),
        in_specs=[pl.BlockSpec((1, tm, D), lambda b, i: (b, i, 0)),
                  pl.BlockSpec((1,) + kv.shape[1:], lambda b, i: (b, 0, 0, 0)),
                  full(gpre), full(wq), full(wo), full(gpost)],
        out_specs=pl.BlockSpec((1, tm, D), lambda b, i: (b, i, 0)),
        out_shape=jax.ShapeDtypeStruct((B, S, D), F32),
        compiler_params=_cparams(("parallel", "parallel")),
        name="xattn",
    )(x, kv, gpre, wq, wo, gpost)


FFN_CHUNK = 256


def _ffn_kernel(x_ref, gpre_ref, wg_ref, wu_ref, wo_ref, gpost_ref, o_ref):
    x = x_ref[...]
    h = _rms(x, gpre_ref[...]).astype(BF16)
    y = None
    for c in range(0, D_FF, FFN_CHUNK):
        g = _dot(h, wg_ref[:, c:c + FFN_CHUNK])
        u = _dot(h, wu_ref[:, c:c + FFN_CHUNK])
        a = (g * jax.nn.sigmoid(g) * u).astype(BF16)
        part = _dot(a, wo_ref[c:c + FFN_CHUNK, :])
        y = part if y is None else y + part
    o_ref[...] = x + _rms(y, gpost_ref[...])


def _ffn(x2d, gpre, wg, wu, wo, gpost, tm=512):
    M, D = x2d.shape
    full = lambda a: pl.BlockSpec(a.shape, lambda i: (0,) * a.ndim, pipeline_mode=pl.Buffered(1))
    return pl.pallas_call(
        _ffn_kernel,
        grid=(M // tm,),
        in_specs=[pl.BlockSpec((tm, D), lambda i: (i, 0)),
                  full(gpre), full(wg), full(wu), full(wo), full(gpost)],
        out_specs=pl.BlockSpec((tm, D), lambda i: (i, 0)),
        out_shape=jax.ShapeDtypeStruct((M, D), F32),
        compiler_params=_cparams(("parallel",)),
        name="ffn",
    )(x2d, gpre, wg, wu, wo, gpost)


def _alibi_slopes(n):
    return (2.0 ** (-8.0 * np.arange(1, n + 1) / n)).astype(np.float32)


def _rotate_half_cols(w):
    half = w.shape[-1] // 2
    return jnp.concatenate([-w[..., half:], w[..., :half]], axis=-1)


def _prep_weights(p):
    L = p["w_in"].shape[0]
    offs = np.concatenate([[0], np.cumsum(IN_SPLITS)])
    col = {n: p["w_in"][:, :, offs[k]:offs[k + 1]] for k, n in enumerate(IN_NAMES)}
    qk_scale = HEAD_DIM ** -0.5
    for n in ("nq", "sq", "fq"):
        col[n] = col[n] * qk_scale
    pad = jnp.zeros((L, D_MODEL, SMALL_COLS - 2 * MLA_ROPE_DIM - 24 - FOX_HEADS), F32)
    small = jnp.concatenate([col["mkr"], _rotate_half_cols(col["mkr"]), col["ng"], col["ff"], pad], -1)
    w_all = jnp.concatenate([col["mcq"], col["mckv"], col["nkc"], col["nvc"], small]
                            + [col[n] for n in MAIN_ORDER], axis=-1).astype(BF16)

    wq = p["mla_w_q_up"].reshape(L, MLA_Q_RANK, MLA_HEADS, MLA_QK_DIM)
    wq_nope = wq[..., :MLA_NOPE_DIM].reshape(L, MLA_Q_RANK, -1)
    wq_rot = wq[..., MLA_NOPE_DIM:]
    wq_all = jnp.concatenate([wq_nope, wq_rot.reshape(L, MLA_Q_RANK, -1),
                              _rotate_half_cols(wq_rot).reshape(L, MLA_Q_RANK, -1)], -1).astype(BF16)
    wkv = p["mla_w_kv_up"].reshape(L, MLA_KV_RANK, MLA_HEADS, MLA_NOPE_DIM + MLA_V_DIM)
    wkv_all = jnp.concatenate([wkv[..., :MLA_NOPE_DIM].reshape(L, MLA_KV_RANK, -1),
                               wkv[..., MLA_NOPE_DIM:].reshape(L, MLA_KV_RANK, -1)], -1).astype(BF16)

    half = NSA_CMP_LEN // 2
    cmp_w = p["nsa_cmp_w"].reshape(L, 2, 2, half * HEAD_DIM, HEAD_DIM).astype(BF16)
    cmp_pe = p["nsa_cmp_pe"].reshape(L, 2, 2, half * HEAD_DIM)

    xa_wq = (p["xa_w_q"] * qk_scale).astype(BF16)
    xa_wo = p["xa_w_o"].astype(BF16)

    row = lambda a: a[:, None, :]
    return dict(
        w_all=w_all, mix_norm_pre=row(p["mix_norm_pre"]),
        wq_all=wq_all, wkv_all=wkv_all, mla_q_norm=row(p["mla_q_norm"]), mla_kv_norm=row(p["mla_kv_norm"]),
        cmp_w=cmp_w, cmp_pe=cmp_pe, swa_sinks=p["swa_sinks"], fox_f_bias=p["fox_f_bias"][:, :, None],
        w_gate=p["w_gate"].astype(BF16), w_branch=p["w_branch"].astype(BF16), w_out=p["w_out"].astype(BF16),
        mix_norm_post=row(p["mix_norm_post"]),
        xa_norm_pre=row(p["xa_norm_pre"]), xa_mem_norm=row(p["xa_mem_norm"]), xa_wq=xa_wq,
        xa_wkv=p["xa_w_kv"].astype(BF16), xa_wo=xa_wo, xa_norm_post=row(p["xa_norm_post"]),
        ffn_norm_pre=row(p["ffn_norm_pre"]), ffn_wg=p["ffn_w_in"][:, :, :D_FF].astype(BF16),
        ffn_wu=p["ffn_w_in"][:, :, D_FF:].astype(BF16), ffn_wo=p["ffn_w_out"].astype(BF16),
        ffn_norm_post=row(p["ffn_norm_post"]),
    )


FLASH_TQ = 512
FLASH_TK = 512
BAND_TQ = LANES


def _per_row(vals_hg, tq, lanes=1):
    Hk, G = vals_hg.shape
    return jnp.broadcast_to(vals_hg[:, :, None, None], (Hk, G, tq, lanes)).reshape(Hk, G * tq, lanes)


def _layer(x, w, c):
    B, S, D = x.shape
    M = B * S
    Hk, G, Dh = NSA_KV_HEADS, NSA_GROUP, HEAD_DIM

    z_mla, z_small, z_cmp, *heads = _inproj(x, w["mix_norm_pre"], w["w_all"])
    hm = dict(zip(MAIN_ORDER, heads))
    grouped = lambda t: t.reshape(B, Hk, G, S, t.shape[-1])
    single = lambda t: t[:, :, None]
    flat = lambda o: o.reshape(B, -1, S, o.shape[-1])

    ng = z_small[:, :, 64:88].reshape(B, S, Hk, G, 3).transpose(4, 0, 2, 3, 1)[..., None]
    nq = grouped(hm["nq"])
    rows = z_cmp.reshape(B, 2 * Hk, S // NSA_CMP_STRIDE, NSA_CMP_STRIDE * Dh)
    kvc = _nsa_compress(rows, w["cmp_pe"], w["cmp_w"])
    oc, selt = _nsa_cmp(nq, kvc, c["pos_col"], c["pc_row"], c["nsa_slopes_cmp"], ng[0], c["overlap_t"])
    selb = selt.transpose(0, 1, 3, 2)
    o_sel = _flash(
        AttnCfg(tq=FLASH_TQ // G, tk=FLASH_TK, alibi=True, gate=True, add=True, selaug=True, out_dtype=F32),
        nq, hm["nks"], hm["nvs"], selb=selb, pos_col=c["pos_col"], pos_rows=c["pos_rows_flash"],
        slopes=c["nsa_slopes_flash"], gate=ng[1], add=oc)
    y_nsa = _band(
        AttnCfg(tq=BAND_TQ, window=NSA_WINDOW, alibi=True, gate=True, add=True),
        nq, hm["nkw"], hm["nvw"], pos_col=c["pos_col"], pos_rows=c["pos_rows_band"],
        slopes=c["nsa_slopes_band"], gate=ng[2], add=o_sel)

    y_swa = _band(
        AttnCfg(tq=BAND_TQ, window=SWA_WINDOW, alibi=True, sink=True),
        grouped(hm["sq"]), hm["sk"], hm["sv"], pos_col=c["pos_col"],
        pos_rows=c["pos_rows_band"], slopes=c["swa_slopes_band"],
        sinks=_per_row(w["swa_sinks"].reshape(SWA_KV_HEADS, SWA_GROUP), BAND_TQ))

    q_mla, k_mla, v_mla = _mla_prep(z_mla, z_small, c["cos"], c["sin"], w["mla_q_norm"], w["mla_kv_norm"],
                                    w["wq_all"], w["wkv_all"])
    y_mla = _flash(AttnCfg(tq=FLASH_TQ, tk=FLASH_TK), single(q_mla), k_mla, v_mla)

    f_logit = z_small[:, :, 88:96].transpose(0, 2, 1)
    cum = _fox_cumsum(f_logit, w["fox_f_bias"])
    y_fox = _flash(
        AttnCfg(tq=FLASH_TQ, tk=FLASH_TK, decay=True), single(hm["fq"]), hm["fk"], hm["fv"],
        cum_col=cum[:, :, None, :, None], cum_rows=cum.reshape(B, FOX_HEADS, S // FLASH_TK, 1, FLASH_TK))

    x1 = _merge(x, [flat(y_nsa), flat(y_swa), flat(y_mla), flat(y_fox)], w["mix_norm_pre"], w["w_gate"],
                w["w_branch"], w["w_out"], w["mix_norm_post"])

    kv = _mem_kv(c["mem"], w["xa_mem_norm"], w["xa_wkv"])
    kv = kv.reshape(B, -1, 2 * XA_HEADS, Dh).transpose(0, 2, 1, 3)
    x2 = _xattn(x1, kv, w["xa_norm_pre"], w["xa_wq"], w["xa_wo"], w["xa_norm_post"])

    x3 = _ffn(x2.reshape(M, D), w["ffn_norm_pre"], w["ffn_wg"], w["ffn_wu"], w["ffn_wo"], w["ffn_norm_post"])
    return x3.reshape(B, S, D)


def _constants(mem, positions, S):
    B = positions.shape[0]
    pos = positions.astype(F32)
    half = MLA_ROPE_DIM // 2
    inv = (ROPE_THETA ** (-np.arange(half, dtype=np.float32) / half)).astype(np.float32)
    inv_row = jnp.asarray(np.tile(inv, 2 * MLA_HEADS)[None, :])
    cos, sin = _rope_tables(pos.reshape(B * S, 1), inv_row)
    nc = S // NSA_CMP_STRIDE
    n_valid = (S - NSA_CMP_LEN) // NSA_CMP_STRIDE + 1
    assert n_valid == nc - 1
    c_end = np.minimum(np.arange(nc) * NSA_CMP_STRIDE + NSA_CMP_LEN - 1, S - 1)
    n_sel = S // NSA_SEL_LEN
    c_start = np.arange(nc) * NSA_CMP_STRIDE
    sel_start = np.arange(n_sel) * NSA_SEL_LEN
    overlap = ((c_start[:, None] < sel_start[None, :] + NSA_SEL_LEN)
               & (c_start[:, None] + NSA_CMP_LEN - 1 >= sel_start[None, :])
               & (np.arange(nc)[:, None] < n_valid)).astype(np.float32)
    nsa_slopes = jnp.asarray(_alibi_slopes(NSA_HEADS).reshape(NSA_KV_HEADS, NSA_GROUP))
    swa_slopes = jnp.asarray(_alibi_slopes(SWA_HEADS).reshape(SWA_KV_HEADS, SWA_GROUP))
    return dict(
        mem=mem, cos=cos.reshape(B, S, -1), sin=sin.reshape(B, S, -1),
        pos_col=pos[:, :, None],
        pos_rows_flash=pos.reshape(B, S // FLASH_TK, 1, FLASH_TK),
        pos_rows_band=pos.reshape(B, S // LANES, 1, LANES),
        pc_row=pos[:, c_end][:, None, :],
        overlap_t=jnp.asarray(overlap.T),
        nsa_slopes_cmp=_per_row(nsa_slopes, NSA_CMP_TQ),
        nsa_slopes_flash=_per_row(nsa_slopes, FLASH_TQ // NSA_GROUP, LANES),
        nsa_slopes_band=_per_row(nsa_slopes, BAND_TQ),
        swa_slopes_band=_per_row(swa_slopes, BAND_TQ),
    )


def kernel(x, mem, positions, mix_norm_pre, w_in, nsa_cmp_w, nsa_cmp_pe, swa_sinks, mla_q_norm, mla_w_q_up,
           mla_kv_norm, mla_w_kv_up, fox_f_bias, w_branch, w_gate, w_out, mix_norm_post, xa_norm_pre,
           xa_mem_norm, xa_w_q, xa_w_kv, xa_w_o, xa_norm_post, ffn_norm_pre, ffn_w_in, ffn_w_out,
           ffn_norm_post):
    params = dict(
        mix_norm_pre=mix_norm_pre, w_in=w_in, nsa_cmp_w=nsa_cmp_w, nsa_cmp_pe=nsa_cmp_pe, swa_sinks=swa_sinks,
        mla_q_norm=mla_q_norm, mla_w_q_up=mla_w_q_up, mla_kv_norm=mla_kv_norm, mla_w_kv_up=mla_w_kv_up,
        fox_f_bias=fox_f_bias, w_branch=w_branch, w_gate=w_gate, w_out=w_out, mix_norm_post=mix_norm_post,
        xa_norm_pre=xa_norm_pre, xa_mem_norm=xa_mem_norm, xa_w_q=xa_w_q, xa_w_kv=xa_w_kv, xa_w_o=xa_w_o,
        xa_norm_post=xa_norm_post, ffn_norm_pre=ffn_norm_pre, ffn_w_in=ffn_w_in, ffn_w_out=ffn_w_out,
        ffn_norm_post=ffn_norm_post)
    S = x.shape[1]
    weights = _prep_weights(params)
    consts = _constants(mem, positions, S)

    def body(xc, w):
        return _layer(xc, w, consts), None

    out, _ = lax.scan(body, x, weights)
    return out
```

```python
import functools
from typing import NamedTuple

import numpy as np
import jax
import jax.numpy as jnp
from jax import lax
from jax.experimental import pallas as pl
from jax.experimental.pallas import tpu as pltpu

F32 = jnp.float32
BF16 = jnp.bfloat16

D_MODEL = 1024
HEAD_DIM = 64
NORM_EPS = 1e-6
NEG_INF = -1e30
FORCE_SCORE = 1e9
LANES = 128

NSA_HEADS = 8
NSA_KV_HEADS = 2
NSA_GROUP = NSA_HEADS // NSA_KV_HEADS
NSA_CMP_LEN = 32
NSA_CMP_STRIDE = 16
NSA_SEL_LEN = 64
NSA_TOP_N = 8
NSA_WINDOW = 512

SWA_HEADS = 8
SWA_KV_HEADS = 2
SWA_GROUP = SWA_HEADS // SWA_KV_HEADS
SWA_WINDOW = 128

MLA_HEADS = 8
MLA_Q_RANK = 256
MLA_KV_RANK = 128
MLA_NOPE_DIM = 64
MLA_ROPE_DIM = 32
MLA_V_DIM = 64
MLA_QK_DIM = MLA_NOPE_DIM + MLA_ROPE_DIM
ROPE_THETA = 10000.0

FOX_HEADS = 8
XA_HEADS = 4
N_BRANCH = 4
BRANCH_WIDTH = 512
D_FF = ((8 * D_MODEL + 3 * 256 - 1) // (3 * 256)) * 256

IN_SPLITS = (512, 128, 128, 128, 128, 128, 128, 24,
             512, 128, 128,
             MLA_Q_RANK, MLA_KV_RANK, MLA_ROPE_DIM,
             512, 512, 512, FOX_HEADS)
IN_NAMES = ("nq", "nkc", "nvc", "nks", "nvs", "nkw", "nvw", "ng",
            "sq", "sk", "sv", "mcq", "mckv", "mkr", "fq", "fk", "fv", "ff")

VMEM_LIMIT_BYTES = 56 * 1024 * 1024

MAIN_ORDER = ("nq", "nks", "nvs", "nkw", "nvw", "sq", "sk", "sv", "fq", "fk", "fv")
MAIN_COLS = 3328
SMALL_COLS = 128


def _cparams(sem):
    return pltpu.CompilerParams(dimension_semantics=sem, vmem_limit_bytes=VMEM_LIMIT_BYTES)


def _rms(x, g):
    return x * lax.rsqrt(jnp.mean(x * x, axis=-1, keepdims=True) + NORM_EPS) * g


def _dot(a, b):
    return jnp.dot(a, b, preferred_element_type=F32)


def _dot_nt(a, b, precision=None):
    return lax.dot_general(a, b, (((1,), (1,)), ((), ())), preferred_element_type=F32,
                           precision=precision)


def _rep(x, n):
    return x if n == 1 else jnp.concatenate([x] * n, axis=1)


F32_COLS = MLA_Q_RANK + MLA_KV_RANK + 4 * HEAD_DIM + SMALL_COLS
MAIN_HEADS = tuple(8 if n in ("nq", "sq", "fq", "fk", "fv") else 2 for n in MAIN_ORDER)
PROJ_CHUNK = 512
FLASH_V_NAMES = ("nvs", "fv")


def _ones_column(rows):
    return jnp.where(lax.broadcasted_iota(jnp.int32, (rows, HEAD_DIM), 1) == 0, 1.0, 0.0)


def _inproj_kernel(x_ref, g_ref, w_ref, zmla_ref, zsmall_ref, cmp_ref, *head_refs):
    tm = x_ref.shape[1]
    h = _rms(x_ref[0], g_ref[...]).astype(BF16)
    n_mla = MLA_Q_RANK + MLA_KV_RANK
    zmla_ref[0] = _dot(h, w_ref[:, :n_mla])
    r = _dot(h, w_ref[:, n_mla:F32_COLS])
    for f in range(4):
        cmp_ref[0, f] = r[:, f * HEAD_DIM:(f + 1) * HEAD_DIM]
    zsmall_ref[0] = r[:, 4 * HEAD_DIM:]

    n_sel = head_refs[MAIN_ORDER.index("nks")].shape[3] - HEAD_DIM
    t = pl.program_id(1) * tm + lax.broadcasted_iota(jnp.int32, (tm, n_sel), 0)
    blk = lax.shift_right_logical(t, NSA_SEL_LEN.bit_length() - 1)
    onehot = jnp.where(blk == lax.broadcasted_iota(jnp.int32, (tm, n_sel), 1), 1.0, 0.0)

    ones_col = _ones_column(tm)
    extra = {"nks": onehot, **{n: ones_col for n in FLASH_V_NAMES}}
    dest = [(ref, hd, extra.get(name)) for ref, nh, name in zip(head_refs, MAIN_HEADS, MAIN_ORDER)
            for hd in range(nh)]
    for c in range(0, MAIN_COLS, PROJ_CHUNK):
        w = min(PROJ_CHUNK, MAIN_COLS - c)
        res = _dot(h, w_ref[:, F32_COLS + c:F32_COLS + c + w])
        for k in range(w // HEAD_DIM):
            ref, hd, appended = dest[c // HEAD_DIM + k]
            piece = res[:, k * HEAD_DIM:(k + 1) * HEAD_DIM]
            if appended is not None:
                piece = jnp.concatenate([piece, appended], axis=-1)
            ref[0, hd] = piece.astype(ref.dtype)


def _inproj(x, g, w, tm=512):
    B, S, D = x.shape
    N = w.shape[1]
    n_sel = S // NSA_SEL_LEN
    assert N == F32_COLS + MAIN_COLS and S % tm == 0
    tok = lambda n: pl.BlockSpec((1, tm, n), lambda b, i: (b, i, 0))
    heads = lambda nh, d: pl.BlockSpec((1, nh, tm, d), lambda b, i: (b, 0, i, 0))
    head_dims = [HEAD_DIM + n_sel if n == "nks" else 2 * HEAD_DIM if n in FLASH_V_NAMES else HEAD_DIM
                 for n in MAIN_ORDER]
    return pl.pallas_call(
        _inproj_kernel,
        grid=(B, S // tm),
        in_specs=[pl.BlockSpec((1, tm, D), lambda b, i: (b, i, 0)),
                  pl.BlockSpec((1, D), lambda b, i: (0, 0)),
                  pl.BlockSpec((D, N), lambda b, i: (0, 0))],
        out_specs=[tok(MLA_Q_RANK + MLA_KV_RANK), tok(SMALL_COLS), heads(4, HEAD_DIM)]
                  + [heads(nh, d) for nh, d in zip(MAIN_HEADS, head_dims)],
        out_shape=[jax.ShapeDtypeStruct((B, S, MLA_Q_RANK + MLA_KV_RANK), F32),
                   jax.ShapeDtypeStruct((B, S, SMALL_COLS), F32),
                   jax.ShapeDtypeStruct((B, 4, S, HEAD_DIM), F32)]
                  + [jax.ShapeDtypeStruct((B, nh, S, d), BF16) for nh, d in zip(MAIN_HEADS, head_dims)],
        compiler_params=_cparams(("parallel", "parallel")),
        name="inproj",
    )(x, g, w)


def _rope_table_kernel(pos_ref, inv_ref, cos_ref, sin_ref):
    ang = pos_ref[...] * inv_ref[...]
    cos_ref[...] = jnp.cos(ang)
    sin_ref[...] = jnp.sin(ang)


def _rope_tables(pos_col, inv_row, tm=512):
    M = pos_col.shape[0]
    N = inv_row.shape[1]
    return pl.pallas_call(
        _rope_table_kernel,
        grid=(M // tm,),
        in_specs=[pl.BlockSpec((tm, 1), lambda i: (i, 0)),
                  pl.BlockSpec((1, N), lambda i: (0, 0))],
        out_specs=[pl.BlockSpec((tm, N), lambda i: (i, 0))] * 2,
        out_shape=[jax.ShapeDtypeStruct((M, N), F32)] * 2,
        compiler_params=_cparams(("parallel",)),
        name="rope_tables",
    )(pos_col, inv_row)


def _mla_prep_kernel(z_ref, zs_ref, cos_ref, sin_ref, qg_ref, kvg_ref, wq_ref, wkv_ref,
                     q_ref, k_ref, v_ref):
    nq = MLA_HEADS * MLA_NOPE_DIM
    nr = MLA_HEADS * MLA_ROPE_DIM
    scale = MLA_QK_DIM ** -0.5
    z = z_ref[0]
    cq = _rms(z[:, :MLA_Q_RANK], qg_ref[...]).astype(BF16)
    ckv = _rms(z[:, MLA_Q_RANK:], kvg_ref[...]).astype(BF16)
    cos = cos_ref[0]
    sin = sin_ref[0]
    q_nope = _dot(cq, wq_ref[:, :nq]) * scale
    q_rot = (_dot(cq, wq_ref[:, nq:nq + nr]) * cos + _dot(cq, wq_ref[:, nq + nr:]) * sin) * scale
    k_nope = _dot(ckv, wkv_ref[:, :nq])
    v = _dot(ckv, wkv_ref[:, nq:])
    zs = zs_ref[0]
    r, d = MLA_ROPE_DIM, MLA_NOPE_DIM
    k_rot = zs[:, :r] * cos[:, :r] + zs[:, r:2 * r] * sin[:, :r]
    ones_col = _ones_column(z.shape[0])
    for h in range(MLA_HEADS):
        q_ref[0, h] = jnp.concatenate([q_nope[:, h * d:(h + 1) * d], q_rot[:, h * r:(h + 1) * r]],
                                      axis=-1).astype(BF16)
        k_ref[0, h] = jnp.concatenate([k_nope[:, h * d:(h + 1) * d], k_rot], axis=-1).astype(BF16)
        v_ref[0, h] = jnp.concatenate([v[:, h * MLA_V_DIM:(h + 1) * MLA_V_DIM], ones_col],
                                      axis=-1).astype(BF16)


def _mla_prep(z_mla, z_small, cos, sin, qg, kvg, wq, wkv, tm=512):
    B, S, _ = z_mla.shape
    nr = MLA_HEADS * MLA_ROPE_DIM
    tok = lambda n: pl.BlockSpec((1, tm, n), lambda b, i: (b, i, 0))
    heads = lambda d: pl.BlockSpec((1, MLA_HEADS, tm, d), lambda b, i: (b, 0, i, 0))
    full = lambda a: pl.BlockSpec(a.shape, lambda b, i: (0,) * a.ndim)
    return pl.pallas_call(
        _mla_prep_kernel,
        grid=(B, S // tm),
        in_specs=[tok(z_mla.shape[2]), tok(SMALL_COLS), tok(nr), tok(nr),
                  full(qg), full(kvg), full(wq), full(wkv)],
        out_specs=[heads(MLA_QK_DIM), heads(MLA_QK_DIM), heads(2 * MLA_V_DIM)],
        out_shape=[jax.ShapeDtypeStruct((B, MLA_HEADS, S, d), BF16)
                   for d in (MLA_QK_DIM, MLA_QK_DIM, 2 * MLA_V_DIM)],
        compiler_params=_cparams(("parallel", "parallel")),
        name="mla_prep",
    )(z_mla, z_small, cos, sin, qg, kvg, wq, wkv)


CUM_CHUNK = 512


def _fox_cumsum_kernel(f_ref, b_ref, u_ref, o_ref):
    x = f_ref[0] + b_ref[...]
    lf = jnp.minimum(x, 0.0) - jnp.log1p(jnp.exp(-jnp.abs(x)))
    S = lf.shape[1]
    carry = jnp.zeros((lf.shape[0], 1), F32)
    for c in range(0, S, CUM_CHUNK):
        loc = jnp.dot(lf[:, c:c + CUM_CHUNK], u_ref[...], preferred_element_type=F32,
                      precision=lax.Precision.HIGHEST) + carry
        o_ref[0, :, c:c + CUM_CHUNK] = loc
        carry = loc[:, CUM_CHUNK - 1:CUM_CHUNK]


def _fox_cumsum(f_bhs, bias_col):
    B, H, S = f_bhs.shape
    assert S % CUM_CHUNK == 0
    upper = jnp.asarray(np.triu(np.ones((CUM_CHUNK, CUM_CHUNK), np.float32)))
    return pl.pallas_call(
        _fox_cumsum_kernel,
        grid=(B,),
        in_specs=[pl.BlockSpec((1, H, S), lambda b: (b, 0, 0)),
                  pl.BlockSpec((H, 1), lambda b: (0, 0)),
                  pl.BlockSpec((CUM_CHUNK, CUM_CHUNK), lambda b: (0, 0))],
        out_specs=pl.BlockSpec((1, H, S), lambda b: (b, 0, 0)),
        out_shape=jax.ShapeDtypeStruct((B, H, S), F32),
        compiler_params=_cparams(("parallel",)),
        name="fox_cumsum",
    )(f_bhs, bias_col, upper)


def _nsa_compress_kernel(r_ref, pe_ref, w_ref, o_ref):
    r = r_ref[0, 0]
    first = _dot((r + pe_ref[0, 0:1, :]).astype(BF16), w_ref[0, 0])
    second = _dot((r + pe_ref[0, 1:2, :]).astype(BF16), w_ref[0, 1])
    nc = r.shape[0]
    nxt = pltpu.roll(second, nc - 1, 0)
    row = lax.broadcasted_iota(jnp.int32, (nc, 1), 0)
    o_ref[0, 0] = jnp.where(row < nc - 1, first + nxt, 0.0).astype(o_ref.dtype)


def _nsa_compress(rows, pe, w):
    B, F, NC, K = rows.shape
    return pl.pallas_call(
        _nsa_compress_kernel,
        grid=(B, F),
        in_specs=[pl.BlockSpec((1, 1, NC, K), lambda b, f: (b, f, 0, 0)),
                  pl.BlockSpec((1, 2, K), lambda b, f: (f // NSA_KV_HEADS, 0, 0)),
                  pl.BlockSpec((1, 2, K, HEAD_DIM), lambda b, f: (f // NSA_KV_HEADS, 0, 0, 0))],
        out_specs=pl.BlockSpec((1, 1, NC, HEAD_DIM), lambda b, f: (b, f, 0, 0)),
        out_shape=jax.ShapeDtypeStruct((B, F, NC, HEAD_DIM), BF16),
        compiler_params=_cparams(("parallel", "parallel")),
        name="nsa_compress",
    )(rows, pe, w)


NSA_CMP_TQ = 256


def _nsa_cmp_kernel(q_ref, kc_ref, vc_ref, pq_ref, pc_ref, sl_ref, gate_ref, ovt_ref,
                    oc_ref, selt_ref, any_ref):
    G, tq, Dh = q_ref.shape[2:]
    R = G * tq
    i = pl.program_id(2)
    q = q_ref[0, 0].reshape(R, Dh)
    kc = kc_ref[0, 0]
    nc = kc.shape[0]
    n_sel = ovt_ref.shape[0]
    s = _dot_nt(q, kc)
    pq = jnp.concatenate([pq_ref[0]] * G, axis=0)
    s = s - sl_ref[0] * jnp.abs(pq - pc_ref[0])
    t_row = (i * tq + lax.broadcasted_iota(jnp.int32, (G, tq, 1), 1)).reshape(R, 1)
    c_col = lax.broadcasted_iota(jnp.int32, (1, nc), 1)
    c_end = jnp.where(c_col < nc - 1, c_col * NSA_CMP_STRIDE + (NSA_CMP_LEN - 1), jnp.int32(2 ** 30))
    s = jnp.where(c_end <= t_row, s, NEG_INF)
    m = jnp.max(s, axis=-1, keepdims=True)
    e = jnp.exp(s - m)
    l = jnp.sum(e, axis=-1, keepdims=True)
    p = e * jnp.where(m > 0.5 * NEG_INF, 1.0 / l, 0.0)
    o = _dot(p.astype(BF16), vc_ref[0, 0])
    gate = jax.nn.sigmoid(gate_ref[0, 0].reshape(R, 1))
    oc_ref[0, 0] = (gate * o).reshape(G, tq, Dh)
    p_sum = jnp.sum(p.reshape(G, tq, nc), axis=0)
    imp = _dot_nt(ovt_ref[...], p_sum, precision=lax.Precision.HIGHEST)
    t_q = i * tq + lax.broadcasted_iota(jnp.int32, (1, tq), 1)
    jt = lax.shift_right_logical(t_q, NSA_SEL_LEN.bit_length() - 1)
    j = lax.broadcasted_iota(jnp.int32, (n_sel, 1), 0)
    forced = (j == 0) | (j == jt) | (j == jt - 1)
    causal = j <= jt
    val = jnp.where(forced, FORCE_SCORE, imp)
    val = jnp.where(causal, val, NEG_INF)
    sel = jnp.zeros((n_sel, tq), jnp.int32)
    for _ in range(min(NSA_TOP_N, n_sel)):
        best = jnp.max(val, axis=0, keepdims=True)
        idx = jnp.min(jnp.where(val == best, j, n_sel), axis=0, keepdims=True)
        pick = j == idx
        sel = jnp.where(pick, 1, sel)
        val = jnp.where(pick, -jnp.inf, val)
    keep = (sel > 0) & causal
    selt_ref[0, 0] = jnp.where(keep, 0.0, NEG_INF).astype(selt_ref.dtype)
    per_tile = any_ref.shape[2]
    any_ref[0, 0] = jnp.max(jnp.where(keep, 1.0, 0.0).reshape(per_tile, n_sel // per_tile, tq), axis=1)


def _nsa_cmp(q, kvc, pos_col, pc_row, slopes, gate, overlap_t, n_tiles):
    B, Hk, G, S, Dh = q.shape
    NC = kvc.shape[2]
    n_sel = overlap_t.shape[0]
    assert NSA_SEL_LEN & (NSA_SEL_LEN - 1) == 0 and n_sel % n_tiles == 0
    tq = NSA_CMP_TQ
    R = G * tq
    return pl.pallas_call(
        _nsa_cmp_kernel,
        grid=(B, Hk, S // tq),
        in_specs=[pl.BlockSpec((1, 1, G, tq, Dh), lambda b, h, i: (b, h, 0, i, 0)),
                  pl.BlockSpec((1, 1, NC, Dh), lambda b, h, i: (b, h, 0, 0)),
                  pl.BlockSpec((1, 1, NC, Dh), lambda b, h, i: (b, h + Hk, 0, 0)),
                  pl.BlockSpec((1, tq, 1), lambda b, h, i: (b, i, 0)),
                  pl.BlockSpec((1, 1, NC), lambda b, h, i: (b, 0, 0)),
                  pl.BlockSpec((1, R, 1), lambda b, h, i: (h, 0, 0)),
                  pl.BlockSpec((1, 1, G, tq, 1), lambda b, h, i: (b, h, 0, i, 0)),
                  pl.BlockSpec((n_sel, NC), lambda b, h, i: (0, 0))],
        out_specs=[pl.BlockSpec((1, 1, G, tq, Dh), lambda b, h, i: (b, h, 0, i, 0)),
                   pl.BlockSpec((1, 1, n_sel, tq), lambda b, h, i: (b, h, 0, i)),
                   pl.BlockSpec((1, 1, n_tiles, tq), lambda b, h, i: (b, h, 0, i))],
        out_shape=[jax.ShapeDtypeStruct((B, Hk, G, S, Dh), F32),
                   jax.ShapeDtypeStruct((B, Hk, n_sel, S), BF16),
                   jax.ShapeDtypeStruct((B, Hk, n_tiles, S), F32)],
        compiler_params=_cparams(("parallel", "parallel", "parallel")),
        name="nsa_cmp",
    )(q, kvc, kvc, pos_col, pc_row, slopes, gate, overlap_t)


class AttnCfg(NamedTuple):
    tq: int
    tk: int = 0
    window: int = 0
    alibi: bool = False
    decay: bool = False
    sink: bool = False
    gate: bool = False
    add: bool = False
    selaug: bool = False
    sparse: bool = False
    out_dtype: object = BF16


def _epilogue(cfg, out, gate_ref, add_ref, o_ref, G, tq):
    R, Dv = out.shape
    if cfg.gate:
        out = out * jax.nn.sigmoid(gate_ref[0, 0].reshape(R, 1))
    if cfg.add:
        out = out + add_ref[0, 0].reshape(R, Dv)
    o_ref[0, 0] = out.reshape(G, tq, Dv).astype(o_ref.dtype)


def _flash_kernel(*refs, cfg):
    it = iter(refs)
    cnt_ref = tiles_ref = None
    if cfg.sparse:
        cnt_ref, tiles_ref = next(it), next(it)
    q_ref, k_ref, v_ref = next(it), next(it), next(it)
    pq_ref = pk_ref = sl_ref = cq_ref = ck_ref = gate_ref = add_ref = None
    selb_ref = next(it) if cfg.selaug else None
    if cfg.alibi:
        pq_ref, pk_ref, sl_ref = next(it), next(it), next(it)
    if cfg.decay:
        cq_ref, ck_ref = next(it), next(it)
    if cfg.gate:
        gate_ref = next(it)
    if cfg.add:
        add_ref = next(it)
    o_ref = next(it)
    m_s, acc_s, sa_s, sb_s = (next(it) for _ in range(4))
    pq_s = next(it) if cfg.alibi else None
    cq_s = next(it) if cfg.decay else None

    G, tq, Dq = q_ref.shape[2:]
    Da = v_ref.shape[3]
    Dv = Da // 2
    tk = cfg.tk
    nrep = tk // LANES
    R = G * tq
    i = pl.program_id(2)
    q = q_ref[0, 0].reshape(R, Dq)
    if cfg.selaug:
        bias = jnp.concatenate([selb_ref[0, 0].astype(F32)] * G, axis=0)
        q = jnp.concatenate([q.astype(F32), bias], axis=-1).astype(BF16)

    m_s[...] = jnp.full((R, LANES), NEG_INF, F32)
    acc_s[...] = jnp.zeros((R, Da), F32)
    if cfg.alibi:
        pq_s[...] = jnp.broadcast_to(jnp.concatenate([pq_ref[0]] * G, axis=0), (R, LANES))
    if cfg.decay:
        cq_s[...] = jnp.broadcast_to(cq_ref[0, 0].reshape(R, 1), (R, LANES))

    def scores(j, s_ref):
        k0 = pl.multiple_of(j * tk, tk)
        s = _dot_nt(q, k_ref[0, 0, pl.ds(k0, tk), :])
        if cfg.alibi:
            s = s - _rep(sl_ref[0], nrep) * jnp.abs(_rep(pq_s[...], nrep) - pk_ref[0, j])
        if cfg.decay:
            s = s + (_rep(cq_s[...], nrep) - ck_ref[0, 0, j])
        s_ref[...] = s

    def update(j, s_ref, masked):
        s = s_ref[...]
        if masked:
            t_row = (i * tq + lax.broadcasted_iota(jnp.int32, (G, tq, tk), 1)).reshape(R, tk)
            col = j * tk + lax.broadcasted_iota(jnp.int32, (R, tk), 1)
            s = jnp.where(col <= t_row, s, NEG_INF)
        m_prev = m_s[...]
        m_new = jnp.maximum(m_prev, jnp.max(s, axis=-1, keepdims=True))
        alpha = jnp.exp(m_prev - m_new)
        p = jnp.exp((s - _rep(m_new, nrep)).astype(BF16))
        k0 = pl.multiple_of(j * tk, tk)
        acc_s[...] = _rep(alpha, Da // LANES) * acc_s[...] + _dot(p, v_ref[0, 0, pl.ds(k0, tk), :])
        m_s[...] = m_new

    diag = ((i + 1) * tq - 1) // tk
    if cfg.sparse:
        step_id = (pl.program_id(0) * pl.num_programs(1) + pl.program_id(1)) * pl.num_programs(2) + i
        n_vis = cnt_ref[step_id]
        nk = k_ref.shape[2] // tk

        def tile_at(pos):
            return jnp.where(pos < n_vis, tiles_ref[step_id * nk + jnp.minimum(pos, nk - 1)], diag)
    else:
        n_vis = diag

        def tile_at(pos):
            return pos

    scores(tile_at(0), sa_s)

    def pair(pp, c):
        pos = 2 * pp
        scores(tile_at(pos + 1), sb_s)
        update(tile_at(pos), sa_s, False)
        scores(tile_at(pos + 2), sa_s)
        update(tile_at(pos + 1), sb_s, False)
        return c
    lax.fori_loop(0, n_vis // 2, pair, 0)

    @pl.when(n_vis % 2 == 1)
    def _():
        scores(diag, sb_s)
        update(tile_at(n_vis - 1), sa_s, False)
        update(diag, sb_s, True)

    @pl.when(n_vis % 2 == 0)
    def _():
        update(diag, sa_s, True)

    acc = acc_s[...]
    out = acc[:, :Dv] / acc[:, Dv:Dv + 1]
    _epilogue(cfg, out, gate_ref, add_ref, o_ref, G, tq)


def _flash(cfg, q, k, v, *, visit=None, selb=None, pos_col=None, pos_rows=None, slopes=None,
           cum_col=None, cum_rows=None, gate=None, add=None):
    B, Hk, G, S, Dq = q.shape
    Dk, Da = k.shape[3], v.shape[3]
    Dv = Da // 2
    tq, tk = cfg.tq, cfg.tk
    assert S % tq == 0 and S % tk == 0 and tk % tq == 0 and tk % LANES == 0 and Da % LANES == 0
    assert Dk == Dq + (selb.shape[3] if cfg.selaug else 0) and cfg.sparse == (visit is not None)
    R = G * tq
    nk = S // tk
    qmap = lambda b, h, i, *_: (b, h, 0, i, 0)
    kvmap = lambda b, h, i, *_: (b, h, 0, 0)
    args = [q, k, v]
    specs = [pl.BlockSpec((1, 1, G, tq, Dq), qmap),
             pl.BlockSpec((1, 1, S, Dk), kvmap),
             pl.BlockSpec((1, 1, S, Da), kvmap)]
    scratch = [pltpu.VMEM((R, LANES), F32), pltpu.VMEM((R, Da), F32),
               pltpu.VMEM((R, tk), F32), pltpu.VMEM((R, tk), F32)]
    if cfg.selaug:
        args += [selb]
        specs += [pl.BlockSpec((1, 1, tq, Dk - Dq), lambda b, h, i, *_: (b, h, i, 0))]
    if cfg.alibi:
        args += [pos_col, pos_rows, slopes]
        specs += [pl.BlockSpec((1, tq, 1), lambda b, h, i, *_: (b, i, 0)),
                  pl.BlockSpec((1, nk, 1, tk), lambda b, h, i, *_: (b, 0, 0, 0)),
                  pl.BlockSpec((1, R, LANES), lambda b, h, i, *_: (h, 0, 0))]
        scratch.append(pltpu.VMEM((R, LANES), F32))
    if cfg.decay:
        args += [cum_col, cum_rows]
        specs += [pl.BlockSpec((1, 1, G, tq, 1), qmap),
                  pl.BlockSpec((1, 1, nk, 1, tk), lambda b, h, i, *_: (b, h, 0, 0, 0))]
        scratch.append(pltpu.VMEM((R, LANES), F32))
    if cfg.gate:
        args += [gate]
        specs += [pl.BlockSpec((1, 1, G, tq, 1), qmap)]
    if cfg.add:
        args += [add]
        specs += [pl.BlockSpec((1, 1, G, tq, Dv), qmap)]
    grid_spec = pltpu.PrefetchScalarGridSpec(
        num_scalar_prefetch=2 if cfg.sparse else 0,
        grid=(B, Hk, S // tq),
        in_specs=specs,
        out_specs=pl.BlockSpec((1, 1, G, tq, Dv), qmap),
        scratch_shapes=scratch)
    return pl.pallas_call(
        functools.partial(_flash_kernel, cfg=cfg),
        grid_spec=grid_spec,
        out_shape=jax.ShapeDtypeStruct((B, Hk, G, S, Dv), cfg.out_dtype),
        compiler_params=_cparams(("parallel", "parallel", "parallel")),
        name="flash",
    )(*(list(visit) if cfg.sparse else []), *args)


def _band_kernel(*refs, cfg):
    it = iter(refs)
    q_ref, k_ref, v_ref, pq_ref, pk_ref, sl_ref = (next(it) for _ in range(6))
    sink_ref = next(it) if cfg.sink else None
    gate_ref = next(it) if cfg.gate else None
    add_ref = next(it) if cfg.add else None
    o_ref = next(it)

    G, tq, Dq = q_ref.shape[2:]
    R = G * tq
    W = cfg.window
    span = W + tq
    i = pl.program_id(2)
    start = pl.multiple_of(jnp.maximum(i * tq - W, 0), LANES)
    blk0 = lax.shift_right_logical(start, LANES.bit_length() - 1)
    q = q_ref[0, 0].reshape(R, Dq)
    s = _dot_nt(q, k_ref[0, 0, pl.ds(start, span), :])
    pq = jnp.concatenate([pq_ref[0]] * G, axis=0)
    slope = sl_ref[0]
    d = (lax.broadcasted_iota(jnp.int32, (G, tq, LANES), 1)
         - lax.broadcasted_iota(jnp.int32, (G, tq, LANES), 2)).reshape(R, LANES)
    off = i * tq - start
    chunks = []
    for c in range(span // LANES):
        sc = s[:, c * LANES:(c + 1) * LANES]
        sc = sc - slope * jnp.abs(pq - pk_ref[0, blk0 + c])
        lo = c * LANES - off
        sc = jnp.where(d >= lo, jnp.where(d < lo + W, sc, NEG_INF), NEG_INF)
        chunks.append(sc)
    s = jnp.concatenate(chunks, axis=1)
    m = jnp.max(s, axis=-1, keepdims=True)
    e = jnp.exp(s - m)
    l = jnp.sum(e, axis=-1, keepdims=True)
    acc = _dot(e.astype(BF16), v_ref[0, 0, pl.ds(start, span), :])
    if cfg.sink:
        sink = sink_ref[0]
        m_f = jnp.maximum(m, sink)
        w = jnp.exp(m - m_f)
        out = acc * (w / (l * w + jnp.exp(sink - m_f)))
    else:
        out = acc / l
    _epilogue(cfg, out, gate_ref, add_ref, o_ref, G, tq)


def _band(cfg, q, k, v, *, pos_col, pos_rows, slopes, sinks=None, gate=None, add=None):
    B, Hk, G, S, Dq = q.shape
    Dv = v.shape[3]
    tq = cfg.tq
    assert tq == LANES and cfg.window % LANES == 0 and S % tq == 0 and S >= cfg.window + tq
    R = G * tq
    qmap = lambda b, h, i: (b, h, 0, i, 0)
    kvmap = lambda b, h, i: (b, h, 0, 0)
    args = [q, k, v, pos_col, pos_rows, slopes]
    specs = [pl.BlockSpec((1, 1, G, tq, Dq), qmap),
             pl.BlockSpec((1, 1, S, Dq), kvmap),
             pl.BlockSpec((1, 1, S, Dv), kvmap),
             pl.BlockSpec((1, tq, 1), lambda b, h, i: (b, i, 0)),
             pl.BlockSpec((1, S // LANES, 1, LANES), lambda b, h, i: (b, 0, 0, 0)),
             pl.BlockSpec((1, R, 1), lambda b, h, i: (h, 0, 0))]
    if cfg.sink:
        args += [sinks]
        specs += [pl.BlockSpec((1, R, 1), lambda b, h, i: (h, 0, 0))]
    if cfg.gate:
        args += [gate]
        specs += [pl.BlockSpec((1, 1, G, tq, 1), qmap)]
    if cfg.add:
        args += [add]
        specs += [pl.BlockSpec((1, 1, G, tq, Dv), qmap)]
    return pl.pallas_call(
        functools.partial(_band_kernel, cfg=cfg),
        grid=(B, Hk, S // tq),
        in_specs=specs,
        out_specs=pl.BlockSpec((1, 1, G, tq, Dv), qmap),
        out_shape=jax.ShapeDtypeStruct((B, Hk, G, S, Dv), cfg.out_dtype),
        compiler_params=_cparams(("parallel", "parallel", "parallel")),
        name="band",
    )(*args)


def _merge_kernel(x_ref, y0_ref, y1_ref, y2_ref, y3_ref, gpre_ref, wg_ref, wb_ref, wo_ref, gpost_ref,
                  o_ref):
    x = x_ref[0]
    h = _rms(x, gpre_ref[...]).astype(BF16)
    merged = None
    for b, y_ref in enumerate((y0_ref, y1_ref, y2_ref, y3_ref)):
        gate = jax.nn.sigmoid(_dot(h, wg_ref[b]))
        y = jnp.concatenate([y_ref[0, hd] for hd in range(y_ref.shape[1])], axis=-1)
        br = _dot(y, wb_ref[b])
        merged = gate * br if merged is None else merged + gate * br
    y = _dot(merged.astype(BF16), wo_ref[...])
    o_ref[0] = x + _rms(y, gpost_ref[...])


def _merge(x, ys, gpre, wg, wb, wo, gpost, tm=512):
    B, S, D = x.shape
    full = lambda a: pl.BlockSpec(a.shape, lambda b, i: (0,) * a.ndim, pipeline_mode=pl.Buffered(1))
    yspec = lambda y: pl.BlockSpec((1, y.shape[1], tm, y.shape[3]), lambda b, i: (b, 0, i, 0))
    return pl.pallas_call(
        _merge_kernel,
        grid=(B, S // tm),
        in_specs=[pl.BlockSpec((1, tm, D), lambda b, i: (b, i, 0))] + [yspec(y) for y in ys]
                 + [full(gpre), full(wg), full(wb), full(wo), full(gpost)],
        out_specs=pl.BlockSpec((1, tm, D), lambda b, i: (b, i, 0)),
        out_shape=jax.ShapeDtypeStruct((B, S, D), F32),
        compiler_params=_cparams(("parallel", "parallel")),
        name="merge",
    )(x, *ys, gpre, wg, wb, wo, gpost)


def _mem_kv_kernel(mem_ref, g_ref, w_ref, o_ref):
    o_ref[0] = _dot(_rms(mem_ref[0], g_ref[...]).astype(BF16), w_ref[...]).astype(BF16)


def _mem_kv(mem, g, w):
    B, Mm, D = mem.shape
    N = w.shape[1]
    return pl.pallas_call(
        _mem_kv_kernel,
        grid=(B,),
        in_specs=[pl.BlockSpec((1, Mm, D), lambda b: (b, 0, 0)),
                  pl.BlockSpec((1, D), lambda b: (0, 0)),
                  pl.BlockSpec((D, N), lambda b: (0, 0))],
        out_specs=pl.BlockSpec((1, Mm, N), lambda b: (b, 0, 0)),
        out_shape=jax.ShapeDtypeStruct((B, Mm, N), BF16),
        compiler_params=_cparams(("parallel",)),
        name="mem_kv",
    )(mem, g, w)


def _xattn_kernel(x_ref, kv_ref, gpre_ref, wq_ref, wo_ref, gpost_ref, o_ref):
    x = x_ref[0]
    h = _rms(x, gpre_ref[...]).astype(BF16)
    q_all = _dot(h, wq_ref[...])
    outs = []
    for hd in range(XA_HEADS):
        q = q_all[:, hd * HEAD_DIM:(hd + 1) * HEAD_DIM].astype(BF16)
        s = _dot_nt(q, kv_ref[0, hd])
        m = jnp.max(s, axis=-1, keepdims=True)
        e = jnp.exp(s - m)
        p = e / jnp.sum(e, axis=-1, keepdims=True)
        outs.append(_dot(p.astype(BF16), kv_ref[0, XA_HEADS + hd]))
    y = _dot(jnp.concatenate(outs, axis=-1).astype(BF16), wo_ref[...])
    o_ref[0] = x + _rms(y, gpost_ref[...])


def _xattn(x, kv, gpre, wq, wo, gpost, tm=512):
    B, S, D = x.shape
    full = lambda a: pl.BlockSpec(a.shape, lambda b, i: (0,) * a.ndim)
    return pl.pallas_call(
        _xattn_kernel,
        grid=(B, S // tm),
        in_specs=[pl.BlockSpec((1, tm, D), lambda b, i: (b, i, 0)),
                  pl.BlockSpec((1,) + kv.shape[1:], lambda b, i: (b, 0, 0, 0)),
                  full(gpre), full(wq), full(wo), full(gpost)],
        out_specs=pl.BlockSpec((1, tm, D), lambda b, i: (b, i, 0)),
        out_shape=jax.ShapeDtypeStruct((B, S, D), F32),
        compiler_params=_cparams(("parallel", "parallel")),
        name="xattn",
    )(x, kv, gpre, wq, wo, gpost)


FFN_CHUNK = 256


def _ffn_kernel(x_ref, gpre_ref, wg_ref, wu_ref, wo_ref, gpost_ref, o_ref):
    x = x_ref[...]
    h = _rms(x, gpre_ref[...]).astype(BF16)
    y = None
    for c in range(0, D_FF, FFN_CHUNK):
        g = _dot(h, wg_ref[:, c:c + FFN_CHUNK])
        u = _dot(h, wu_ref[:, c:c + FFN_CHUNK])
        a = (g * jax.nn.sigmoid(g) * u).astype(BF16)
        part = _dot(a, wo_ref[c:c + FFN_CHUNK, :])
        y = part if y is None else y + part
    o_ref[...] = x + _rms(y, gpost_ref[...])


def _ffn(x2d, gpre, wg, wu, wo, gpost, tm=512):
    M, D = x2d.shape
    full = lambda a: pl.BlockSpec(a.shape, lambda i: (0,) * a.ndim, pipeline_mode=pl.Buffered(1))
    return pl.pallas_call(
        _ffn_kernel,
        grid=(M // tm,),
        in_specs=[pl.BlockSpec((tm, D), lambda i: (i, 0)),
                  full(gpre), full(wg), full(wu), full(wo), full(gpost)],
        out_specs=pl.BlockSpec((tm, D), lambda i: (i, 0)),
        out_shape=jax.ShapeDtypeStruct((M, D), F32),
        compiler_params=_cparams(("parallel",)),
        name="ffn",
    )(x2d, gpre, wg, wu, wo, gpost)


def _alibi_slopes(n):
    return (2.0 ** (-8.0 * np.arange(1, n + 1) / n)).astype(np.float32)


def _rotate_half_cols(w):
    half = w.shape[-1] // 2
    return jnp.concatenate([-w[..., half:], w[..., :half]], axis=-1)


def _prep_weights(p):
    L = p["w_in"].shape[0]
    offs = np.concatenate([[0], np.cumsum(IN_SPLITS)])
    col = {n: p["w_in"][:, :, offs[k]:offs[k + 1]] for k, n in enumerate(IN_NAMES)}
    qk_scale = HEAD_DIM ** -0.5
    for n in ("nq", "sq", "fq"):
        col[n] = col[n] * qk_scale
    pad = jnp.zeros((L, D_MODEL, SMALL_COLS - 2 * MLA_ROPE_DIM - 24 - FOX_HEADS), F32)
    small = jnp.concatenate([col["mkr"], _rotate_half_cols(col["mkr"]), col["ng"], col["ff"], pad], -1)
    w_all = jnp.concatenate([col["mcq"], col["mckv"], col["nkc"], col["nvc"], small]
                            + [col[n] for n in MAIN_ORDER], axis=-1).astype(BF16)

    wq = p["mla_w_q_up"].reshape(L, MLA_Q_RANK, MLA_HEADS, MLA_QK_DIM)
    wq_nope = wq[..., :MLA_NOPE_DIM].reshape(L, MLA_Q_RANK, -1)
    wq_rot = wq[..., MLA_NOPE_DIM:]
    wq_all = jnp.concatenate([wq_nope, wq_rot.reshape(L, MLA_Q_RANK, -1),
                              _rotate_half_cols(wq_rot).reshape(L, MLA_Q_RANK, -1)], -1).astype(BF16)
    wkv = p["mla_w_kv_up"].reshape(L, MLA_KV_RANK, MLA_HEADS, MLA_NOPE_DIM + MLA_V_DIM)
    wkv_all = jnp.concatenate([wkv[..., :MLA_NOPE_DIM].reshape(L, MLA_KV_RANK, -1),
                               wkv[..., MLA_NOPE_DIM:].reshape(L, MLA_KV_RANK, -1)], -1).astype(BF16)

    half = NSA_CMP_LEN // 2
    cmp_w = p["nsa_cmp_w"].reshape(L, 2, 2, half * HEAD_DIM, HEAD_DIM).astype(BF16)
    cmp_pe = p["nsa_cmp_pe"].reshape(L, 2, 2, half * HEAD_DIM)

    xa_wq = (p["xa_w_q"] * qk_scale).astype(BF16)
    xa_wo = p["xa_w_o"].astype(BF16)

    row = lambda a: a[:, None, :]
    return dict(
        w_all=w_all, mix_norm_pre=row(p["mix_norm_pre"]),
        wq_all=wq_all, wkv_all=wkv_all, mla_q_norm=row(p["mla_q_norm"]), mla_kv_norm=row(p["mla_kv_norm"]),
        cmp_w=cmp_w, cmp_pe=cmp_pe, swa_sinks=p["swa_sinks"], fox_f_bias=p["fox_f_bias"][:, :, None],
        w_gate=p["w_gate"].astype(BF16), w_branch=p["w_branch"].astype(BF16), w_out=p["w_out"].astype(BF16),
        mix_norm_post=row(p["mix_norm_post"]),
        xa_norm_pre=row(p["xa_norm_pre"]), xa_mem_norm=row(p["xa_mem_norm"]), xa_wq=xa_wq,
        xa_wkv=p["xa_w_kv"].astype(BF16), xa_wo=xa_wo, xa_norm_post=row(p["xa_norm_post"]),
        ffn_norm_pre=row(p["ffn_norm_pre"]), ffn_wg=p["ffn_w_in"][:, :, :D_FF].astype(BF16),
        ffn_wu=p["ffn_w_in"][:, :, D_FF:].astype(BF16), ffn_wo=p["ffn_w_out"].astype(BF16),
        ffn_norm_post=row(p["ffn_norm_post"]),
    )


FLASH_TQ = 512
FLASH_TK = 512
BAND_TQ = LANES


def _per_row(vals_hg, tq, lanes=1):
    Hk, G = vals_hg.shape
    return jnp.broadcast_to(vals_hg[:, :, None, None], (Hk, G, tq, lanes)).reshape(Hk, G * tq, lanes)


def _visit_lists(needed, tq, tk):
    nq, nk = needed.shape[2:]
    diag = ((np.arange(nq)[:, None] + 1) * tq - 1) // tk
    needed = needed & jnp.asarray(np.arange(nk)[None, :] < diag)
    count = jnp.sum(needed, axis=-1).astype(jnp.int32)
    tiles = jnp.argsort(~needed, axis=-1, stable=True).astype(jnp.int32)
    return count.reshape(-1), tiles.reshape(-1)


ZERO_WEIGHT_LOGIT = -110.0


def _fox_tiles_needed(q, k, cum, tq, tk):
    B, H, S, _ = q.shape
    qn = jnp.sqrt(jnp.sum(jnp.square(q.astype(F32)), axis=-1))
    kmax = jnp.sqrt(jnp.max(jnp.sum(jnp.square(k.astype(F32)), axis=-1), axis=-1))
    row_bound = cum + 2.02 * qn * kmax[..., None]
    tile_bound = jnp.max(row_bound.reshape(B, H, S // tq, tq), axis=-1)
    cum_last = cum[:, :, tk - 1::tk]
    return tile_bound[..., None] - cum_last[:, :, None, :] > ZERO_WEIGHT_LOGIT


def _layer(x, w, c):
    B, S, D = x.shape
    M = B * S
    Hk, G, Dh = NSA_KV_HEADS, NSA_GROUP, HEAD_DIM

    z_mla, z_small, z_cmp, *heads = _inproj(x, w["mix_norm_pre"], w["w_all"])
    hm = dict(zip(MAIN_ORDER, heads))
    grouped = lambda t: t.reshape(B, Hk, G, S, t.shape[-1])
    single = lambda t: t[:, :, None]
    flat = lambda o: o.reshape(B, -1, S, o.shape[-1])

    ng = z_small[:, :, 64:88].reshape(B, S, Hk, G, 3).transpose(4, 0, 2, 3, 1)[..., None]
    nq = grouped(hm["nq"])
    rows = z_cmp.reshape(B, 2 * Hk, S // NSA_CMP_STRIDE, NSA_CMP_STRIDE * Dh)
    kvc = _nsa_compress(rows, w["cmp_pe"], w["cmp_w"])
    nk = S // FLASH_TK
    oc, selt, tile_any = _nsa_cmp(nq, kvc, c["pos_col"], c["pc_row"], c["nsa_slopes_cmp"], ng[0],
                                  c["overlap_t"], nk)
    selb = selt.transpose(0, 1, 3, 2)
    sel_tq = FLASH_TQ // G
    used = tile_any.reshape(B, Hk, nk, S // sel_tq, sel_tq).max(-1).transpose(0, 1, 3, 2) > 0.0
    o_sel = _flash(
        AttnCfg(tq=sel_tq, tk=FLASH_TK, alibi=True, gate=True, add=True, selaug=True, sparse=True,
                out_dtype=F32),
        nq, hm["nks"], hm["nvs"], visit=_visit_lists(used, sel_tq, FLASH_TK), selb=selb,
        pos_col=c["pos_col"], pos_rows=c["pos_rows_flash"], slopes=c["nsa_slopes_flash"], gate=ng[1], add=oc)
    y_nsa = _band(
        AttnCfg(tq=BAND_TQ, window=NSA_WINDOW, alibi=True, gate=True, add=True),
        nq, hm["nkw"], hm["nvw"], pos_col=c["pos_col"], pos_rows=c["pos_rows_band"],
        slopes=c["nsa_slopes_band"], gate=ng[2], add=o_sel)

    y_swa = _band(
        AttnCfg(tq=BAND_TQ, window=SWA_WINDOW, alibi=True, sink=True),
        grouped(hm["sq"]), hm["sk"], hm["sv"], pos_col=c["pos_col"],
        pos_rows=c["pos_rows_band"], slopes=c["swa_slopes_band"],
        sinks=_per_row(w["swa_sinks"].reshape(SWA_KV_HEADS, SWA_GROUP), BAND_TQ))

    q_mla, k_mla, v_mla = _mla_prep(z_mla, z_small, c["cos"], c["sin"], w["mla_q_norm"], w["mla_kv_norm"],
                                    w["wq_all"], w["wkv_all"])
    y_mla = _flash(AttnCfg(tq=FLASH_TQ, tk=FLASH_TK), single(q_mla), k_mla, v_mla)

    f_logit = z_small[:, :, 88:96].transpose(0, 2, 1)
    cum = _fox_cumsum(f_logit, w["fox_f_bias"])
    y_fox = _flash(
        AttnCfg(tq=FLASH_TQ, tk=FLASH_TK, decay=True, sparse=True), single(hm["fq"]), hm["fk"], hm["fv"],
        visit=_visit_lists(_fox_tiles_needed(hm["fq"], hm["fk"], cum, FLASH_TQ, FLASH_TK), FLASH_TQ, FLASH_TK),
        cum_col=cum[:, :, None, :, None], cum_rows=cum.reshape(B, FOX_HEADS, S // FLASH_TK, 1, FLASH_TK))

    x1 = _merge(x, [flat(y_nsa), flat(y_swa), flat(y_mla), flat(y_fox)], w["mix_norm_pre"], w["w_gate"],
                w["w_branch"], w["w_out"], w["mix_norm_post"])

    kv = _mem_kv(c["mem"], w["xa_mem_norm"], w["xa_wkv"])
    kv = kv.reshape(B, -1, 2 * XA_HEADS, Dh).transpose(0, 2, 1, 3)
    x2 = _xattn(x1, kv, w["xa_norm_pre"], w["xa_wq"], w["xa_wo"], w["xa_norm_post"])

    x3 = _ffn(x2.reshape(M, D), w["ffn_norm_pre"], w["ffn_wg"], w["ffn_wu"], w["ffn_wo"], w["ffn_norm_post"])
    return x3.reshape(B, S, D)


def _constants(mem, positions, S):
    B = positions.shape[0]
    pos = positions.astype(F32)
    half = MLA_ROPE_DIM // 2
    inv = (ROPE_THETA ** (-np.arange(half, dtype=np.float32) / half)).astype(np.float32)
    inv_row = jnp.asarray(np.tile(inv, 2 * MLA_HEADS)[None, :])
    cos, sin = _rope_tables(pos.reshape(B * S, 1), inv_row)
    nc = S // NSA_CMP_STRIDE
    n_valid = (S - NSA_CMP_LEN) // NSA_CMP_STRIDE + 1
    assert n_valid == nc - 1
    c_end = np.minimum(np.arange(nc) * NSA_CMP_STRIDE + NSA_CMP_LEN - 1, S - 1)
    n_sel = S // NSA_SEL_LEN
    c_start = np.arange(nc) * NSA_CMP_STRIDE
    sel_start = np.arange(n_sel) * NSA_SEL_LEN
    overlap = ((c_start[:, None] < sel_start[None, :] + NSA_SEL_LEN)
               & (c_start[:, None] + NSA_CMP_LEN - 1 >= sel_start[None, :])
               & (np.arange(nc)[:, None] < n_valid)).astype(np.float32)
    nsa_slopes = jnp.asarray(_alibi_slopes(NSA_HEADS).reshape(NSA_KV_HEADS, NSA_GROUP))
    swa_slopes = jnp.asarray(_alibi_slopes(SWA_HEADS).reshape(SWA_KV_HEADS, SWA_GROUP))
    return dict(
        mem=mem, cos=cos.reshape(B, S, -1), sin=sin.reshape(B, S, -1),
        pos_col=pos[:, :, None],
        pos_rows_flash=pos.reshape(B, S // FLASH_TK, 1, FLASH_TK),
        pos_rows_band=pos.reshape(B, S // LANES, 1, LANES),
        pc_row=pos[:, c_end][:, None, :],
        overlap_t=jnp.asarray(overlap.T),
        nsa_slopes_cmp=_per_row(nsa_slopes, NSA_CMP_TQ),
        nsa_slopes_flash=_per_row(nsa_slopes, FLASH_TQ // NSA_GROUP, LANES),
        nsa_slopes_band=_per_row(nsa_slopes, BAND_TQ),
        swa_slopes_band=_per_row(swa_slopes, BAND_TQ),
    )


def kernel(x, mem, positions, mix_norm_pre, w_in, nsa_cmp_w, nsa_cmp_pe, swa_sinks, mla_q_norm, mla_w_q_up,
           mla_kv_norm, mla_w_kv_up, fox_f_bias, w_branch, w_gate, w_out, mix_norm_post, xa_norm_pre,
           xa_mem_norm, xa_w_q, xa_w_kv, xa_w_o, xa_norm_post, ffn_norm_pre, ffn_w_in, ffn_w_out,
           ffn_norm_post):
    params = dict(
        mix_norm_pre=mix_norm_pre, w_in=w_in, nsa_cmp_w=nsa_cmp_w, nsa_cmp_pe=nsa_cmp_pe, swa_sinks=swa_sinks,
        mla_q_norm=mla_q_norm, mla_w_q_up=mla_w_q_up, mla_kv_norm=mla_kv_norm, mla_w_kv_up=mla_w_kv_up,
        fox_f_bias=fox_f_bias, w_branch=w_branch, w_gate=w_gate, w_out=w_out, mix_norm_post=mix_norm_post,
        xa_norm_pre=xa_norm_pre, xa_mem_norm=xa_mem_norm, xa_w_q=xa_w_q, xa_w_kv=xa_w_kv, xa_w_o=xa_w_o,
        xa_norm_post=xa_norm_post, ffn_norm_pre=ffn_norm_pre, ffn_w_in=ffn_w_in, ffn_w_out=ffn_w_out,
        ffn_norm_post=ffn_norm_post)
    S = x.shape[1]
    weights = _prep_weights(params)
    consts = _constants(mem, positions, S)

    def body(xc, w):
        return _layer(xc, w, consts), None

    out, _ = lax.scan(body, x, weights)
    return out
```

```python
import functools
from typing import NamedTuple

import numpy as np
import jax
import jax.numpy as jnp
from jax import lax
from jax.experimental import pallas as pl
from jax.experimental.pallas import tpu as pltpu

F32 = jnp.float32
BF16 = jnp.bfloat16

D_MODEL = 1024
HEAD_DIM = 64
NORM_EPS = 1e-6
NEG_INF = -1e30
FORCE_SCORE = 1e9
LANES = 128

NSA_HEADS = 8
NSA_KV_HEADS = 2
NSA_GROUP = NSA_HEADS // NSA_KV_HEADS
NSA_CMP_LEN = 32
NSA_CMP_STRIDE = 16
NSA_SEL_LEN = 64
NSA_TOP_N = 8
NSA_WINDOW = 512

SWA_HEADS = 8
SWA_KV_HEADS = 2
SWA_GROUP = SWA_HEADS // SWA_KV_HEADS
SWA_WINDOW = 128

MLA_HEADS = 8
MLA_Q_RANK = 256
MLA_KV_RANK = 128
MLA_NOPE_DIM = 64
MLA_ROPE_DIM = 32
MLA_V_DIM = 64
MLA_QK_DIM = MLA_NOPE_DIM + MLA_ROPE_DIM
ROPE_THETA = 10000.0

FOX_HEADS = 8
XA_HEADS = 4
N_BRANCH = 4
BRANCH_WIDTH = 512
D_FF = ((8 * D_MODEL + 3 * 256 - 1) // (3 * 256)) * 256

IN_SPLITS = (512, 128, 128, 128, 128, 128, 128, 24,
             512, 128, 128,
             MLA_Q_RANK, MLA_KV_RANK, MLA_ROPE_DIM,
             512, 512, 512, FOX_HEADS)
IN_NAMES = ("nq", "nkc", "nvc", "nks", "nvs", "nkw", "nvw", "ng",
            "sq", "sk", "sv", "mcq", "mckv", "mkr", "fq", "fk", "fv", "ff")

VMEM_LIMIT_BYTES = 56 * 1024 * 1024

MAIN_ORDER = ("nq", "nks", "nvs", "nkw", "nvw", "sq", "sk", "sv", "fq", "fk", "fv")
MAIN_COLS = 3328
SMALL_COLS = 128


def _cparams(sem):
    return pltpu.CompilerParams(dimension_semantics=sem, vmem_limit_bytes=VMEM_LIMIT_BYTES)


def _rms(x, g):
    return x * lax.rsqrt(jnp.mean(x * x, axis=-1, keepdims=True) + NORM_EPS) * g


def _dot(a, b):
    return jnp.dot(a, b, preferred_element_type=F32)


def _dot_nt(a, b, precision=None):
    return lax.dot_general(a, b, (((1,), (1,)), ((), ())), preferred_element_type=F32,
                           precision=precision)


def _rep(x, n):
    return x if n == 1 else jnp.concatenate([x] * n, axis=1)


F32_COLS = MLA_Q_RANK + MLA_KV_RANK + 4 * HEAD_DIM + SMALL_COLS
MAIN_HEADS = tuple(8 if n in ("nq", "sq", "fq", "fk", "fv") else 2 for n in MAIN_ORDER)
PROJ_CHUNK = 512
FLASH_V_NAMES = ("nvs", "fv")


def _ones_column(rows):
    return jnp.where(lax.broadcasted_iota(jnp.int32, (rows, HEAD_DIM), 1) == 0, 1.0, 0.0)


def _inproj_kernel(x_ref, g_ref, w_ref, zmla_ref, zsmall_ref, cmp_ref, *head_refs):
    tm = x_ref.shape[1]
    h = _rms(x_ref[0], g_ref[...]).astype(BF16)
    n_mla = MLA_Q_RANK + MLA_KV_RANK
    zmla_ref[0] = _dot(h, w_ref[:, :n_mla])
    r = _dot(h, w_ref[:, n_mla:F32_COLS])
    for f in range(4):
        cmp_ref[0, f] = r[:, f * HEAD_DIM:(f + 1) * HEAD_DIM]
    zsmall_ref[0] = r[:, 4 * HEAD_DIM:]

    n_sel = head_refs[MAIN_ORDER.index("nks")].shape[3] - HEAD_DIM
    t = pl.program_id(1) * tm + lax.broadcasted_iota(jnp.int32, (tm, n_sel), 0)
    blk = lax.shift_right_logical(t, NSA_SEL_LEN.bit_length() - 1)
    onehot = jnp.where(blk == lax.broadcasted_iota(jnp.int32, (tm, n_sel), 1), 1.0, 0.0)

    ones_col = _ones_column(tm)
    extra = {"nks": onehot, **{n: ones_col for n in FLASH_V_NAMES}}
    dest = [(ref, hd, extra.get(name)) for ref, nh, name in zip(head_refs, MAIN_HEADS, MAIN_ORDER)
            for hd in range(nh)]
    for c in range(0, MAIN_COLS, PROJ_CHUNK):
        w = min(PROJ_CHUNK, MAIN_COLS - c)
        res = _dot(h, w_ref[:, F32_COLS + c:F32_COLS + c + w])
        for k in range(w // HEAD_DIM):
            ref, hd, appended = dest[c // HEAD_DIM + k]
            piece = res[:, k * HEAD_DIM:(k + 1) * HEAD_DIM]
            if appended is not None:
                piece = jnp.concatenate([piece, appended], axis=-1)
            ref[0, hd] = piece.astype(ref.dtype)


def _inproj(x, g, w, tm=512):
    B, S, D = x.shape
    N = w.shape[1]
    n_sel = S // NSA_SEL_LEN
    assert N == F32_COLS + MAIN_COLS and S % tm == 0
    tok = lambda n: pl.BlockSpec((1, tm, n), lambda b, i: (b, i, 0))
    heads = lambda nh, d: pl.BlockSpec((1, nh, tm, d), lambda b, i: (b, 0, i, 0))
    head_dims = [HEAD_DIM + n_sel if n == "nks" else 2 * HEAD_DIM if n in FLASH_V_NAMES else HEAD_DIM
                 for n in MAIN_ORDER]
    return pl.pallas_call(
        _inproj_kernel,
        grid=(B, S // tm),
        in_specs=[pl.BlockSpec((1, tm, D), lambda b, i: (b, i, 0)),
                  pl.BlockSpec((1, D), lambda b, i: (0, 0)),
                  pl.BlockSpec((D, N), lambda b, i: (0, 0))],
        out_specs=[tok(MLA_Q_RANK + MLA_KV_RANK), tok(SMALL_COLS), heads(4, HEAD_DIM)]
                  + [heads(nh, d) for nh, d in zip(MAIN_HEADS, head_dims)],
        out_shape=[jax.ShapeDtypeStruct((B, S, MLA_Q_RANK + MLA_KV_RANK), F32),
                   jax.ShapeDtypeStruct((B, S, SMALL_COLS), F32),
                   jax.ShapeDtypeStruct((B, 4, S, HEAD_DIM), F32)]
                  + [jax.ShapeDtypeStruct((B, nh, S, d), BF16) for nh, d in zip(MAIN_HEADS, head_dims)],
        compiler_params=_cparams(("parallel", "parallel")),
        name="inproj",
    )(x, g, w)


def _rope_table_kernel(pos_ref, inv_ref, cos_ref, sin_ref):
    ang = pos_ref[...] * inv_ref[...]
    cos_ref[...] = jnp.cos(ang)
    sin_ref[...] = jnp.sin(ang)


def _rope_tables(pos_col, inv_row, tm=512):
    M = pos_col.shape[0]
    N = inv_row.shape[1]
    return pl.pallas_call(
        _rope_table_kernel,
        grid=(M // tm,),
        in_specs=[pl.BlockSpec((tm, 1), lambda i: (i, 0)),
                  pl.BlockSpec((1, N), lambda i: (0, 0))],
        out_specs=[pl.BlockSpec((tm, N), lambda i: (i, 0))] * 2,
        out_shape=[jax.ShapeDtypeStruct((M, N), F32)] * 2,
        compiler_params=_cparams(("parallel",)),
        name="rope_tables",
    )(pos_col, inv_row)


def _mla_prep_kernel(z_ref, zs_ref, cos_ref, sin_ref, qg_ref, kvg_ref, wq_ref, wkv_ref,
                     q_ref, k_ref, v_ref):
    nq = MLA_HEADS * MLA_NOPE_DIM
    nr = MLA_HEADS * MLA_ROPE_DIM
    scale = MLA_QK_DIM ** -0.5
    z = z_ref[0]
    cq = _rms(z[:, :MLA_Q_RANK], qg_ref[...]).astype(BF16)
    ckv = _rms(z[:, MLA_Q_RANK:], kvg_ref[...]).astype(BF16)
    cos = cos_ref[0]
    sin = sin_ref[0]
    q_nope = _dot(cq, wq_ref[:, :nq]) * scale
    q_rot = (_dot(cq, wq_ref[:, nq:nq + nr]) * cos + _dot(cq, wq_ref[:, nq + nr:]) * sin) * scale
    k_nope = _dot(ckv, wkv_ref[:, :nq])
    v = _dot(ckv, wkv_ref[:, nq:])
    zs = zs_ref[0]
    r, d = MLA_ROPE_DIM, MLA_NOPE_DIM
    k_rot = zs[:, :r] * cos[:, :r] + zs[:, r:2 * r] * sin[:, :r]
    ones_col = _ones_column(z.shape[0])
    for h in range(MLA_HEADS):
        q_ref[0, h] = jnp.concatenate([q_nope[:, h * d:(h + 1) * d], q_rot[:, h * r:(h + 1) * r]],
                                      axis=-1).astype(BF16)
        k_ref[0, h] = jnp.concatenate([k_nope[:, h * d:(h + 1) * d], k_rot], axis=-1).astype(BF16)
        v_ref[0, h] = jnp.concatenate([v[:, h * MLA_V_DIM:(h + 1) * MLA_V_DIM], ones_col],
                                      axis=-1).astype(BF16)


def _mla_prep(z_mla, z_small, cos, sin, qg, kvg, wq, wkv, tm=512):
    B, S, _ = z_mla.shape
    nr = MLA_HEADS * MLA_ROPE_DIM
    tok = lambda n: pl.BlockSpec((1, tm, n), lambda b, i: (b, i, 0))
    heads = lambda d: pl.BlockSpec((1, MLA_HEADS, tm, d), lambda b, i: (b, 0, i, 0))
    full = lambda a: pl.BlockSpec(a.shape, lambda b, i: (0,) * a.ndim)
    return pl.pallas_call(
        _mla_prep_kernel,
        grid=(B, S // tm),
        in_specs=[tok(z_mla.shape[2]), tok(SMALL_COLS), tok(nr), tok(nr),
                  full(qg), full(kvg), full(wq), full(wkv)],
        out_specs=[heads(MLA_QK_DIM), heads(MLA_QK_DIM), heads(2 * MLA_V_DIM)],
        out_shape=[jax.ShapeDtypeStruct((B, MLA_HEADS, S, d), BF16)
                   for d in (MLA_QK_DIM, MLA_QK_DIM, 2 * MLA_V_DIM)],
        compiler_params=_cparams(("parallel", "parallel")),
        name="mla_prep",
    )(z_mla, z_small, cos, sin, qg, kvg, wq, wkv)


CUM_CHUNK = 512


def _fox_cumsum_kernel(f_ref, b_ref, u_ref, o_ref):
    x = f_ref[0] + b_ref[...]
    lf = jnp.minimum(x, 0.0) - jnp.log1p(jnp.exp(-jnp.abs(x)))
    S = lf.shape[1]
    carry = jnp.zeros((lf.shape[0], 1), F32)
    for c in range(0, S, CUM_CHUNK):
        loc = jnp.dot(lf[:, c:c + CUM_CHUNK], u_ref[...], preferred_element_type=F32,
                      precision=lax.Precision.HIGHEST) + carry
        o_ref[0, :, c:c + CUM_CHUNK] = loc
        carry = loc[:, CUM_CHUNK - 1:CUM_CHUNK]


def _fox_cumsum(f_bhs, bias_col):
    B, H, S = f_bhs.shape
    assert S % CUM_CHUNK == 0
    upper = jnp.asarray(np.triu(np.ones((CUM_CHUNK, CUM_CHUNK), np.float32)))
    return pl.pallas_call(
        _fox_cumsum_kernel,
        grid=(B,),
        in_specs=[pl.BlockSpec((1, H, S), lambda b: (b, 0, 0)),
                  pl.BlockSpec((H, 1), lambda b: (0, 0)),
                  pl.BlockSpec((CUM_CHUNK, CUM_CHUNK), lambda b: (0, 0))],
        out_specs=pl.BlockSpec((1, H, S), lambda b: (b, 0, 0)),
        out_shape=jax.ShapeDtypeStruct((B, H, S), F32),
        compiler_params=_cparams(("parallel",)),
        name="fox_cumsum",
    )(f_bhs, bias_col, upper)


def _nsa_compress_kernel(r_ref, pe_ref, w_ref, o_ref):
    r = r_ref[0, 0]
    first = _dot((r + pe_ref[0, 0:1, :]).astype(BF16), w_ref[0, 0])
    second = _dot((r + pe_ref[0, 1:2, :]).astype(BF16), w_ref[0, 1])
    nc = r.shape[0]
    nxt = pltpu.roll(second, nc - 1, 0)
    row = lax.broadcasted_iota(jnp.int32, (nc, 1), 0)
    o_ref[0, 0] = jnp.where(row < nc - 1, first + nxt, 0.0).astype(o_ref.dtype)


def _nsa_compress(rows, pe, w):
    B, F, NC, K = rows.shape
    return pl.pallas_call(
        _nsa_compress_kernel,
        grid=(B, F),
        in_specs=[pl.BlockSpec((1, 1, NC, K), lambda b, f: (b, f, 0, 0)),
                  pl.BlockSpec((1, 2, K), lambda b, f: (f // NSA_KV_HEADS, 0, 0)),
                  pl.BlockSpec((1, 2, K, HEAD_DIM), lambda b, f: (f // NSA_KV_HEADS, 0, 0, 0))],
        out_specs=pl.BlockSpec((1, 1, NC, HEAD_DIM), lambda b, f: (b, f, 0, 0)),
        out_shape=jax.ShapeDtypeStruct((B, F, NC, HEAD_DIM), BF16),
        compiler_params=_cparams(("parallel", "parallel")),
        name="nsa_compress",
    )(rows, pe, w)


NSA_CMP_TQ = 256


def _nsa_cmp_kernel(q_ref, kc_ref, vc_ref, pq_ref, pc_ref, sl_ref, gate_ref, ovt_ref,
                    oc_ref, selt_ref, any_ref):
    G, tq, Dh = q_ref.shape[2:]
    R = G * tq
    i = pl.program_id(2)
    q = q_ref[0, 0].reshape(R, Dh)
    kc = kc_ref[0, 0]
    nc = kc.shape[0]
    n_sel = ovt_ref.shape[0]
    s = _dot_nt(q, kc)
    pq = jnp.concatenate([pq_ref[0]] * G, axis=0)
    s = s - sl_ref[0] * jnp.abs(pq - pc_ref[0])
    t_row = (i * tq + lax.broadcasted_iota(jnp.int32, (G, tq, 1), 1)).reshape(R, 1)
    c_col = lax.broadcasted_iota(jnp.int32, (1, nc), 1)
    c_end = jnp.where(c_col < nc - 1, c_col * NSA_CMP_STRIDE + (NSA_CMP_LEN - 1), jnp.int32(2 ** 30))
    s = jnp.where(c_end <= t_row, s, NEG_INF)
    m = jnp.max(s, axis=-1, keepdims=True)
    e = jnp.exp(s - m)
    l = jnp.sum(e, axis=-1, keepdims=True)
    p = e * jnp.where(m > 0.5 * NEG_INF, 1.0 / l, 0.0)
    o = _dot(p.astype(BF16), vc_ref[0, 0])
    oc_ref[0, 0] = (_gate_column(gate_ref[0, 0], 0, G) * o).reshape(G, tq, Dh)
    p_sum = jnp.sum(p.reshape(G, tq, nc), axis=0)
    imp = _dot_nt(ovt_ref[...], p_sum, precision=lax.Precision.HIGHEST)
    t_q = i * tq + lax.broadcasted_iota(jnp.int32, (1, tq), 1)
    jt = lax.shift_right_logical(t_q, NSA_SEL_LEN.bit_length() - 1)
    j = lax.broadcasted_iota(jnp.int32, (n_sel, 1), 0)
    forced = (j == 0) | (j == jt) | (j == jt - 1)
    causal = j <= jt
    val = jnp.where(forced, FORCE_SCORE, imp)
    val = jnp.where(causal, val, NEG_INF)
    sel = jnp.zeros((n_sel, tq), jnp.int32)
    for _ in range(min(NSA_TOP_N, n_sel)):
        best = jnp.max(val, axis=0, keepdims=True)
        idx = jnp.min(jnp.where(val == best, j, n_sel), axis=0, keepdims=True)
        pick = j == idx
        sel = jnp.where(pick, 1, sel)
        val = jnp.where(pick, -jnp.inf, val)
    keep = (sel > 0) & causal
    selt_ref[0, 0] = jnp.where(keep, 0.0, NEG_INF).astype(selt_ref.dtype)
    per_tile = any_ref.shape[2]
    any_ref[0, 0] = jnp.max(jnp.where(keep, 1.0, 0.0).reshape(per_tile, n_sel // per_tile, tq), axis=1)


def _nsa_cmp(q, kvc, pos_col, pc_row, slopes, gate, overlap_t, n_tiles):
    B, Hk, G, S, Dh = q.shape
    NC = kvc.shape[2]
    n_sel = overlap_t.shape[0]
    assert NSA_SEL_LEN & (NSA_SEL_LEN - 1) == 0 and n_sel % n_tiles == 0
    tq = NSA_CMP_TQ
    R = G * tq
    return pl.pallas_call(
        _nsa_cmp_kernel,
        grid=(B, Hk, S // tq),
        in_specs=[pl.BlockSpec((1, 1, G, tq, Dh), lambda b, h, i: (b, h, 0, i, 0)),
                  pl.BlockSpec((1, 1, NC, Dh), lambda b, h, i: (b, h, 0, 0)),
                  pl.BlockSpec((1, 1, NC, Dh), lambda b, h, i: (b, h + Hk, 0, 0)),
                  pl.BlockSpec((1, tq, 1), lambda b, h, i: (b, i, 0)),
                  pl.BlockSpec((1, 1, NC), lambda b, h, i: (b, 0, 0)),
                  pl.BlockSpec((1, R, 1), lambda b, h, i: (h, 0, 0)),
                  pl.BlockSpec((1, 1, tq, 3 * G), lambda b, h, i: (b, h, i, 0)),
                  pl.BlockSpec((n_sel, NC), lambda b, h, i: (0, 0))],
        out_specs=[pl.BlockSpec((1, 1, G, tq, Dh), lambda b, h, i: (b, h, 0, i, 0)),
                   pl.BlockSpec((1, 1, n_sel, tq), lambda b, h, i: (b, h, 0, i)),
                   pl.BlockSpec((1, 1, n_tiles, tq), lambda b, h, i: (b, h, 0, i))],
        out_shape=[jax.ShapeDtypeStruct((B, Hk, G, S, Dh), F32),
                   jax.ShapeDtypeStruct((B, Hk, n_sel, S), BF16),
                   jax.ShapeDtypeStruct((B, Hk, n_tiles, S), F32)],
        compiler_params=_cparams(("parallel", "parallel", "parallel")),
        name="nsa_cmp",
    )(q, kvc, kvc, pos_col, pc_row, slopes, gate, overlap_t)


class AttnCfg(NamedTuple):
    tq: int
    tk: int = 0
    window: int = 0
    alibi: bool = False
    decay: bool = False
    sink: bool = False
    gate: int = -1
    n_sub: int = 1
    add: bool = False
    selaug: bool = False
    sparse: bool = False
    out_dtype: object = BF16


def _gate_column(gates, branch, G):
    cols = [gates[:, 3 * g + branch:3 * g + branch + 1] for g in range(G)]
    return jax.nn.sigmoid(jnp.concatenate(cols, axis=0))


def _epilogue(cfg, out, gates, add, G, tq):
    R, Dv = out.shape
    if cfg.gate >= 0:
        out = out * _gate_column(gates, cfg.gate, G)
    out = out.reshape(G, tq, Dv)
    if cfg.add:
        out = out + add
    return out


def _flash_kernel(*refs, cfg):
    it = iter(refs)
    cnt_ref = tiles_ref = None
    if cfg.sparse:
        cnt_ref, tiles_ref = next(it), next(it)
    q_ref, k_ref, v_ref = next(it), next(it), next(it)
    pq_ref = pk_ref = sl_ref = cq_ref = ck_ref = gate_ref = add_ref = None
    selb_ref = next(it) if cfg.selaug else None
    if cfg.alibi:
        pq_ref, pk_ref, sl_ref = next(it), next(it), next(it)
    if cfg.decay:
        cq_ref, ck_ref = next(it), next(it)
    if cfg.gate >= 0:
        gate_ref = next(it)
    if cfg.add:
        add_ref = next(it)
    o_ref = next(it)
    m_s, acc_s, sa_s, sb_s = (next(it) for _ in range(4))
    pq_s = next(it) if cfg.alibi else None
    cq_s = next(it) if cfg.decay else None

    G, tq, Dq = q_ref.shape[2:]
    Da = v_ref.shape[3]
    Dv = Da // 2
    tk = cfg.tk
    nrep = tk // LANES
    R = G * tq
    i = pl.program_id(2)
    q = q_ref[0, 0].reshape(R, Dq)
    if cfg.selaug:
        bias = jnp.concatenate([selb_ref[0, 0].astype(F32)] * G, axis=0)
        q = jnp.concatenate([q.astype(F32), bias], axis=-1).astype(BF16)

    m_s[...] = jnp.full((R, LANES), NEG_INF, F32)
    acc_s[...] = jnp.zeros((R, Da), F32)
    if cfg.alibi:
        pq_s[...] = jnp.broadcast_to(jnp.concatenate([pq_ref[0]] * G, axis=0), (R, LANES))
    if cfg.decay:
        cq_all = cq_ref[0]
        head = lax.broadcasted_iota(jnp.int32, cq_all.shape, 1) == pl.program_id(1)
        cq_s[...] = jnp.broadcast_to(jnp.sum(jnp.where(head, cq_all, 0.0), axis=-1, keepdims=True), (R, LANES))

    def scores(j, s_ref):
        k0 = pl.multiple_of(j * tk, tk)
        s = _dot_nt(q, k_ref[0, 0, pl.ds(k0, tk), :])
        if cfg.alibi:
            s = s - _rep(sl_ref[0], nrep) * jnp.abs(_rep(pq_s[...], nrep) - pk_ref[0, j])
        if cfg.decay:
            s = s + (_rep(cq_s[...], nrep) - ck_ref[0, 0, j])
        s_ref[...] = s

    def update(j, s_ref, masked):
        s = s_ref[...]
        if masked:
            t_row = (i * tq + lax.broadcasted_iota(jnp.int32, (G, tq, tk), 1)).reshape(R, tk)
            col = j * tk + lax.broadcasted_iota(jnp.int32, (R, tk), 1)
            s = jnp.where(col <= t_row, s, NEG_INF)
        m_prev = m_s[...]
        m_new = jnp.maximum(m_prev, jnp.max(s, axis=-1, keepdims=True))
        alpha = jnp.exp(m_prev - m_new)
        p = jnp.exp((s - _rep(m_new, nrep)).astype(BF16))
        k0 = pl.multiple_of(j * tk, tk)
        acc_s[...] = _rep(alpha, Da // LANES) * acc_s[...] + _dot(p, v_ref[0, 0, pl.ds(k0, tk), :])
        m_s[...] = m_new

    diag = ((i + 1) * tq - 1) // tk
    if cfg.sparse:
        step_id = (pl.program_id(0) * pl.num_programs(1) + pl.program_id(1)) * pl.num_programs(2) + i
        n_vis = cnt_ref[step_id]
        nk = k_ref.shape[2] // tk

        def tile_at(pos):
            return jnp.where(pos < n_vis, tiles_ref[step_id * nk + jnp.minimum(pos, nk - 1)], diag)
    else:
        n_vis = diag

        def tile_at(pos):
            return pos

    scores(tile_at(0), sa_s)

    def pair(pp, c):
        pos = 2 * pp
        scores(tile_at(pos + 1), sb_s)
        update(tile_at(pos), sa_s, False)
        scores(tile_at(pos + 2), sa_s)
        update(tile_at(pos + 1), sb_s, False)
        return c
    lax.fori_loop(0, n_vis // 2, pair, 0)

    @pl.when(n_vis % 2 == 1)
    def _():
        scores(diag, sb_s)
        update(tile_at(n_vis - 1), sa_s, False)
        update(diag, sb_s, True)

    @pl.when(n_vis % 2 == 0)
    def _():
        update(diag, sa_s, True)

    acc = acc_s[...]
    out = acc[:, :Dv] / acc[:, Dv:Dv + 1]
    out = _epilogue(cfg, out, gate_ref[0, 0] if cfg.gate >= 0 else None, add_ref[0, 0] if cfg.add else None,
                    G, tq)
    o_ref[0, 0] = out.astype(o_ref.dtype)


def _flash(cfg, q, k, v, *, visit=None, selb=None, pos_col=None, pos_rows=None, slopes=None,
           cum_col=None, cum_rows=None, gate=None, add=None):
    B, Hk, G, S, Dq = q.shape
    Dk, Da = k.shape[3], v.shape[3]
    Dv = Da // 2
    tq, tk = cfg.tq, cfg.tk
    assert S % tq == 0 and S % tk == 0 and tk % tq == 0 and tk % LANES == 0 and Da % LANES == 0
    assert Dk == Dq + (selb.shape[3] if cfg.selaug else 0) and cfg.sparse == (visit is not None)
    R = G * tq
    nk = S // tk
    qmap = lambda b, h, i, *_: (b, h, 0, i, 0)
    kvmap = lambda b, h, i, *_: (b, h, 0, 0)
    args = [q, k, v]
    specs = [pl.BlockSpec((1, 1, G, tq, Dq), qmap),
             pl.BlockSpec((1, 1, S, Dk), kvmap),
             pl.BlockSpec((1, 1, S, Da), kvmap)]
    scratch = [pltpu.VMEM((R, LANES), F32), pltpu.VMEM((R, Da), F32),
               pltpu.VMEM((R, tk), F32), pltpu.VMEM((R, tk), F32)]
    if cfg.selaug:
        args += [selb]
        specs += [pl.BlockSpec((1, 1, tq, Dk - Dq), lambda b, h, i, *_: (b, h, i, 0))]
    if cfg.alibi:
        args += [pos_col, pos_rows, slopes]
        specs += [pl.BlockSpec((1, tq, 1), lambda b, h, i, *_: (b, i, 0)),
                  pl.BlockSpec((1, nk, 1, tk), lambda b, h, i, *_: (b, 0, 0, 0)),
                  pl.BlockSpec((1, R, LANES), lambda b, h, i, *_: (h, 0, 0))]
        scratch.append(pltpu.VMEM((R, LANES), F32))
    if cfg.decay:
        args += [cum_col, cum_rows]
        assert G == 1
        specs += [pl.BlockSpec((1, tq, Hk), lambda b, h, i, *_: (b, i, 0)),
                  pl.BlockSpec((1, 1, nk, 1, tk), lambda b, h, i, *_: (b, h, 0, 0, 0))]
        scratch.append(pltpu.VMEM((R, LANES), F32))
    if cfg.gate >= 0:
        args += [gate]
        specs += [pl.BlockSpec((1, 1, tq, 3 * G), lambda b, h, i, *_: (b, h, i, 0))]
    if cfg.add:
        args += [add]
        specs += [pl.BlockSpec((1, 1, G, tq, Dv), qmap)]
    grid_spec = pltpu.PrefetchScalarGridSpec(
        num_scalar_prefetch=2 if cfg.sparse else 0,
        grid=(B, Hk, S // tq),
        in_specs=specs,
        out_specs=pl.BlockSpec((1, 1, G, tq, Dv), qmap),
        scratch_shapes=scratch)
    return pl.pallas_call(
        functools.partial(_flash_kernel, cfg=cfg),
        grid_spec=grid_spec,
        out_shape=jax.ShapeDtypeStruct((B, Hk, G, S, Dv), cfg.out_dtype),
        compiler_params=_cparams(("parallel", "parallel", "parallel")),
        name="flash",
    )(*(list(visit) if cfg.sparse else []), *args)


def _band_kernel(*refs, cfg):
    it = iter(refs)
    q_ref, k_ref, v_ref, pq_ref, pk_ref, sl_ref = (next(it) for _ in range(6))
    sink_ref = next(it) if cfg.sink else None
    gate_ref = next(it) if cfg.gate >= 0 else None
    add_ref = next(it) if cfg.add else None
    o_ref = next(it)

    G, Dq = q_ref.shape[2], q_ref.shape[4]
    tq = cfg.tq
    R = G * tq
    W = cfg.window
    span = W + tq
    slope = sl_ref[0]
    d = (lax.broadcasted_iota(jnp.int32, (G, tq, LANES), 1)
         - lax.broadcasted_iota(jnp.int32, (G, tq, LANES), 2)).reshape(R, LANES)
    for sub in range(cfg.n_sub):
        i = pl.program_id(2) * cfg.n_sub + sub
        tok = slice(sub * tq, (sub + 1) * tq)
        start = pl.multiple_of(jnp.maximum(i * tq - W, 0), LANES)
        blk0 = lax.shift_right_logical(start, LANES.bit_length() - 1)
        q = q_ref[0, 0, :, tok, :].reshape(R, Dq)
        s = _dot_nt(q, k_ref[0, 0, pl.ds(start, span), :])
        pq = jnp.concatenate([pq_ref[0, tok, :]] * G, axis=0)
        off = i * tq - start
        chunks = []
        for c in range(span // LANES):
            sc = s[:, c * LANES:(c + 1) * LANES]
            sc = sc - slope * jnp.abs(pq - pk_ref[0, blk0 + c])
            lo = c * LANES - off
            sc = jnp.where(d >= lo, jnp.where(d < lo + W, sc, NEG_INF), NEG_INF)
            chunks.append(sc)
        s = jnp.concatenate(chunks, axis=1)
        m = jnp.max(s, axis=-1, keepdims=True)
        e = jnp.exp(s - m)
        l = jnp.sum(e, axis=-1, keepdims=True)
        acc = _dot(e.astype(BF16), v_ref[0, 0, pl.ds(start, span), :])
        if cfg.sink:
            sink = sink_ref[0]
            m_f = jnp.maximum(m, sink)
            w = jnp.exp(m - m_f)
            out = acc * (w / (l * w + jnp.exp(sink - m_f)))
        else:
            out = acc / l
        out = _epilogue(cfg, out, gate_ref[0, 0, tok, :] if cfg.gate >= 0 else None,
                        add_ref[0, 0, :, tok, :] if cfg.add else None, G, tq)
        o_ref[0, 0, :, tok, :] = out.astype(o_ref.dtype)


def _band(cfg, q, k, v, *, pos_col, pos_rows, slopes, sinks=None, gate=None, add=None):
    B, Hk, G, S, Dq = q.shape
    Dv = v.shape[3]
    tq = cfg.tq
    T = tq * cfg.n_sub
    assert tq == LANES and cfg.window % LANES == 0 and S % T == 0 and S >= cfg.window + tq
    R = G * tq
    qmap = lambda b, h, i: (b, h, 0, i, 0)
    kvmap = lambda b, h, i: (b, h, 0, 0)
    args = [q, k, v, pos_col, pos_rows, slopes]
    specs = [pl.BlockSpec((1, 1, G, T, Dq), qmap),
             pl.BlockSpec((1, 1, S, Dq), kvmap),
             pl.BlockSpec((1, 1, S, Dv), kvmap),
             pl.BlockSpec((1, T, 1), lambda b, h, i: (b, i, 0)),
             pl.BlockSpec((1, S // LANES, 1, LANES), lambda b, h, i: (b, 0, 0, 0)),
             pl.BlockSpec((1, R, 1), lambda b, h, i: (h, 0, 0))]
    if cfg.sink:
        args += [sinks]
        specs += [pl.BlockSpec((1, R, 1), lambda b, h, i: (h, 0, 0))]
    if cfg.gate >= 0:
        args += [gate]
        specs += [pl.BlockSpec((1, 1, T, 3 * G), lambda b, h, i: (b, h, i, 0))]
    if cfg.add:
        args += [add]
        specs += [pl.BlockSpec((1, 1, G, T, Dv), qmap)]
    return pl.pallas_call(
        functools.partial(_band_kernel, cfg=cfg),
        grid=(B, Hk, S // T),
        in_specs=specs,
        out_specs=pl.BlockSpec((1, 1, G, T, Dv), qmap),
        out_shape=jax.ShapeDtypeStruct((B, Hk, G, S, Dv), cfg.out_dtype),
        compiler_params=_cparams(("parallel", "parallel", "parallel")),
        name="band",
    )(*args)


def _merge_kernel(x_ref, y0_ref, y1_ref, y2_ref, y3_ref, gpre_ref, wg_ref, wb_ref, wo_ref, gpost_ref,
                  o_ref):
    x = x_ref[0]
    h = _rms(x, gpre_ref[...]).astype(BF16)
    merged = None
    for b, y_ref in enumerate((y0_ref, y1_ref, y2_ref, y3_ref)):
        gate = jax.nn.sigmoid(_dot(h, wg_ref[b]))
        y = jnp.concatenate([y_ref[0, hd] for hd in range(y_ref.shape[1])], axis=-1)
        br = _dot(y, wb_ref[b])
        merged = gate * br if merged is None else merged + gate * br
    y = _dot(merged.astype(BF16), wo_ref[...])
    o_ref[0] = x + _rms(y, gpost_ref[...])


def _merge(x, ys, gpre, wg, wb, wo, gpost, tm=512):
    B, S, D = x.shape
    full = lambda a: pl.BlockSpec(a.shape, lambda b, i: (0,) * a.ndim, pipeline_mode=pl.Buffered(1))
    yspec = lambda y: pl.BlockSpec((1, y.shape[1], tm, y.shape[3]), lambda b, i: (b, 0, i, 0))
    return pl.pallas_call(
        _merge_kernel,
        grid=(B, S // tm),
        in_specs=[pl.BlockSpec((1, tm, D), lambda b, i: (b, i, 0))] + [yspec(y) for y in ys]
                 + [full(gpre), full(wg), full(wb), full(wo), full(gpost)],
        out_specs=pl.BlockSpec((1, tm, D), lambda b, i: (b, i, 0)),
        out_shape=jax.ShapeDtypeStruct((B, S, D), F32),
        compiler_params=_cparams(("parallel", "parallel")),
        name="merge",
    )(x, *ys, gpre, wg, wb, wo, gpost)


def _mem_kv_kernel(mem_ref, g_ref, w_ref, o_ref):
    o_ref[0] = _dot(_rms(mem_ref[0], g_ref[...]).astype(BF16), w_ref[...]).astype(BF16)


def _mem_kv(mem, g, w):
    B, Mm, D = mem.shape
    N = w.shape[1]
    return pl.pallas_call(
        _mem_kv_kernel,
        grid=(B,),
        in_specs=[pl.BlockSpec((1, Mm, D), lambda b: (b, 0, 0)),
                  pl.BlockSpec((1, D), lambda b: (0, 0)),
                  pl.BlockSpec((D, N), lambda b: (0, 0))],
        out_specs=pl.BlockSpec((1, Mm, N), lambda b: (b, 0, 0)),
        out_shape=jax.ShapeDtypeStruct((B, Mm, N), BF16),
        compiler_params=_cparams(("parallel",)),
        name="mem_kv",
    )(mem, g, w)


def _xattn_kernel(x_ref, kv_ref, gpre_ref, wq_ref, wo_ref, gpost_ref, o_ref):
    x = x_ref[0]
    h = _rms(x, gpre_ref[...]).astype(BF16)
    q_all = _dot(h, wq_ref[...])
    outs = []
    for hd in range(XA_HEADS):
        q = q_all[:, hd * HEAD_DIM:(hd + 1) * HEAD_DIM].astype(BF16)
        s = _dot_nt(q, kv_ref[0, hd])
        m = jnp.max(s, axis=-1, keepdims=True)
        e = jnp.exp(s - m)
        p = e / jnp.sum(e, axis=-1, keepdims=True)
        outs.append(_dot(p.astype(BF16), kv_ref[0, XA_HEADS + hd]))
    y = _dot(jnp.concatenate(outs, axis=-1).astype(BF16), wo_ref[...])
    o_ref[0] = x + _rms(y, gpost_ref[...])


def _xattn(x, kv, gpre, wq, wo, gpost, tm=512):
    B, S, D = x.shape
    full = lambda a: pl.BlockSpec(a.shape, lambda b, i: (0,) * a.ndim)
    return pl.pallas_call(
        _xattn_kernel,
        grid=(B, S // tm),
        in_specs=[pl.BlockSpec((1, tm, D), lambda b, i: (b, i, 0)),
                  pl.BlockSpec((1,) + kv.shape[1:], lambda b, i: (b, 0, 0, 0)),
                  full(gpre), full(wq), full(wo), full(gpost)],
        out_specs=pl.BlockSpec((1, tm, D), lambda b, i: (b, i, 0)),
        out_shape=jax.ShapeDtypeStruct((B, S, D), F32),
        compiler_params=_cparams(("parallel", "parallel")),
        name="xattn",
    )(x, kv, gpre, wq, wo, gpost)


FFN_CHUNK = 256


def _ffn_kernel(x_ref, gpre_ref, wg_ref, wu_ref, wo_ref, gpost_ref, o_ref):
    x = x_ref[...]
    h = _rms(x, gpre_ref[...]).astype(BF16)
    y = None
    for c in range(0, D_FF, FFN_CHUNK):
        g = _dot(h, wg_ref[:, c:c + FFN_CHUNK])
        u = _dot(h, wu_ref[:, c:c + FFN_CHUNK])
        a = (g * jax.nn.sigmoid(g) * u).astype(BF16)
        part = _dot(a, wo_ref[c:c + FFN_CHUNK, :])
        y = part if y is None else y + part
    o_ref[...] = x + _rms(y, gpost_ref[...])


def _ffn(x2d, gpre, wg, wu, wo, gpost, tm=512):
    M, D = x2d.shape
    full = lambda a: pl.BlockSpec(a.shape, lambda i: (0,) * a.ndim, pipeline_mode=pl.Buffered(1))
    return pl.pallas_call(
        _ffn_kernel,
        grid=(M // tm,),
        in_specs=[pl.BlockSpec((tm, D), lambda i: (i, 0)),
                  full(gpre), full(wg), full(wu), full(wo), full(gpost)],
        out_specs=pl.BlockSpec((tm, D), lambda i: (i, 0)),
        out_shape=jax.ShapeDtypeStruct((M, D), F32),
        compiler_params=_cparams(("parallel",)),
        name="ffn",
    )(x2d, gpre, wg, wu, wo, gpost)


def _alibi_slopes(n):
    return (2.0 ** (-8.0 * np.arange(1, n + 1) / n)).astype(np.float32)


def _rotate_half_cols(w):
    half = w.shape[-1] // 2
    return jnp.concatenate([-w[..., half:], w[..., :half]], axis=-1)


def _prep_weights(p):
    L = p["w_in"].shape[0]
    offs = np.concatenate([[0], np.cumsum(IN_SPLITS)])
    col = {n: p["w_in"][:, :, offs[k]:offs[k + 1]] for k, n in enumerate(IN_NAMES)}
    qk_scale = HEAD_DIM ** -0.5
    for n in ("nq", "sq", "fq"):
        col[n] = col[n] * qk_scale
    pad = jnp.zeros((L, D_MODEL, SMALL_COLS - 2 * MLA_ROPE_DIM - 24 - FOX_HEADS), F32)
    small = jnp.concatenate([col["mkr"], _rotate_half_cols(col["mkr"]), col["ng"], col["ff"], pad], -1)
    w_all = jnp.concatenate([col["mcq"], col["mckv"], col["nkc"], col["nvc"], small]
                            + [col[n] for n in MAIN_ORDER], axis=-1).astype(BF16)

    wq = p["mla_w_q_up"].reshape(L, MLA_Q_RANK, MLA_HEADS, MLA_QK_DIM)
    wq_nope = wq[..., :MLA_NOPE_DIM].reshape(L, MLA_Q_RANK, -1)
    wq_rot = wq[..., MLA_NOPE_DIM:]
    wq_all = jnp.concatenate([wq_nope, wq_rot.reshape(L, MLA_Q_RANK, -1),
                              _rotate_half_cols(wq_rot).reshape(L, MLA_Q_RANK, -1)], -1).astype(BF16)
    wkv = p["mla_w_kv_up"].reshape(L, MLA_KV_RANK, MLA_HEADS, MLA_NOPE_DIM + MLA_V_DIM)
    wkv_all = jnp.concatenate([wkv[..., :MLA_NOPE_DIM].reshape(L, MLA_KV_RANK, -1),
                               wkv[..., MLA_NOPE_DIM:].reshape(L, MLA_KV_RANK, -1)], -1).astype(BF16)

    half = NSA_CMP_LEN // 2
    cmp_w = p["nsa_cmp_w"].reshape(L, 2, 2, half * HEAD_DIM, HEAD_DIM).astype(BF16)
    cmp_pe = p["nsa_cmp_pe"].reshape(L, 2, 2, half * HEAD_DIM)

    xa_wq = (p["xa_w_q"] * qk_scale).astype(BF16)
    xa_wo = p["xa_w_o"].astype(BF16)

    row = lambda a: a[:, None, :]
    return dict(
        w_all=w_all, mix_norm_pre=row(p["mix_norm_pre"]),
        wq_all=wq_all, wkv_all=wkv_all, mla_q_norm=row(p["mla_q_norm"]), mla_kv_norm=row(p["mla_kv_norm"]),
        cmp_w=cmp_w, cmp_pe=cmp_pe, swa_sinks=p["swa_sinks"], fox_f_bias=p["fox_f_bias"][:, :, None],
        w_gate=p["w_gate"].astype(BF16), w_branch=p["w_branch"].astype(BF16), w_out=p["w_out"].astype(BF16),
        mix_norm_post=row(p["mix_norm_post"]),
        xa_norm_pre=row(p["xa_norm_pre"]), xa_mem_norm=row(p["xa_mem_norm"]), xa_wq=xa_wq,
        xa_wkv=p["xa_w_kv"].astype(BF16), xa_wo=xa_wo, xa_norm_post=row(p["xa_norm_post"]),
        ffn_norm_pre=row(p["ffn_norm_pre"]), ffn_wg=p["ffn_w_in"][:, :, :D_FF].astype(BF16),
        ffn_wu=p["ffn_w_in"][:, :, D_FF:].astype(BF16), ffn_wo=p["ffn_w_out"].astype(BF16),
        ffn_norm_post=row(p["ffn_norm_post"]),
    )


FLASH_TQ = 512
FLASH_TK = 512
BAND_TQ = LANES
BAND_SUB = 4


def _per_row(vals_hg, tq, lanes=1):
    Hk, G = vals_hg.shape
    return jnp.broadcast_to(vals_hg[:, :, None, None], (Hk, G, tq, lanes)).reshape(Hk, G * tq, lanes)


def _visit_lists(needed, tq, tk):
    nq, nk = needed.shape[2:]
    diag = ((np.arange(nq)[:, None] + 1) * tq - 1) // tk
    needed = needed & jnp.asarray(np.arange(nk)[None, :] < diag)
    count = jnp.sum(needed, axis=-1).astype(jnp.int32)
    tiles = jnp.argsort(~needed, axis=-1, stable=True).astype(jnp.int32)
    return count.reshape(-1), tiles.reshape(-1)


ZERO_WEIGHT_LOGIT = -110.0


def _fox_tiles_needed(q, k, cum, tq, tk):
    B, H, S, _ = q.shape
    qn = jnp.sqrt(jnp.sum(jnp.square(q.astype(F32)), axis=-1))
    kmax = jnp.sqrt(jnp.max(jnp.sum(jnp.square(k.astype(F32)), axis=-1), axis=-1))
    row_bound = cum + 2.02 * qn * kmax[..., None]
    tile_bound = jnp.max(row_bound.reshape(B, H, S // tq, tq), axis=-1)
    cum_last = cum[:, :, tk - 1::tk]
    return tile_bound[..., None] - cum_last[:, :, None, :] > ZERO_WEIGHT_LOGIT


def _layer(x, w, c):
    B, S, D = x.shape
    M = B * S
    Hk, G, Dh = NSA_KV_HEADS, NSA_GROUP, HEAD_DIM

    z_mla, z_small, z_cmp, *heads = _inproj(x, w["mix_norm_pre"], w["w_all"])
    hm = dict(zip(MAIN_ORDER, heads))
    grouped = lambda t: t.reshape(B, Hk, G, S, t.shape[-1])
    single = lambda t: t[:, :, None]
    flat = lambda o: o.reshape(B, -1, S, o.shape[-1])

    ng = z_small[:, :, 64:88].reshape(B, S, Hk, 3 * G).transpose(0, 2, 1, 3)
    nq = grouped(hm["nq"])
    rows = z_cmp.reshape(B, 2 * Hk, S // NSA_CMP_STRIDE, NSA_CMP_STRIDE * Dh)
    kvc = _nsa_compress(rows, w["cmp_pe"], w["cmp_w"])
    nk = S // FLASH_TK
    oc, selt, tile_any = _nsa_cmp(nq, kvc, c["pos_col"], c["pc_row"], c["nsa_slopes_cmp"], ng,
                                  c["overlap_t"], nk)
    selb = selt.transpose(0, 1, 3, 2)
    sel_tq = FLASH_TQ // G
    used = tile_any.reshape(B, Hk, nk, S // sel_tq, sel_tq).max(-1).transpose(0, 1, 3, 2) > 0.0
    o_sel = _flash(
        AttnCfg(tq=sel_tq, tk=FLASH_TK, alibi=True, gate=1, add=True, selaug=True, sparse=True,
                out_dtype=F32),
        nq, hm["nks"], hm["nvs"], visit=_visit_lists(used, sel_tq, FLASH_TK), selb=selb,
        pos_col=c["pos_col"], pos_rows=c["pos_rows_flash"], slopes=c["nsa_slopes_flash"], gate=ng, add=oc)
    y_nsa = _band(
        AttnCfg(tq=BAND_TQ, n_sub=BAND_SUB, window=NSA_WINDOW, alibi=True, gate=2, add=True),
        nq, hm["nkw"], hm["nvw"], pos_col=c["pos_col"], pos_rows=c["pos_rows_band"],
        slopes=c["nsa_slopes_band"], gate=ng, add=o_sel)

    y_swa = _band(
        AttnCfg(tq=BAND_TQ, n_sub=BAND_SUB, window=SWA_WINDOW, alibi=True, sink=True),
        grouped(hm["sq"]), hm["sk"], hm["sv"], pos_col=c["pos_col"],
        pos_rows=c["pos_rows_band"], slopes=c["swa_slopes_band"],
        sinks=_per_row(w["swa_sinks"].reshape(SWA_KV_HEADS, SWA_GROUP), BAND_TQ))

    q_mla, k_mla, v_mla = _mla_prep(z_mla, z_small, c["cos"], c["sin"], w["mla_q_norm"], w["mla_kv_norm"],
                                    w["wq_all"], w["wkv_all"])
    y_mla = _flash(AttnCfg(tq=FLASH_TQ, tk=FLASH_TK), single(q_mla), k_mla, v_mla)

    f_logit = z_small[:, :, 88:96].transpose(0, 2, 1)
    cum = _fox_cumsum(f_logit, w["fox_f_bias"])
    y_fox = _flash(
        AttnCfg(tq=FLASH_TQ, tk=FLASH_TK, decay=True, sparse=True), single(hm["fq"]), hm["fk"], hm["fv"],
        visit=_visit_lists(_fox_tiles_needed(hm["fq"], hm["fk"], cum, FLASH_TQ, FLASH_TK), FLASH_TQ, FLASH_TK),
        cum_col=cum.transpose(0, 2, 1), cum_rows=cum.reshape(B, FOX_HEADS, S // FLASH_TK, 1, FLASH_TK))

    x1 = _merge(x, [flat(y_nsa), flat(y_swa), flat(y_mla), flat(y_fox)], w["mix_norm_pre"], w["w_gate"],
                w["w_branch"], w["w_out"], w["mix_norm_post"])

    kv = _mem_kv(c["mem"], w["xa_mem_norm"], w["xa_wkv"])
    kv = kv.reshape(B, -1, 2 * XA_HEADS, Dh).transpose(0, 2, 1, 3)
    x2 = _xattn(x1, kv, w["xa_norm_pre"], w["xa_wq"], w["xa_wo"], w["xa_norm_post"])

    x3 = _ffn(x2.reshape(M, D), w["ffn_norm_pre"], w["ffn_wg"], w["ffn_wu"], w["ffn_wo"], w["ffn_norm_post"])
    return x3.reshape(B, S, D)


def _constants(mem, positions, S):
    B = positions.shape[0]
    pos = positions.astype(F32)
    half = MLA_ROPE_DIM // 2
    inv = (ROPE_THETA ** (-np.arange(half, dtype=np.float32) / half)).astype(np.float32)
    inv_row = jnp.asarray(np.tile(inv, 2 * MLA_HEADS)[None, :])
    cos, sin = _rope_tables(pos.reshape(B * S, 1), inv_row)
    nc = S // NSA_CMP_STRIDE
    n_valid = (S - NSA_CMP_LEN) // NSA_CMP_STRIDE + 1
    assert n_valid == nc - 1
    c_end = np.minimum(np.arange(nc) * NSA_CMP_STRIDE + NSA_CMP_LEN - 1, S - 1)
    n_sel = S // NSA_SEL_LEN
    c_start = np.arange(nc) * NSA_CMP_STRIDE
    sel_start = np.arange(n_sel) * NSA_SEL_LEN
    overlap = ((c_start[:, None] < sel_start[None, :] + NSA_SEL_LEN)
               & (c_start[:, None] + NSA_CMP_LEN - 1 >= sel_start[None, :])
               & (np.arange(nc)[:, None] < n_valid)).astype(np.float32)
    nsa_slopes = jnp.asarray(_alibi_slopes(NSA_HEADS).reshape(NSA_KV_HEADS, NSA_GROUP))
    swa_slopes = jnp.asarray(_alibi_slopes(SWA_HEADS).reshape(SWA_KV_HEADS, SWA_GROUP))
    return dict(
        mem=mem, cos=cos.reshape(B, S, -1), sin=sin.reshape(B, S, -1),
        pos_col=pos[:, :, None],
        pos_rows_flash=pos.reshape(B, S // FLASH_TK, 1, FLASH_TK),
        pos_rows_band=pos.reshape(B, S // LANES, 1, LANES),
        pc_row=pos[:, c_end][:, None, :],
        overlap_t=jnp.asarray(overlap.T),
        nsa_slopes_cmp=_per_row(nsa_slopes, NSA_CMP_TQ),
        nsa_slopes_flash=_per_row(nsa_slopes, FLASH_TQ // NSA_GROUP, LANES),
        nsa_slopes_band=_per_row(nsa_slopes, BAND_TQ),
        swa_slopes_band=_per_row(swa_slopes, BAND_TQ),
    )


def kernel(x, mem, positions, mix_norm_pre, w_in, nsa_cmp_w, nsa_cmp_pe, swa_sinks, mla_q_norm, mla_w_q_up,
           mla_kv_norm, mla_w_kv_up, fox_f_bias, w_branch, w_gate, w_out, mix_norm_post, xa_norm_pre,
           xa_mem_norm, xa_w_q, xa_w_kv, xa_w_o, xa_norm_post, ffn_norm_pre, ffn_w_in, ffn_w_out,
           ffn_norm_post):
    params = dict(
        mix_norm_pre=mix_norm_pre, w_in=w_in, nsa_cmp_w=nsa_cmp_w, nsa_cmp_pe=nsa_cmp_pe, swa_sinks=swa_sinks,
        mla_q_norm=mla_q_norm, mla_w_q_up=mla_w_q_up, mla_kv_norm=mla_kv_norm, mla_w_kv_up=mla_w_kv_up,
        fox_f_bias=fox_f_bias, w_branch=w_branch, w_gate=w_gate, w_out=w_out, mix_norm_post=mix_norm_post,
        xa_norm_pre=xa_norm_pre, xa_mem_norm=xa_mem_norm, xa_w_q=xa_w_q, xa_w_kv=xa_w_kv, xa_w_o=xa_w_o,
        xa_norm_post=xa_norm_post, ffn_norm_pre=ffn_norm_pre, ffn_w_in=ffn_w_in, ffn_w_out=ffn_w_out,
        ffn_norm_post=ffn_norm_post)
    S = x.shape[1]
    weights = _prep_weights(params)
    consts = _constants(mem, positions, S)

    def body(xc, w):
        return _layer(xc, w, consts), None

    out, _ = lax.scan(body, x, weights)
    return out
```

```python
import functools
from typing import NamedTuple

import numpy as np
import jax
import jax.numpy as jnp
from jax import lax
from jax.experimental import pallas as pl
from jax.experimental.pallas import tpu as pltpu

F32 = jnp.float32
BF16 = jnp.bfloat16

D_MODEL = 1024
HEAD_DIM = 64
NORM_EPS = 1e-6
NEG_INF = -1e30
FORCE_SCORE = 1e9
LANES = 128

NSA_HEADS = 8
NSA_KV_HEADS = 2
NSA_GROUP = NSA_HEADS // NSA_KV_HEADS
NSA_CMP_LEN = 32
NSA_CMP_STRIDE = 16
NSA_SEL_LEN = 64
NSA_TOP_N = 8
NSA_WINDOW = 512

SWA_HEADS = 8
SWA_KV_HEADS = 2
SWA_GROUP = SWA_HEADS // SWA_KV_HEADS
SWA_WINDOW = 128

MLA_HEADS = 8
MLA_Q_RANK = 256
MLA_KV_RANK = 128
MLA_NOPE_DIM = 64
MLA_ROPE_DIM = 32
MLA_V_DIM = 64
MLA_QK_DIM = MLA_NOPE_DIM + MLA_ROPE_DIM
ROPE_THETA = 10000.0

FOX_HEADS = 8
XA_HEADS = 4
N_BRANCH = 4
BRANCH_WIDTH = 512
D_FF = ((8 * D_MODEL + 3 * 256 - 1) // (3 * 256)) * 256

IN_SPLITS = (512, 128, 128, 128, 128, 128, 128, 24,
             512, 128, 128,
             MLA_Q_RANK, MLA_KV_RANK, MLA_ROPE_DIM,
             512, 512, 512, FOX_HEADS)
IN_NAMES = ("nq", "nkc", "nvc", "nks", "nvs", "nkw", "nvw", "ng",
            "sq", "sk", "sv", "mcq", "mckv", "mkr", "fq", "fk", "fv", "ff")

VMEM_LIMIT_BYTES = 56 * 1024 * 1024

MAIN_ORDER = ("nq", "nks", "nvs", "nkw", "nvw", "sq", "sk", "sv", "fq", "fk", "fv")
MAIN_COLS = 3328
SMALL_COLS = 128


def _cparams(sem):
    return pltpu.CompilerParams(dimension_semantics=sem, vmem_limit_bytes=VMEM_LIMIT_BYTES)


def _rms(x, g):
    return x * lax.rsqrt(jnp.mean(x * x, axis=-1, keepdims=True) + NORM_EPS) * g


def _dot(a, b):
    return jnp.dot(a, b, preferred_element_type=F32)


def _dot_nt(a, b, precision=None):
    return lax.dot_general(a, b, (((1,), (1,)), ((), ())), preferred_element_type=F32,
                           precision=precision)


def _rep(x, n):
    return x if n == 1 else jnp.concatenate([x] * n, axis=1)


F32_COLS = MLA_Q_RANK + MLA_KV_RANK + 4 * HEAD_DIM + SMALL_COLS
MAIN_HEADS = tuple(8 if n in ("nq", "sq", "fq", "fk", "fv") else 2 for n in MAIN_ORDER)
PROJ_CHUNK = 512
FLASH_V_NAMES = ("nvs", "fv")


def _ones_column(rows):
    return jnp.where(lax.broadcasted_iota(jnp.int32, (rows, HEAD_DIM), 1) == 0, 1.0, 0.0)


def _inproj_kernel(x_ref, g_ref, w_ref, zmla_ref, zsmall_ref, cmp_ref, *head_refs):
    tm = x_ref.shape[1]
    h = _rms(x_ref[0], g_ref[...]).astype(BF16)
    n_mla = MLA_Q_RANK + MLA_KV_RANK
    zmla_ref[0] = _dot(h, w_ref[:, :n_mla])
    r = _dot(h, w_ref[:, n_mla:F32_COLS])
    for f in range(4):
        cmp_ref[0, f] = r[:, f * HEAD_DIM:(f + 1) * HEAD_DIM]
    zsmall_ref[0] = r[:, 4 * HEAD_DIM:]

    n_sel = head_refs[MAIN_ORDER.index("nks")].shape[3] - HEAD_DIM
    t = pl.program_id(1) * tm + lax.broadcasted_iota(jnp.int32, (tm, n_sel), 0)
    blk = lax.shift_right_logical(t, NSA_SEL_LEN.bit_length() - 1)
    onehot = jnp.where(blk == lax.broadcasted_iota(jnp.int32, (tm, n_sel), 1), 1.0, 0.0)

    ones_col = _ones_column(tm)
    extra = {"nks": onehot, **{n: ones_col for n in FLASH_V_NAMES}}
    dest = [(ref, hd, extra.get(name)) for ref, nh, name in zip(head_refs, MAIN_HEADS, MAIN_ORDER)
            for hd in range(nh)]
    for c in range(0, MAIN_COLS, PROJ_CHUNK):
        w = min(PROJ_CHUNK, MAIN_COLS - c)
        res = _dot(h, w_ref[:, F32_COLS + c:F32_COLS + c + w])
        for k in range(w // HEAD_DIM):
            ref, hd, appended = dest[c // HEAD_DIM + k]
            piece = res[:, k * HEAD_DIM:(k + 1) * HEAD_DIM]
            if appended is not None:
                piece = jnp.concatenate([piece, appended], axis=-1)
            ref[0, hd] = piece.astype(ref.dtype)


def _inproj(x, g, w, tm=512):
    B, S, D = x.shape
    N = w.shape[1]
    n_sel = S // NSA_SEL_LEN
    assert N == F32_COLS + MAIN_COLS and S % tm == 0
    tok = lambda n: pl.BlockSpec((1, tm, n), lambda b, i: (b, i, 0))
    heads = lambda nh, d: pl.BlockSpec((1, nh, tm, d), lambda b, i: (b, 0, i, 0))
    head_dims = [HEAD_DIM + n_sel if n == "nks" else 2 * HEAD_DIM if n in FLASH_V_NAMES else HEAD_DIM
                 for n in MAIN_ORDER]
    return pl.pallas_call(
        _inproj_kernel,
        grid=(B, S // tm),
        in_specs=[pl.BlockSpec((1, tm, D), lambda b, i: (b, i, 0)),
                  pl.BlockSpec((1, D), lambda b, i: (0, 0)),
                  pl.BlockSpec((D, N), lambda b, i: (0, 0))],
        out_specs=[tok(MLA_Q_RANK + MLA_KV_RANK), tok(SMALL_COLS), heads(4, HEAD_DIM)]
                  + [heads(nh, d) for nh, d in zip(MAIN_HEADS, head_dims)],
        out_shape=[jax.ShapeDtypeStruct((B, S, MLA_Q_RANK + MLA_KV_RANK), F32),
                   jax.ShapeDtypeStruct((B, S, SMALL_COLS), F32),
                   jax.ShapeDtypeStruct((B, 4, S, HEAD_DIM), F32)]
                  + [jax.ShapeDtypeStruct((B, nh, S, d), BF16) for nh, d in zip(MAIN_HEADS, head_dims)],
        compiler_params=_cparams(("parallel", "parallel")),
        name="inproj",
    )(x, g, w)


def _rope_table_kernel(pos_ref, inv_ref, cos_ref, sin_ref):
    ang = pos_ref[...] * inv_ref[...]
    cos_ref[...] = jnp.cos(ang)
    sin_ref[...] = jnp.sin(ang)


def _rope_tables(pos_col, inv_row, tm=512):
    M = pos_col.shape[0]
    N = inv_row.shape[1]
    return pl.pallas_call(
        _rope_table_kernel,
        grid=(M // tm,),
        in_specs=[pl.BlockSpec((tm, 1), lambda i: (i, 0)),
                  pl.BlockSpec((1, N), lambda i: (0, 0))],
        out_specs=[pl.BlockSpec((tm, N), lambda i: (i, 0))] * 2,
        out_shape=[jax.ShapeDtypeStruct((M, N), F32)] * 2,
        compiler_params=_cparams(("parallel",)),
        name="rope_tables",
    )(pos_col, inv_row)


def _mla_prep_kernel(z_ref, zs_ref, cos_ref, sin_ref, qg_ref, kvg_ref, wq_ref, wkv_ref,
                     q_ref, k_ref, v_ref):
    nq = MLA_HEADS * MLA_NOPE_DIM
    nr = MLA_HEADS * MLA_ROPE_DIM
    scale = MLA_QK_DIM ** -0.5
    z = z_ref[0]
    cq = _rms(z[:, :MLA_Q_RANK], qg_ref[...]).astype(BF16)
    ckv = _rms(z[:, MLA_Q_RANK:], kvg_ref[...]).astype(BF16)
    cos = cos_ref[0]
    sin = sin_ref[0]
    q_nope = _dot(cq, wq_ref[:, :nq]) * scale
    q_rot = (_dot(cq, wq_ref[:, nq:nq + nr]) * cos + _dot(cq, wq_ref[:, nq + nr:]) * sin) * scale
    k_nope = _dot(ckv, wkv_ref[:, :nq])
    v = _dot(ckv, wkv_ref[:, nq:])
    zs = zs_ref[0]
    r, d = MLA_ROPE_DIM, MLA_NOPE_DIM
    k_rot = zs[:, :r] * cos[:, :r] + zs[:, r:2 * r] * sin[:, :r]
    ones_col = _ones_column(z.shape[0])
    for h in range(MLA_HEADS):
        q_ref[0, h] = jnp.concatenate([q_nope[:, h * d:(h + 1) * d], q_rot[:, h * r:(h + 1) * r]],
                                      axis=-1).astype(BF16)
        k_ref[0, h] = jnp.concatenate([k_nope[:, h * d:(h + 1) * d], k_rot], axis=-1).astype(BF16)
        v_ref[0, h] = jnp.concatenate([v[:, h * MLA_V_DIM:(h + 1) * MLA_V_DIM], ones_col],
                                      axis=-1).astype(BF16)


def _mla_prep(z_mla, z_small, cos, sin, qg, kvg, wq, wkv, tm=512):
    B, S, _ = z_mla.shape
    nr = MLA_HEADS * MLA_ROPE_DIM
    tok = lambda n: pl.BlockSpec((1, tm, n), lambda b, i: (b, i, 0))
    heads = lambda d: pl.BlockSpec((1, MLA_HEADS, tm, d), lambda b, i: (b, 0, i, 0))
    full = lambda a: pl.BlockSpec(a.shape, lambda b, i: (0,) * a.ndim)
    return pl.pallas_call(
        _mla_prep_kernel,
        grid=(B, S // tm),
        in_specs=[tok(z_mla.shape[2]), tok(SMALL_COLS), tok(nr), tok(nr),
                  full(qg), full(kvg), full(wq), full(wkv)],
        out_specs=[heads(MLA_QK_DIM), heads(MLA_QK_DIM), heads(2 * MLA_V_DIM)],
        out_shape=[jax.ShapeDtypeStruct((B, MLA_HEADS, S, d), BF16)
                   for d in (MLA_QK_DIM, MLA_QK_DIM, 2 * MLA_V_DIM)],
        compiler_params=_cparams(("parallel", "parallel")),
        name="mla_prep",
    )(z_mla, z_small, cos, sin, qg, kvg, wq, wkv)


CUM_CHUNK = 512


def _fox_cumsum_kernel(f_ref, b_ref, u_ref, q_ref, k_ref, o_ref, qn_ref, kn_ref):
    x = f_ref[0] + b_ref[...]
    lf = jnp.minimum(x, 0.0) - jnp.log1p(jnp.exp(-jnp.abs(x)))
    S = lf.shape[1]
    carry = jnp.zeros((lf.shape[0], 1), F32)
    for c in range(0, S, CUM_CHUNK):
        loc = jnp.dot(lf[:, c:c + CUM_CHUNK], u_ref[...], preferred_element_type=F32,
                      precision=lax.Precision.HIGHEST) + carry
        o_ref[0, :, c:c + CUM_CHUNK] = loc
        carry = loc[:, CUM_CHUNK - 1:CUM_CHUNK]
    ones = jnp.ones((8, q_ref.shape[3]), BF16)
    for h in range(q_ref.shape[1]):
        for src, dst in ((q_ref, qn_ref), (k_ref, kn_ref)):
            v = src[0, h]
            dst[0, h:h + 1, :] = _dot_nt(ones, v * v)[0:1]


def _fox_cumsum(f_bhs, bias_col, q, k):
    B, H, S = f_bhs.shape
    Dh = q.shape[3]
    assert S % CUM_CHUNK == 0
    upper = jnp.asarray(np.triu(np.ones((CUM_CHUNK, CUM_CHUNK), np.float32)))
    return pl.pallas_call(
        _fox_cumsum_kernel,
        grid=(B,),
        in_specs=[pl.BlockSpec((1, H, S), lambda b: (b, 0, 0)),
                  pl.BlockSpec((H, 1), lambda b: (0, 0)),
                  pl.BlockSpec((CUM_CHUNK, CUM_CHUNK), lambda b: (0, 0)),
                  pl.BlockSpec((1, H, S, Dh), lambda b: (b, 0, 0, 0)),
                  pl.BlockSpec((1, H, S, Dh), lambda b: (b, 0, 0, 0))],
        out_specs=[pl.BlockSpec((1, H, S), lambda b: (b, 0, 0))] * 3,
        out_shape=[jax.ShapeDtypeStruct((B, H, S), F32)] * 3,
        compiler_params=_cparams(("parallel",)),
        name="fox_cumsum",
    )(f_bhs, bias_col, upper, q, k)


def _nsa_compress_kernel(r_ref, pe_ref, w_ref, o_ref):
    r = r_ref[0, 0]
    first = _dot((r + pe_ref[0, 0:1, :]).astype(BF16), w_ref[0, 0])
    second = _dot((r + pe_ref[0, 1:2, :]).astype(BF16), w_ref[0, 1])
    nc = r.shape[0]
    nxt = pltpu.roll(second, nc - 1, 0)
    row = lax.broadcasted_iota(jnp.int32, (nc, 1), 0)
    o_ref[0, 0] = jnp.where(row < nc - 1, first + nxt, 0.0).astype(o_ref.dtype)


def _nsa_compress(rows, pe, w):
    B, F, NC, K = rows.shape
    return pl.pallas_call(
        _nsa_compress_kernel,
        grid=(B, F),
        in_specs=[pl.BlockSpec((1, 1, NC, K), lambda b, f: (b, f, 0, 0)),
                  pl.BlockSpec((1, 2, K), lambda b, f: (f // NSA_KV_HEADS, 0, 0)),
                  pl.BlockSpec((1, 2, K, HEAD_DIM), lambda b, f: (f // NSA_KV_HEADS, 0, 0, 0))],
        out_specs=pl.BlockSpec((1, 1, NC, HEAD_DIM), lambda b, f: (b, f, 0, 0)),
        out_shape=jax.ShapeDtypeStruct((B, F, NC, HEAD_DIM), BF16),
        compiler_params=_cparams(("parallel", "parallel")),
        name="nsa_compress",
    )(rows, pe, w)


NSA_CMP_TQ = 256


def _nsa_cmp_kernel(q_ref, kc_ref, vc_ref, pq_ref, pc_ref, sl_ref, gate_ref, ovt_ref,
                    oc_ref, selt_ref, any_ref):
    G, tq, Dh = q_ref.shape[2:]
    R = G * tq
    i = pl.program_id(2)
    q = q_ref[0, 0].reshape(R, Dh)
    kc = kc_ref[0, 0]
    nc = kc.shape[0]
    n_sel = ovt_ref.shape[0]
    s = _dot_nt(q, kc)
    pq = jnp.concatenate([jnp.broadcast_to(pq_ref[0], (tq, LANES))] * G, axis=0)
    s = s - _rep(sl_ref[0], nc // LANES) * jnp.abs(_rep(pq, nc // LANES) - pc_ref[0])
    t_row = (i * tq + lax.broadcasted_iota(jnp.int32, (G, tq, nc), 1)).reshape(R, nc)
    c_col = lax.broadcasted_iota(jnp.int32, (1, nc), 1)
    c_end = jnp.where(c_col < nc - 1, c_col * NSA_CMP_STRIDE + (NSA_CMP_LEN - 1), jnp.int32(2 ** 30))
    s = jnp.where(c_end <= t_row, s, NEG_INF)
    m = jnp.max(s, axis=-1, keepdims=True)
    e = jnp.exp(s - m)
    l = jnp.sum(e, axis=-1, keepdims=True)
    p = e * jnp.where(m > 0.5 * NEG_INF, 1.0 / l, 0.0)
    o = _dot(p.astype(BF16), vc_ref[0, 0])
    oc_ref[0, 0] = (_gate_column(gate_ref[0, 0], 0, G) * o).reshape(G, tq, Dh)
    p_sum = jnp.sum(p.reshape(G, tq, nc), axis=0)
    imp = _dot_nt(ovt_ref[...], p_sum, precision=lax.Precision.HIGHEST)
    j = lax.broadcasted_iota(jnp.int32, (n_sel, 1), 0)

    def select(imp_c, t0):
        t_q = t0 + lax.broadcasted_iota(jnp.int32, (1, LANES), 1)
        jt = lax.shift_right_logical(t_q, NSA_SEL_LEN.bit_length() - 1)
        forced = (j == 0) | (j == jt) | (j == jt - 1)
        causal = j <= jt
        val = jnp.where(forced, FORCE_SCORE, imp_c)
        val = jnp.where(causal, val, NEG_INF)
        sel = jnp.zeros((n_sel, LANES), jnp.int32)
        for _ in range(min(NSA_TOP_N, n_sel)):
            best = jnp.max(val, axis=0, keepdims=True)
            idx = jnp.min(jnp.where(val == best, j, n_sel), axis=0, keepdims=True)
            pick = j == idx
            sel = jnp.where(pick, 1, sel)
            val = jnp.where(pick, -jnp.inf, val)
        return (sel > 0) & causal

    keep = jnp.concatenate([select(imp[:, c:c + LANES], i * tq + c) for c in range(0, tq, LANES)], axis=1)
    selt_ref[0, 0] = jnp.where(keep, 0.0, NEG_INF).astype(selt_ref.dtype)
    per_tile = any_ref.shape[2]
    any_ref[0, 0] = jnp.max(jnp.where(keep, 1.0, 0.0).reshape(per_tile, n_sel // per_tile, tq), axis=1)


def _nsa_cmp(q, kvc, pos_col, pc_row, slopes, gate, overlap_t, n_tiles):
    B, Hk, G, S, Dh = q.shape
    NC = kvc.shape[2]
    n_sel = overlap_t.shape[0]
    assert NSA_SEL_LEN & (NSA_SEL_LEN - 1) == 0 and n_sel % n_tiles == 0
    tq = NSA_CMP_TQ
    R = G * tq
    return pl.pallas_call(
        _nsa_cmp_kernel,
        grid=(B, Hk, S // tq),
        in_specs=[pl.BlockSpec((1, 1, G, tq, Dh), lambda b, h, i: (b, h, 0, i, 0)),
                  pl.BlockSpec((1, 1, NC, Dh), lambda b, h, i: (b, h, 0, 0)),
                  pl.BlockSpec((1, 1, NC, Dh), lambda b, h, i: (b, h + Hk, 0, 0)),
                  pl.BlockSpec((1, tq, 1), lambda b, h, i: (b, i, 0)),
                  pl.BlockSpec((1, 1, NC), lambda b, h, i: (b, 0, 0)),
                  pl.BlockSpec((1, R, LANES), lambda b, h, i: (h, 0, 0)),
                  pl.BlockSpec((1, 1, tq, 3 * G), lambda b, h, i: (b, h, i, 0)),
                  pl.BlockSpec((n_sel, NC), lambda b, h, i: (0, 0))],
        out_specs=[pl.BlockSpec((1, 1, G, tq, Dh), lambda b, h, i: (b, h, 0, i, 0)),
                   pl.BlockSpec((1, 1, n_sel, tq), lambda b, h, i: (b, h, 0, i)),
                   pl.BlockSpec((1, 1, n_tiles, tq), lambda b, h, i: (b, h, 0, i))],
        out_shape=[jax.ShapeDtypeStruct((B, Hk, G, S, Dh), F32),
                   jax.ShapeDtypeStruct((B, Hk, n_sel, S), BF16),
                   jax.ShapeDtypeStruct((B, Hk, n_tiles, S), F32)],
        compiler_params=_cparams(("parallel", "parallel", "parallel")),
        name="nsa_cmp",
    )(q, kvc, kvc, pos_col, pc_row, slopes, gate, overlap_t)


class AttnCfg(NamedTuple):
    tq: int
    tk: int = 0
    window: int = 0
    alibi: bool = False
    decay: bool = False
    sink: bool = False
    gate: int = -1
    n_sub: int = 1
    add: bool = False
    selaug: bool = False
    sparse: bool = False
    out_dtype: object = BF16


def _gate_column(gates, branch, G):
    cols = [gates[:, 3 * g + branch:3 * g + branch + 1] for g in range(G)]
    return jax.nn.sigmoid(jnp.concatenate(cols, axis=0))


def _epilogue(cfg, out, gates, add, G, tq):
    R, Dv = out.shape
    if cfg.gate >= 0:
        out = out * _gate_column(gates, cfg.gate, G)
    out = out.reshape(G, tq, Dv)
    if cfg.add:
        out = out + add
    return out


def _flash_kernel(*refs, cfg):
    it = iter(refs)
    cnt_ref = tiles_ref = None
    if cfg.sparse:
        cnt_ref, tiles_ref = next(it), next(it)
    q_ref, k_ref, v_ref, dmask_ref = next(it), next(it), next(it), next(it)
    pq_ref = pk_ref = sl_ref = cq_ref = ck_ref = gate_ref = add_ref = None
    selb_ref = next(it) if cfg.selaug else None
    if cfg.alibi:
        pq_ref, pk_ref, sl_ref = next(it), next(it), next(it)
    if cfg.decay:
        cq_ref, ck_ref = next(it), next(it)
    if cfg.gate >= 0:
        gate_ref = next(it)
    if cfg.add:
        add_ref = next(it)
    o_ref = next(it)
    m_s, acc_s, sa_s, sb_s = (next(it) for _ in range(4))
    pq_s = next(it) if cfg.alibi else None
    cq_s = next(it) if cfg.decay else None

    G, tq, Dq = q_ref.shape[2:]
    Da = v_ref.shape[3]
    Dv = Da // 2
    tk = cfg.tk
    nrep = tk // LANES
    R = G * tq
    i = pl.program_id(2)
    q = q_ref[0, 0].reshape(R, Dq)
    if cfg.selaug:
        bias = jnp.concatenate([selb_ref[0, 0].astype(F32)] * G, axis=0)
        q = jnp.concatenate([q.astype(F32), bias], axis=-1).astype(BF16)

    m_s[...] = jnp.full((R, LANES), NEG_INF, F32)
    acc_s[...] = jnp.zeros((R, Da), F32)
    if cfg.alibi:
        pq_s[...] = jnp.broadcast_to(jnp.concatenate([pq_ref[0]] * G, axis=0), (R, LANES))
    if cfg.decay:
        cq_all = cq_ref[0]
        head = lax.broadcasted_iota(jnp.int32, cq_all.shape, 1) == pl.program_id(1)
        cq_s[...] = jnp.broadcast_to(jnp.sum(jnp.where(head, cq_all, 0.0), axis=-1, keepdims=True), (R, LANES))

    def scores(j, s_ref):
        k0 = pl.multiple_of(j * tk, tk)
        s = _dot_nt(q, k_ref[0, 0, pl.ds(k0, tk), :])
        if cfg.alibi:
            s = s - _rep(sl_ref[0], nrep) * jnp.abs(_rep(pq_s[...], nrep) - pk_ref[0, j])
        if cfg.decay:
            s = s + (_rep(cq_s[...], nrep) - ck_ref[0, 0, j])
        s_ref[...] = s

    def update(j, s_ref, mask_variant=None):
        s = s_ref[...]
        if mask_variant is not None:
            s = s + dmask_ref[mask_variant]
        m_prev = m_s[...]
        m_new = jnp.maximum(m_prev, jnp.max(s, axis=-1, keepdims=True))
        alpha = jnp.exp(m_prev - m_new)
        p = jnp.exp((s - _rep(m_new, nrep)).astype(BF16))
        k0 = pl.multiple_of(j * tk, tk)
        acc_s[...] = _rep(alpha, Da // LANES) * acc_s[...] + _dot(p, v_ref[0, 0, pl.ds(k0, tk), :])
        m_s[...] = m_new

    diag = (i * tq) // tk
    n_diag = max(1, tq // tk)
    variants = [i % (tk // tq)] if tq < tk else list(range(n_diag))
    if cfg.sparse:
        step_id = (pl.program_id(0) * pl.num_programs(1) + pl.program_id(1)) * pl.num_programs(2) + i
        n_vis = cnt_ref[step_id]
        nk = k_ref.shape[2] // tk

        def tile_at(pos):
            return jnp.where(pos < n_vis, tiles_ref[step_id * nk + jnp.minimum(pos, nk - 1)], diag)
    else:
        n_vis = diag

        def tile_at(pos):
            return pos

    scores(tile_at(0), sa_s)

    def pair(pp, c):
        pos = 2 * pp
        scores(tile_at(pos + 1), sb_s)
        update(tile_at(pos), sa_s)
        scores(tile_at(pos + 2), sa_s)
        update(tile_at(pos + 1), sb_s)
        return c
    lax.fori_loop(0, n_vis // 2, pair, 0)

    def tail(leftover):
        items = ([(tile_at(n_vis - 1), None)] if leftover else []) + \
                [(diag + d, variants[d]) for d in range(n_diag)]
        bufs = (sa_s, sb_s)
        for k, (tile, variant) in enumerate(items):
            if k + 1 < len(items):
                scores(items[k + 1][0], bufs[(k + 1) % 2])
            update(tile, bufs[k % 2], variant)

    @pl.when(n_vis % 2 == 1)
    def _():
        tail(True)

    @pl.when(n_vis % 2 == 0)
    def _():
        tail(False)

    acc = acc_s[...]
    out = acc[:, :Dv] / acc[:, Dv:Dv + 1]
    out = _epilogue(cfg, out, gate_ref[0, 0] if cfg.gate >= 0 else None, add_ref[0, 0] if cfg.add else None,
                    G, tq)
    o_ref[0, 0] = out.astype(o_ref.dtype)


def _flash(cfg, q, k, v, *, visit=None, selb=None, pos_col=None, pos_rows=None, slopes=None,
           cum_col=None, cum_rows=None, gate=None, add=None):
    B, Hk, G, S, Dq = q.shape
    Dk, Da = k.shape[3], v.shape[3]
    Dv = Da // 2
    tq, tk = cfg.tq, cfg.tk
    assert S % tq == 0 and S % tk == 0 and (tk % tq == 0 or tq % tk == 0) and tk % LANES == 0
    assert Da % LANES == 0
    assert Dk == Dq + (selb.shape[3] if cfg.selaug else 0) and cfg.sparse == (visit is not None)
    R = G * tq
    nk = S // tk
    qmap = lambda b, h, i, *_: (b, h, 0, i, 0)
    kvmap = lambda b, h, i, *_: (b, h, 0, 0)
    t_local = np.tile(np.arange(tq), G)[None, :, None]
    first = (np.arange(tk // tq) * tq if tq < tk else -np.arange(tq // tk) * tk)[:, None, None]
    dmask = jnp.asarray(np.where(np.arange(tk)[None, None, :] <= first + t_local, 0.0, NEG_INF), dtype=F32)
    args = [q, k, v, dmask]
    specs = [pl.BlockSpec((1, 1, G, tq, Dq), qmap),
             pl.BlockSpec((1, 1, S, Dk), kvmap),
             pl.BlockSpec((1, 1, S, Da), kvmap),
             pl.BlockSpec(dmask.shape, lambda b, h, i, *_: (0, 0, 0), pipeline_mode=pl.Buffered(1))]
    scratch = [pltpu.VMEM((R, LANES), F32), pltpu.VMEM((R, Da), F32),
               pltpu.VMEM((R, tk), F32), pltpu.VMEM((R, tk), F32)]
    if cfg.selaug:
        args += [selb]
        specs += [pl.BlockSpec((1, 1, tq, Dk - Dq), lambda b, h, i, *_: (b, h, i, 0))]
    if cfg.alibi:
        args += [pos_col, pos_rows, slopes]
        specs += [pl.BlockSpec((1, tq, 1), lambda b, h, i, *_: (b, i, 0)),
                  pl.BlockSpec((1, nk, 1, tk), lambda b, h, i, *_: (b, 0, 0, 0)),
                  pl.BlockSpec((1, R, LANES), lambda b, h, i, *_: (h, 0, 0))]
        scratch.append(pltpu.VMEM((R, LANES), F32))
    if cfg.decay:
        args += [cum_col, cum_rows]
        assert G == 1
        specs += [pl.BlockSpec((1, tq, Hk), lambda b, h, i, *_: (b, i, 0)),
                  pl.BlockSpec((1, 1, nk, 1, tk), lambda b, h, i, *_: (b, h, 0, 0, 0))]
        scratch.append(pltpu.VMEM((R, LANES), F32))
    if cfg.gate >= 0:
        args += [gate]
        specs += [pl.BlockSpec((1, 1, tq, 3 * G), lambda b, h, i, *_: (b, h, i, 0))]
    if cfg.add:
        args += [add]
        specs += [pl.BlockSpec((1, 1, G, tq, Dv), qmap)]
    grid_spec = pltpu.PrefetchScalarGridSpec(
        num_scalar_prefetch=2 if cfg.sparse else 0,
        grid=(B, Hk, S // tq),
        in_specs=specs,
        out_specs=pl.BlockSpec((1, 1, G, tq, Dv), qmap),
        scratch_shapes=scratch)
    return pl.pallas_call(
        functools.partial(_flash_kernel, cfg=cfg),
        grid_spec=grid_spec,
        out_shape=jax.ShapeDtypeStruct((B, Hk, G, S, Dv), cfg.out_dtype),
        compiler_params=_cparams(("parallel", "parallel", "parallel")),
        name="flash",
    )(*(list(visit) if cfg.sparse else []), *args)


def _band_kernel(*refs, cfg):
    it = iter(refs)
    q_ref, k_ref, v_ref, pq_ref, pk_ref, sl_ref = (next(it) for _ in range(6))
    sink_ref = next(it) if cfg.sink else None
    gate_ref = next(it) if cfg.gate >= 0 else None
    add_ref = next(it) if cfg.add else None
    o_ref = next(it)

    G, Dq = q_ref.shape[2], q_ref.shape[4]
    tq = cfg.tq
    R = G * tq
    W = cfg.window
    span = W + tq
    slope = sl_ref[0]
    d = (lax.broadcasted_iota(jnp.int32, (G, tq, LANES), 1)
         - lax.broadcasted_iota(jnp.int32, (G, tq, LANES), 2)).reshape(R, LANES)
    for sub in range(cfg.n_sub):
        i = pl.program_id(2) * cfg.n_sub + sub
        tok = slice(sub * tq, (sub + 1) * tq)
        start = pl.multiple_of(jnp.maximum(i * tq - W, 0), LANES)
        blk0 = lax.shift_right_logical(start, LANES.bit_length() - 1)
        q = q_ref[0, 0, :, tok, :].reshape(R, Dq)
        s = _dot_nt(q, k_ref[0, 0, pl.ds(start, span), :])
        pq = jnp.concatenate([pq_ref[0, tok, :]] * G, axis=0)
        off = i * tq - start
        chunks = []
        for c in range(span // LANES):
            sc = s[:, c * LANES:(c + 1) * LANES]
            sc = sc - slope * jnp.abs(pq - pk_ref[0, blk0 + c])
            lo = c * LANES - off
            sc = jnp.where(d >= lo, jnp.where(d < lo + W, sc, NEG_INF), NEG_INF)
            chunks.append(sc)
        s = jnp.concatenate(chunks, axis=1)
        m = jnp.max(s, axis=-1, keepdims=True)
        e = jnp.exp(s - m)
        l = jnp.sum(e, axis=-1, keepdims=True)
        acc = _dot(e.astype(BF16), v_ref[0, 0, pl.ds(start, span), :])
        if cfg.sink:
            sink = sink_ref[0]
            m_f = jnp.maximum(m, sink)
            w = jnp.exp(m - m_f)
            out = acc * (w / (l * w + jnp.exp(sink - m_f)))
        else:
            out = acc / l
        out = _epilogue(cfg, out, gate_ref[0, 0, tok, :] if cfg.gate >= 0 else None,
                        add_ref[0, 0, :, tok, :] if cfg.add else None, G, tq)
        o_ref[0, 0, :, tok, :] = out.astype(o_ref.dtype)


def _band(cfg, q, k, v, *, pos_col, pos_rows, slopes, sinks=None, gate=None, add=None):
    B, Hk, G, S, Dq = q.shape
    Dv = v.shape[3]
    tq = cfg.tq
    T = tq * cfg.n_sub
    assert tq == LANES and cfg.window % LANES == 0 and S % T == 0 and S >= cfg.window + tq
    R = G * tq
    qmap = lambda b, h, i: (b, h, 0, i, 0)
    kvmap = lambda b, h, i: (b, h, 0, 0)
    args = [q, k, v, pos_col, pos_rows, slopes]
    specs = [pl.BlockSpec((1, 1, G, T, Dq), qmap),
             pl.BlockSpec((1, 1, S, Dq), kvmap),
             pl.BlockSpec((1, 1, S, Dv), kvmap),
             pl.BlockSpec((1, T, 1), lambda b, h, i: (b, i, 0)),
             pl.BlockSpec((1, S // LANES, 1, LANES), lambda b, h, i: (b, 0, 0, 0)),
             pl.BlockSpec((1, R, 1), lambda b, h, i: (h, 0, 0))]
    if cfg.sink:
        args += [sinks]
        specs += [pl.BlockSpec((1, R, 1), lambda b, h, i: (h, 0, 0))]
    if cfg.gate >= 0:
        args += [gate]
        specs += [pl.BlockSpec((1, 1, T, 3 * G), lambda b, h, i: (b, h, i, 0))]
    if cfg.add:
        args += [add]
        specs += [pl.BlockSpec((1, 1, G, T, Dv), qmap)]
    return pl.pallas_call(
        functools.partial(_band_kernel, cfg=cfg),
        grid=(B, Hk, S // T),
        in_specs=specs,
        out_specs=pl.BlockSpec((1, 1, G, T, Dv), qmap),
        out_shape=jax.ShapeDtypeStruct((B, Hk, G, S, Dv), cfg.out_dtype),
        compiler_params=_cparams(("parallel", "parallel", "parallel")),
        name="band",
    )(*args)


def _merge_kernel(x_ref, y0_ref, y1_ref, y2_ref, y3_ref, gpre_ref, wg_ref, wb_ref, wo_ref, gpost_ref,
                  o_ref):
    x = x_ref[0]
    h = _rms(x, gpre_ref[...]).astype(BF16)
    merged = None
    for b, y_ref in enumerate((y0_ref, y1_ref, y2_ref, y3_ref)):
        gate = jax.nn.sigmoid(_dot(h, wg_ref[b]))
        y = jnp.concatenate([y_ref[0, hd] for hd in range(y_ref.shape[1])], axis=-1)
        br = _dot(y, wb_ref[b])
        merged = gate * br if merged is None else merged + gate * br
    y = _dot(merged.astype(BF16), wo_ref[...])
    o_ref[0] = x + _rms(y, gpost_ref[...])


def _merge(x, ys, gpre, wg, wb, wo, gpost, tm=512):
    B, S, D = x.shape
    full = lambda a: pl.BlockSpec(a.shape, lambda b, i: (0,) * a.ndim, pipeline_mode=pl.Buffered(1))
    yspec = lambda y: pl.BlockSpec((1, y.shape[1], tm, y.shape[3]), lambda b, i: (b, 0, i, 0))
    return pl.pallas_call(
        _merge_kernel,
        grid=(B, S // tm),
        in_specs=[pl.BlockSpec((1, tm, D), lambda b, i: (b, i, 0))] + [yspec(y) for y in ys]
                 + [full(gpre), full(wg), full(wb), full(wo), full(gpost)],
        out_specs=pl.BlockSpec((1, tm, D), lambda b, i: (b, i, 0)),
        out_shape=jax.ShapeDtypeStruct((B, S, D), F32),
        compiler_params=_cparams(("parallel", "parallel")),
        name="merge",
    )(x, *ys, gpre, wg, wb, wo, gpost)


def _mem_kv_kernel(mem_ref, g_ref, w_ref, o_ref):
    o_ref[0] = _dot(_rms(mem_ref[0], g_ref[...]).astype(BF16), w_ref[...]).astype(BF16)


def _mem_kv(mem, g, w):
    B, Mm, D = mem.shape
    N = w.shape[1]
    return pl.pallas_call(
        _mem_kv_kernel,
        grid=(B,),
        in_specs=[pl.BlockSpec((1, Mm, D), lambda b: (b, 0, 0)),
                  pl.BlockSpec((1, D), lambda b: (0, 0)),
                  pl.BlockSpec((D, N), lambda b: (0, 0))],
        out_specs=pl.BlockSpec((1, Mm, N), lambda b: (b, 0, 0)),
        out_shape=jax.ShapeDtypeStruct((B, Mm, N), BF16),
        compiler_params=_cparams(("parallel",)),
        name="mem_kv",
    )(mem, g, w)


def _xattn_kernel(x_ref, kv_ref, gpre_ref, wq_ref, wo_ref, gpost_ref, o_ref):
    x = x_ref[0]
    h = _rms(x, gpre_ref[...]).astype(BF16)
    q_all = _dot(h, wq_ref[...])
    outs = []
    for hd in range(XA_HEADS):
        q = q_all[:, hd * HEAD_DIM:(hd + 1) * HEAD_DIM].astype(BF16)
        s = _dot_nt(q, kv_ref[0, hd])
        m = jnp.max(s, axis=-1, keepdims=True)
        e = jnp.exp(s - m)
        p = e / jnp.sum(e, axis=-1, keepdims=True)
        outs.append(_dot(p.astype(BF16), kv_ref[0, XA_HEADS + hd]))
    y = _dot(jnp.concatenate(outs, axis=-1).astype(BF16), wo_ref[...])
    o_ref[0] = x + _rms(y, gpost_ref[...])


def _xattn(x, kv, gpre, wq, wo, gpost, tm=512):
    B, S, D = x.shape
    full = lambda a: pl.BlockSpec(a.shape, lambda b, i: (0,) * a.ndim)
    return pl.pallas_call(
        _xattn_kernel,
        grid=(B, S // tm),
        in_specs=[pl.BlockSpec((1, tm, D), lambda b, i: (b, i, 0)),
                  pl.BlockSpec((1,) + kv.shape[1:], lambda b, i: (b, 0, 0, 0)),
                  full(gpre), full(wq), full(wo), full(gpost)],
        out_specs=pl.BlockSpec((1, tm, D), lambda b, i: (b, i, 0)),
        out_shape=jax.ShapeDtypeStruct((B, S, D), F32),
        compiler_params=_cparams(("parallel", "parallel")),
        name="xattn",
    )(x, kv, gpre, wq, wo, gpost)


FFN_CHUNK = 256


def _ffn_kernel(x_ref, gpre_ref, wg_ref, wu_ref, wo_ref, gpost_ref, o_ref):
    x = x_ref[...]
    h = _rms(x, gpre_ref[...]).astype(BF16)
    y = None
    for c in range(0, D_FF, FFN_CHUNK):
        g = _dot(h, wg_ref[:, c:c + FFN_CHUNK])
        u = _dot(h, wu_ref[:, c:c + FFN_CHUNK])
        a = (g * jax.nn.sigmoid(g) * u).astype(BF16)
        part = _dot(a, wo_ref[c:c + FFN_CHUNK, :])
        y = part if y is None else y + part
    o_ref[...] = x + _rms(y, gpost_ref[...])


def _ffn(x2d, gpre, wg, wu, wo, gpost, tm=512):
    M, D = x2d.shape
    full = lambda a: pl.BlockSpec(a.shape, lambda i: (0,) * a.ndim, pipeline_mode=pl.Buffered(1))
    return pl.pallas_call(
        _ffn_kernel,
        grid=(M // tm,),
        in_specs=[pl.BlockSpec((tm, D), lambda i: (i, 0)),
                  full(gpre), full(wg), full(wu), full(wo), full(gpost)],
        out_specs=pl.BlockSpec((tm, D), lambda i: (i, 0)),
        out_shape=jax.ShapeDtypeStruct((M, D), F32),
        compiler_params=_cparams(("parallel",)),
        name="ffn",
    )(x2d, gpre, wg, wu, wo, gpost)


def _alibi_slopes(n):
    return (2.0 ** (-8.0 * np.arange(1, n + 1) / n)).astype(np.float32)


def _rotate_half_cols(w):
    half = w.shape[-1] // 2
    return jnp.concatenate([-w[..., half:], w[..., :half]], axis=-1)


def _prep_weights(p):
    L = p["w_in"].shape[0]
    offs = np.concatenate([[0], np.cumsum(IN_SPLITS)])
    col = {n: p["w_in"][:, :, offs[k]:offs[k + 1]] for k, n in enumerate(IN_NAMES)}
    qk_scale = HEAD_DIM ** -0.5
    for n in ("nq", "sq", "fq"):
        col[n] = col[n] * qk_scale
    pad = jnp.zeros((L, D_MODEL, SMALL_COLS - 2 * MLA_ROPE_DIM - 24 - FOX_HEADS), F32)
    small = jnp.concatenate([col["mkr"], _rotate_half_cols(col["mkr"]), col["ng"], col["ff"], pad], -1)
    w_all = jnp.concatenate([col["mcq"], col["mckv"], col["nkc"], col["nvc"], small]
                            + [col[n] for n in MAIN_ORDER], axis=-1).astype(BF16)

    wq = p["mla_w_q_up"].reshape(L, MLA_Q_RANK, MLA_HEADS, MLA_QK_DIM)
    wq_nope = wq[..., :MLA_NOPE_DIM].reshape(L, MLA_Q_RANK, -1)
    wq_rot = wq[..., MLA_NOPE_DIM:]
    wq_all = jnp.concatenate([wq_nope, wq_rot.reshape(L, MLA_Q_RANK, -1),
                              _rotate_half_cols(wq_rot).reshape(L, MLA_Q_RANK, -1)], -1).astype(BF16)
    wkv = p["mla_w_kv_up"].reshape(L, MLA_KV_RANK, MLA_HEADS, MLA_NOPE_DIM + MLA_V_DIM)
    wkv_all = jnp.concatenate([wkv[..., :MLA_NOPE_DIM].reshape(L, MLA_KV_RANK, -1),
                               wkv[..., MLA_NOPE_DIM:].reshape(L, MLA_KV_RANK, -1)], -1).astype(BF16)

    half = NSA_CMP_LEN // 2
    cmp_w = p["nsa_cmp_w"].reshape(L, 2, 2, half * HEAD_DIM, HEAD_DIM).astype(BF16)
    cmp_pe = p["nsa_cmp_pe"].reshape(L, 2, 2, half * HEAD_DIM)

    xa_wq = (p["xa_w_q"] * qk_scale).astype(BF16)
    xa_wo = p["xa_w_o"].astype(BF16)

    row = lambda a: a[:, None, :]
    return dict(
        w_all=w_all, mix_norm_pre=row(p["mix_norm_pre"]),
        wq_all=wq_all, wkv_all=wkv_all, mla_q_norm=row(p["mla_q_norm"]), mla_kv_norm=row(p["mla_kv_norm"]),
        cmp_w=cmp_w, cmp_pe=cmp_pe, swa_sinks=p["swa_sinks"], fox_f_bias=p["fox_f_bias"][:, :, None],
        w_gate=p["w_gate"].astype(BF16), w_branch=p["w_branch"].astype(BF16), w_out=p["w_out"].astype(BF16),
        mix_norm_post=row(p["mix_norm_post"]),
        xa_norm_pre=row(p["xa_norm_pre"]), xa_mem_norm=row(p["xa_mem_norm"]), xa_wq=xa_wq,
        xa_wkv=p["xa_w_kv"].astype(BF16), xa_wo=xa_wo, xa_norm_post=row(p["xa_norm_post"]),
        ffn_norm_pre=row(p["ffn_norm_pre"]), ffn_wg=p["ffn_w_in"][:, :, :D_FF].astype(BF16),
        ffn_wu=p["ffn_w_in"][:, :, D_FF:].astype(BF16), ffn_wo=p["ffn_w_out"].astype(BF16),
        ffn_norm_post=row(p["ffn_norm_post"]),
    )


FLASH_TQ = 512
FLASH_TK = 512
BAND_TQ = LANES
BAND_SUB = 4


def _per_row(vals_hg, tq, lanes=1):
    Hk, G = vals_hg.shape
    return jnp.broadcast_to(vals_hg[:, :, None, None], (Hk, G, tq, lanes)).reshape(Hk, G * tq, lanes)


def _visit_lists(needed, tq, tk):
    nq, nk = needed.shape[2:]
    diag = (np.arange(nq)[:, None] * tq) // tk
    needed = needed & jnp.asarray(np.arange(nk)[None, :] < diag)
    count = jnp.sum(needed, axis=-1).astype(jnp.int32)
    tiles = jnp.argsort(~needed, axis=-1, stable=True).astype(jnp.int32)
    return count.reshape(-1), tiles.reshape(-1)


ZERO_WEIGHT_LOGIT = -110.0


def _fox_tiles_needed(q_sq, k_sq, cum, tq, tk):
    B, H, S = cum.shape
    qn = jnp.sqrt(q_sq)
    kmax = jnp.sqrt(jnp.max(k_sq, axis=-1))
    row_bound = cum + 2.02 * qn * kmax[..., None]
    tile_bound = jnp.max(row_bound.reshape(B, H, S // tq, tq), axis=-1)
    cum_last = cum[:, :, tk - 1::tk]
    return tile_bound[..., None] - cum_last[:, :, None, :] > ZERO_WEIGHT_LOGIT


def _layer(x, w, c):
    B, S, D = x.shape
    M = B * S
    Hk, G, Dh = NSA_KV_HEADS, NSA_GROUP, HEAD_DIM

    z_mla, z_small, z_cmp, *heads = _inproj(x, w["mix_norm_pre"], w["w_all"])
    hm = dict(zip(MAIN_ORDER, heads))
    grouped = lambda t: t.reshape(B, Hk, G, S, t.shape[-1])
    single = lambda t: t[:, :, None]
    flat = lambda o: o.reshape(B, -1, S, o.shape[-1])

    ng = z_small[:, :, 64:88].reshape(B, S, Hk, 3 * G).transpose(0, 2, 1, 3)
    nq = grouped(hm["nq"])
    rows = z_cmp.reshape(B, 2 * Hk, S // NSA_CMP_STRIDE, NSA_CMP_STRIDE * Dh)
    kvc = _nsa_compress(rows, w["cmp_pe"], w["cmp_w"])
    nk = S // FLASH_TK
    oc, selt, tile_any = _nsa_cmp(nq, kvc, c["pos_col"], c["pc_row"], c["nsa_slopes_cmp"], ng,
                                  c["overlap_t"], nk)
    selb = selt.transpose(0, 1, 3, 2)
    sel_tq = FLASH_TQ // G
    used = tile_any.reshape(B, Hk, nk, S // sel_tq, sel_tq).max(-1).transpose(0, 1, 3, 2) > 0.0
    o_sel = _flash(
        AttnCfg(tq=sel_tq, tk=FLASH_TK, alibi=True, gate=1, add=True, selaug=True, sparse=True,
                out_dtype=F32),
        nq, hm["nks"], hm["nvs"], visit=_visit_lists(used, sel_tq, FLASH_TK), selb=selb,
        pos_col=c["pos_col"], pos_rows=c["pos_rows_flash"], slopes=c["nsa_slopes_flash"], gate=ng, add=oc)
    y_nsa = _band(
        AttnCfg(tq=BAND_TQ, n_sub=BAND_SUB, window=NSA_WINDOW, alibi=True, gate=2, add=True),
        nq, hm["nkw"], hm["nvw"], pos_col=c["pos_col"], pos_rows=c["pos_rows_band"],
        slopes=c["nsa_slopes_band"], gate=ng, add=o_sel)

    y_swa = _band(
        AttnCfg(tq=BAND_TQ, n_sub=BAND_SUB, window=SWA_WINDOW, alibi=True, sink=True),
        grouped(hm["sq"]), hm["sk"], hm["sv"], pos_col=c["pos_col"],
        pos_rows=c["pos_rows_band"], slopes=c["swa_slopes_band"],
        sinks=_per_row(w["swa_sinks"].reshape(SWA_KV_HEADS, SWA_GROUP), BAND_TQ))

    q_mla, k_mla, v_mla = _mla_prep(z_mla, z_small, c["cos"], c["sin"], w["mla_q_norm"], w["mla_kv_norm"],
                                    w["wq_all"], w["wkv_all"])
    y_mla = _flash(AttnCfg(tq=FLASH_TQ, tk=FLASH_TK), single(q_mla), k_mla, v_mla)

    f_logit = z_small[:, :, 88:96].transpose(0, 2, 1)
    cum, fq_sq, fk_sq = _fox_cumsum(f_logit, w["fox_f_bias"], hm["fq"], hm["fk"])
    y_fox = _flash(
        AttnCfg(tq=FLASH_TQ, tk=FLASH_TK, decay=True, sparse=True), single(hm["fq"]), hm["fk"], hm["fv"],
        visit=_visit_lists(_fox_tiles_needed(fq_sq, fk_sq, cum, FLASH_TQ, FLASH_TK), FLASH_TQ, FLASH_TK),
        cum_col=cum.transpose(0, 2, 1), cum_rows=cum.reshape(B, FOX_HEADS, S // FLASH_TK, 1, FLASH_TK))

    x1 = _merge(x, [flat(y_nsa), flat(y_swa), flat(y_mla), flat(y_fox)], w["mix_norm_pre"], w["w_gate"],
                w["w_branch"], w["w_out"], w["mix_norm_post"])

    kv = _mem_kv(c["mem"], w["xa_mem_norm"], w["xa_wkv"])
    kv = kv.reshape(B, -1, 2 * XA_HEADS, Dh).transpose(0, 2, 1, 3)
    x2 = _xattn(x1, kv, w["xa_norm_pre"], w["xa_wq"], w["xa_wo"], w["xa_norm_post"])

    x3 = _ffn(x2.reshape(M, D), w["ffn_norm_pre"], w["ffn_wg"], w["ffn_wu"], w["ffn_wo"], w["ffn_norm_post"])
    return x3.reshape(B, S, D)


def _constants(mem, positions, S):
    B = positions.shape[0]
    pos = positions.astype(F32)
    half = MLA_ROPE_DIM // 2
    inv = (ROPE_THETA ** (-np.arange(half, dtype=np.float32) / half)).astype(np.float32)
    inv_row = jnp.asarray(np.tile(inv, 2 * MLA_HEADS)[None, :])
    cos, sin = _rope_tables(pos.reshape(B * S, 1), inv_row)
    nc = S // NSA_CMP_STRIDE
    n_valid = (S - NSA_CMP_LEN) // NSA_CMP_STRIDE + 1
    assert n_valid == nc - 1
    c_end = np.minimum(np.arange(nc) * NSA_CMP_STRIDE + NSA_CMP_LEN - 1, S - 1)
    n_sel = S // NSA_SEL_LEN
    c_start = np.arange(nc) * NSA_CMP_STRIDE
    sel_start = np.arange(n_sel) * NSA_SEL_LEN
    overlap = ((c_start[:, None] < sel_start[None, :] + NSA_SEL_LEN)
               & (c_start[:, None] + NSA_CMP_LEN - 1 >= sel_start[None, :])
               & (np.arange(nc)[:, None] < n_valid)).astype(np.float32)
    nsa_slopes = jnp.asarray(_alibi_slopes(NSA_HEADS).reshape(NSA_KV_HEADS, NSA_GROUP))
    swa_slopes = jnp.asarray(_alibi_slopes(SWA_HEADS).reshape(SWA_KV_HEADS, SWA_GROUP))
    return dict(
        mem=mem, cos=cos.reshape(B, S, -1), sin=sin.reshape(B, S, -1),
        pos_col=pos[:, :, None],
        pos_rows_flash=pos.reshape(B, S // FLASH_TK, 1, FLASH_TK),
        pos_rows_band=pos.reshape(B, S // LANES, 1, LANES),
        pc_row=pos[:, c_end][:, None, :],
        overlap_t=jnp.asarray(overlap.T),
        nsa_slopes_cmp=_per_row(nsa_slopes, NSA_CMP_TQ, LANES),
        nsa_slopes_flash=_per_row(nsa_slopes, FLASH_TQ // NSA_GROUP, LANES),
        nsa_slopes_band=_per_row(nsa_slopes, BAND_TQ),
        swa_slopes_band=_per_row(swa_slopes, BAND_TQ),
    )


def kernel(x, mem, positions, mix_norm_pre, w_in, nsa_cmp_w, nsa_cmp_pe, swa_sinks, mla_q_norm, mla_w_q_up,
           mla_kv_norm, mla_w_kv_up, fox_f_bias, w_branch, w_gate, w_out, mix_norm_post, xa_norm_pre,
           xa_mem_norm, xa_w_q, xa_w_kv, xa_w_o, xa_norm_post, ffn_norm_pre, ffn_w_in, ffn_w_out,
           ffn_norm_post):
    params = dict(
        mix_norm_pre=mix_norm_pre, w_in=w_in, nsa_cmp_w=nsa_cmp_w, nsa_cmp_pe=nsa_cmp_pe, swa_sinks=swa_sinks,
        mla_q_norm=mla_q_norm, mla_w_q_up=mla_w_q_up, mla_kv_norm=mla_kv_norm, mla_w_kv_up=mla_w_kv_up,
        fox_f_bias=fox_f_bias, w_branch=w_branch, w_gate=w_gate, w_out=w_out, mix_norm_post=mix_norm_post,
        xa_norm_pre=xa_norm_pre, xa_mem_norm=xa_mem_norm, xa_w_q=xa_w_q, xa_w_kv=xa_w_kv, xa_w_o=xa_w_o,
        xa_norm_post=xa_norm_post, ffn_norm_pre=ffn_norm_pre, ffn_w_in=ffn_w_in, ffn_w_out=ffn_w_out,
        ffn_norm_post=ffn_norm_post)
    S = x.shape[1]
    weights = _prep_weights(params)
    consts = _constants(mem, positions, S)

    def body(xc, w):
        return _layer(xc, w, consts), None

    out, _ = lax.scan(body, x, weights)
    return out
```

```python
import functools
from typing import NamedTuple

import numpy as np
import jax
import jax.numpy as jnp
from jax import lax
from jax.experimental import pallas as pl
from jax.experimental.pallas import tpu as pltpu

F32 = jnp.float32
BF16 = jnp.bfloat16

D_MODEL = 1024
HEAD_DIM = 64
NORM_EPS = 1e-6
NEG_INF = -1e30
FORCE_SCORE = 1e9
LANES = 128

NSA_HEADS = 8
NSA_KV_HEADS = 2
NSA_GROUP = NSA_HEADS // NSA_KV_HEADS
NSA_CMP_LEN = 32
NSA_CMP_STRIDE = 16
NSA_SEL_LEN = 64
NSA_TOP_N = 8
NSA_WINDOW = 512

SWA_HEADS = 8
SWA_KV_HEADS = 2
SWA_GROUP = SWA_HEADS // SWA_KV_HEADS
SWA_WINDOW = 128

MLA_HEADS = 8
MLA_Q_RANK = 256
MLA_KV_RANK = 128
MLA_NOPE_DIM = 64
MLA_ROPE_DIM = 32
MLA_V_DIM = 64
MLA_QK_DIM = MLA_NOPE_DIM + MLA_ROPE_DIM
ROPE_THETA = 10000.0

FOX_HEADS = 8
XA_HEADS = 4
N_BRANCH = 4
BRANCH_WIDTH = 512
D_FF = ((8 * D_MODEL + 3 * 256 - 1) // (3 * 256)) * 256

IN_SPLITS = (512, 128, 128, 128, 128, 128, 128, 24,
             512, 128, 128,
             MLA_Q_RANK, MLA_KV_RANK, MLA_ROPE_DIM,
             512, 512, 512, FOX_HEADS)
IN_NAMES = ("nq", "nkc", "nvc", "nks", "nvs", "nkw", "nvw", "ng",
            "sq", "sk", "sv", "mcq", "mckv", "mkr", "fq", "fk", "fv", "ff")

VMEM_LIMIT_BYTES = 56 * 1024 * 1024

MAIN_ORDER = ("nq", "nks", "nvs", "nkw", "nvw", "sq", "sk", "sv", "fq", "fk", "fv")
MAIN_COLS = 3328
SMALL_COLS = 128


def _cparams(sem):
    return pltpu.CompilerParams(dimension_semantics=sem, vmem_limit_bytes=VMEM_LIMIT_BYTES)


def _rms(x, g):
    return x * lax.rsqrt(jnp.mean(x * x, axis=-1, keepdims=True) + NORM_EPS) * g


def _dot(a, b):
    return jnp.dot(a, b, preferred_element_type=F32)


def _dot_nt(a, b, precision=None):
    return lax.dot_general(a, b, (((1,), (1,)), ((), ())), preferred_element_type=F32,
                           precision=precision)


def _rep(x, n):
    return x if n == 1 else jnp.concatenate([x] * n, axis=1)


F32_COLS = MLA_Q_RANK + MLA_KV_RANK + 4 * HEAD_DIM + SMALL_COLS
MAIN_HEADS = tuple(8 if n in ("nq", "sq", "fq", "fk", "fv") else 2 for n in MAIN_ORDER)
PROJ_CHUNK = 512
FLASH_V_NAMES = ("nvs", "fv")


def _ones_column(rows):
    return jnp.where(lax.broadcasted_iota(jnp.int32, (rows, HEAD_DIM), 1) == 0, 1.0, 0.0)


def _inproj_kernel(x_ref, g_ref, w_ref, zmla_ref, zsmall_ref, cmp_ref, *head_refs):
    tm = x_ref.shape[1]
    h = _rms(x_ref[0], g_ref[...]).astype(BF16)
    n_mla = MLA_Q_RANK + MLA_KV_RANK
    zmla_ref[0] = _dot(h, w_ref[:, :n_mla])
    r = _dot(h, w_ref[:, n_mla:F32_COLS])
    for f in range(4):
        cmp_ref[0, f] = r[:, f * HEAD_DIM:(f + 1) * HEAD_DIM]
    zsmall_ref[0] = r[:, 4 * HEAD_DIM:]

    n_sel = head_refs[MAIN_ORDER.index("nks")].shape[3] - HEAD_DIM
    t = pl.program_id(1) * tm + lax.broadcasted_iota(jnp.int32, (tm, n_sel), 0)
    blk = lax.shift_right_logical(t, NSA_SEL_LEN.bit_length() - 1)
    onehot = jnp.where(blk == lax.broadcasted_iota(jnp.int32, (tm, n_sel), 1), 1.0, 0.0)

    ones_col = _ones_column(tm)
    extra = {"nks": onehot, **{n: ones_col for n in FLASH_V_NAMES}}
    dest = [(ref, hd, extra.get(name)) for ref, nh, name in zip(head_refs, MAIN_HEADS, MAIN_ORDER)
            for hd in range(nh)]
    for c in range(0, MAIN_COLS, PROJ_CHUNK):
        w = min(PROJ_CHUNK, MAIN_COLS - c)
        res = _dot(h, w_ref[:, F32_COLS + c:F32_COLS + c + w])
        for k in range(w // HEAD_DIM):
            ref, hd, appended = dest[c // HEAD_DIM + k]
            piece = res[:, k * HEAD_DIM:(k + 1) * HEAD_DIM]
            if appended is not None:
                piece = jnp.concatenate([piece, appended], axis=-1)
            ref[0, hd] = piece.astype(ref.dtype)


def _inproj(x, g, w, tm=512):
    B, S, D = x.shape
    N = w.shape[1]
    n_sel = S // NSA_SEL_LEN
    assert N == F32_COLS + MAIN_COLS and S % tm == 0
    tok = lambda n: pl.BlockSpec((1, tm, n), lambda b, i: (b, i, 0))
    heads = lambda nh, d: pl.BlockSpec((1, nh, tm, d), lambda b, i: (b, 0, i, 0))
    head_dims = [HEAD_DIM + n_sel if n == "nks" else 2 * HEAD_DIM if n in FLASH_V_NAMES else HEAD_DIM
                 for n in MAIN_ORDER]
    return pl.pallas_call(
        _inproj_kernel,
        grid=(B, S // tm),
        in_specs=[pl.BlockSpec((1, tm, D), lambda b, i: (b, i, 0)),
                  pl.BlockSpec((1, D), lambda b, i: (0, 0)),
                  pl.BlockSpec((D, N), lambda b, i: (0, 0))],
        out_specs=[tok(MLA_Q_RANK + MLA_KV_RANK), tok(SMALL_COLS), heads(4, HEAD_DIM)]
                  + [heads(nh, d) for nh, d in zip(MAIN_HEADS, head_dims)],
        out_shape=[jax.ShapeDtypeStruct((B, S, MLA_Q_RANK + MLA_KV_RANK), F32),
                   jax.ShapeDtypeStruct((B, S, SMALL_COLS), F32),
                   jax.ShapeDtypeStruct((B, 4, S, HEAD_DIM), F32)]
                  + [jax.ShapeDtypeStruct((B, nh, S, d), BF16) for nh, d in zip(MAIN_HEADS, head_dims)],
        compiler_params=_cparams(("parallel", "parallel")),
        name="inproj",
    )(x, g, w)


def _rope_table_kernel(pos_ref, inv_ref, cos_ref, sin_ref):
    ang = pos_ref[...] * inv_ref[...]
    cos_ref[...] = jnp.cos(ang)
    sin_ref[...] = jnp.sin(ang)


def _rope_tables(pos_col, inv_row, tm=512):
    M = pos_col.shape[0]
    N = inv_row.shape[1]
    return pl.pallas_call(
        _rope_table_kernel,
        grid=(M // tm,),
        in_specs=[pl.BlockSpec((tm, 1), lambda i: (i, 0)),
                  pl.BlockSpec((1, N), lambda i: (0, 0))],
        out_specs=[pl.BlockSpec((tm, N), lambda i: (i, 0))] * 2,
        out_shape=[jax.ShapeDtypeStruct((M, N), F32)] * 2,
        compiler_params=_cparams(("parallel",)),
        name="rope_tables",
    )(pos_col, inv_row)


def _mla_prep_kernel(z_ref, zs_ref, cos_ref, sin_ref, qg_ref, kvg_ref, wq_ref, wkv_ref,
                     q_ref, k_ref, v_ref):
    nq = MLA_HEADS * MLA_NOPE_DIM
    nr = MLA_HEADS * MLA_ROPE_DIM
    scale = MLA_QK_DIM ** -0.5
    z = z_ref[0]
    cq = _rms(z[:, :MLA_Q_RANK], qg_ref[...]).astype(BF16)
    ckv = _rms(z[:, MLA_Q_RANK:], kvg_ref[...]).astype(BF16)
    cos = cos_ref[0]
    sin = sin_ref[0]
    q_nope = _dot(cq, wq_ref[:, :nq]) * scale
    q_rot = (_dot(cq, wq_ref[:, nq:nq + nr]) * cos + _dot(cq, wq_ref[:, nq + nr:]) * sin) * scale
    k_nope = _dot(ckv, wkv_ref[:, :nq])
    v = _dot(ckv, wkv_ref[:, nq:])
    zs = zs_ref[0]
    r, d = MLA_ROPE_DIM, MLA_NOPE_DIM
    k_rot = zs[:, :r] * cos[:, :r] + zs[:, r:2 * r] * sin[:, :r]
    ones_col = _ones_column(z.shape[0])
    for h in range(MLA_HEADS):
        q_ref[0, h] = jnp.concatenate([q_nope[:, h * d:(h + 1) * d], q_rot[:, h * r:(h + 1) * r]],
                                      axis=-1).astype(BF16)
        k_ref[0, h] = jnp.concatenate([k_nope[:, h * d:(h + 1) * d], k_rot], axis=-1).astype(BF16)
        v_ref[0, h] = jnp.concatenate([v[:, h * MLA_V_DIM:(h + 1) * MLA_V_DIM], ones_col],
                                      axis=-1).astype(BF16)


def _mla_prep(z_mla, z_small, cos, sin, qg, kvg, wq, wkv, tm=512):
    B, S, _ = z_mla.shape
    nr = MLA_HEADS * MLA_ROPE_DIM
    tok = lambda n: pl.BlockSpec((1, tm, n), lambda b, i: (b, i, 0))
    heads = lambda d: pl.BlockSpec((1, MLA_HEADS, tm, d), lambda b, i: (b, 0, i, 0))
    full = lambda a: pl.BlockSpec(a.shape, lambda b, i: (0,) * a.ndim)
    return pl.pallas_call(
        _mla_prep_kernel,
        grid=(B, S // tm),
        in_specs=[tok(z_mla.shape[2]), tok(SMALL_COLS), tok(nr), tok(nr),
                  full(qg), full(kvg), full(wq), full(wkv)],
        out_specs=[heads(MLA_QK_DIM), heads(MLA_QK_DIM), heads(2 * MLA_V_DIM)],
        out_shape=[jax.ShapeDtypeStruct((B, MLA_HEADS, S, d), BF16)
                   for d in (MLA_QK_DIM, MLA_QK_DIM, 2 * MLA_V_DIM)],
        compiler_params=_cparams(("parallel", "parallel")),
        name="mla_prep",
    )(z_mla, z_small, cos, sin, qg, kvg, wq, wkv)


CUM_CHUNK = 512


def _fox_cumsum_kernel(f_ref, b_ref, u_ref, q_ref, k_ref, o_ref, qn_ref, kn_ref):
    x = f_ref[0] + b_ref[...]
    lf = jnp.minimum(x, 0.0) - jnp.log1p(jnp.exp(-jnp.abs(x)))
    S = lf.shape[1]
    carry = jnp.zeros((lf.shape[0], 1), F32)
    for c in range(0, S, CUM_CHUNK):
        loc = jnp.dot(lf[:, c:c + CUM_CHUNK], u_ref[...], preferred_element_type=F32,
                      precision=lax.Precision.HIGHEST) + carry
        o_ref[0, :, c:c + CUM_CHUNK] = loc
        carry = loc[:, CUM_CHUNK - 1:CUM_CHUNK]
    ones = jnp.ones((8, q_ref.shape[3]), BF16)
    for h in range(q_ref.shape[1]):
        for src, dst in ((q_ref, qn_ref), (k_ref, kn_ref)):
            v = src[0, h]
            dst[0, h:h + 1, :] = _dot_nt(ones, v * v)[0:1]


def _fox_cumsum(f_bhs, bias_col, q, k):
    B, H, S = f_bhs.shape
    Dh = q.shape[3]
    assert S % CUM_CHUNK == 0
    upper = jnp.asarray(np.triu(np.ones((CUM_CHUNK, CUM_CHUNK), np.float32)))
    return pl.pallas_call(
        _fox_cumsum_kernel,
        grid=(B,),
        in_specs=[pl.BlockSpec((1, H, S), lambda b: (b, 0, 0)),
                  pl.BlockSpec((H, 1), lambda b: (0, 0)),
                  pl.BlockSpec((CUM_CHUNK, CUM_CHUNK), lambda b: (0, 0)),
                  pl.BlockSpec((1, H, S, Dh), lambda b: (b, 0, 0, 0)),
                  pl.BlockSpec((1, H, S, Dh), lambda b: (b, 0, 0, 0))],
        out_specs=[pl.BlockSpec((1, H, S), lambda b: (b, 0, 0))] * 3,
        out_shape=[jax.ShapeDtypeStruct((B, H, S), F32)] * 3,
        compiler_params=_cparams(("parallel",)),
        name="fox_cumsum",
    )(f_bhs, bias_col, upper, q, k)


def _nsa_compress_kernel(r_ref, pe_ref, w_ref, o_ref):
    r = r_ref[0, 0]
    first = _dot((r + pe_ref[0, 0:1, :]).astype(BF16), w_ref[0, 0])
    second = _dot((r + pe_ref[0, 1:2, :]).astype(BF16), w_ref[0, 1])
    nc = r.shape[0]
    nxt = pltpu.roll(second, nc - 1, 0)
    row = lax.broadcasted_iota(jnp.int32, (nc, 1), 0)
    o_ref[0, 0] = jnp.where(row < nc - 1, first + nxt, 0.0).astype(o_ref.dtype)


def _nsa_compress(rows, pe, w):
    B, F, NC, K = rows.shape
    return pl.pallas_call(
        _nsa_compress_kernel,
        grid=(B, F),
        in_specs=[pl.BlockSpec((1, 1, NC, K), lambda b, f: (b, f, 0, 0)),
                  pl.BlockSpec((1, 2, K), lambda b, f: (f // NSA_KV_HEADS, 0, 0)),
                  pl.BlockSpec((1, 2, K, HEAD_DIM), lambda b, f: (f // NSA_KV_HEADS, 0, 0, 0))],
        out_specs=pl.BlockSpec((1, 1, NC, HEAD_DIM), lambda b, f: (b, f, 0, 0)),
        out_shape=jax.ShapeDtypeStruct((B, F, NC, HEAD_DIM), BF16),
        compiler_params=_cparams(("parallel", "parallel")),
        name="nsa_compress",
    )(rows, pe, w)


NSA_CMP_TQ = 512


def _nsa_cmp_kernel(q_ref, kc_ref, vc_ref, pq_ref, pc_ref, sl_ref, gate_ref, ovt_ref,
                    oc_ref, selt_ref, any_ref):
    G, tq, Dh = q_ref.shape[2:]
    R = G * tq
    i = pl.program_id(2)
    q = q_ref[0, 0].reshape(R, Dh)
    kc = kc_ref[0, 0]
    nc = kc.shape[0]
    n_sel = ovt_ref.shape[0]
    s = _dot_nt(q, kc)
    pq = jnp.concatenate([jnp.broadcast_to(pq_ref[0], (tq, LANES))] * G, axis=0)
    s = s - _rep(sl_ref[0], nc // LANES) * jnp.abs(_rep(pq, nc // LANES) - pc_ref[0])
    t_row = (i * tq + lax.broadcasted_iota(jnp.int32, (G, tq, nc), 1)).reshape(R, nc)
    c_col = lax.broadcasted_iota(jnp.int32, (1, nc), 1)
    c_end = jnp.where(c_col < nc - 1, c_col * NSA_CMP_STRIDE + (NSA_CMP_LEN - 1), jnp.int32(2 ** 30))
    s = jnp.where(c_end <= t_row, s, NEG_INF)
    m = jnp.max(s, axis=-1, keepdims=True)
    e = jnp.exp(s - m)
    l = jnp.sum(e, axis=-1, keepdims=True)
    p = e * jnp.where(m > 0.5 * NEG_INF, 1.0 / l, 0.0)
    o = _dot(p.astype(BF16), vc_ref[0, 0])
    oc_ref[0, 0] = (_gate_column(gate_ref[0, 0], 0, G) * o).reshape(G, tq, Dh)
    p_sum = jnp.sum(p.reshape(G, tq, nc), axis=0)
    imp = _dot_nt(ovt_ref[...], p_sum, precision=lax.Precision.HIGHEST)
    j = lax.broadcasted_iota(jnp.int32, (n_sel, 1), 0)

    def select(imp_c, t0):
        t_q = t0 + lax.broadcasted_iota(jnp.int32, (1, LANES), 1)
        jt = lax.shift_right_logical(t_q, NSA_SEL_LEN.bit_length() - 1)
        forced = (j == 0) | (j == jt) | (j == jt - 1)
        causal = j <= jt
        val = jnp.where(forced, FORCE_SCORE, imp_c)
        val = jnp.where(causal, val, NEG_INF)
        sel = jnp.zeros((n_sel, LANES), jnp.int32)
        for _ in range(min(NSA_TOP_N, n_sel)):
            best = jnp.max(val, axis=0, keepdims=True)
            idx = jnp.min(jnp.where(val == best, j, n_sel), axis=0, keepdims=True)
            pick = j == idx
            sel = jnp.where(pick, 1, sel)
            val = jnp.where(pick, -jnp.inf, val)
        return (sel > 0) & causal

    keep = jnp.concatenate([select(imp[:, c:c + LANES], i * tq + c) for c in range(0, tq, LANES)], axis=1)
    selt_ref[0, 0] = jnp.where(keep, 0.0, NEG_INF).astype(selt_ref.dtype)
    per_tile = any_ref.shape[2]
    any_ref[0, 0] = jnp.max(jnp.where(keep, 1.0, 0.0).reshape(per_tile, n_sel // per_tile, tq), axis=1)


def _nsa_cmp(q, kvc, pos_col, pc_row, slopes, gate, overlap_t, n_tiles):
    B, Hk, G, S, Dh = q.shape
    NC = kvc.shape[2]
    n_sel = overlap_t.shape[0]
    assert NSA_SEL_LEN & (NSA_SEL_LEN - 1) == 0 and n_sel % n_tiles == 0
    tq = NSA_CMP_TQ
    R = G * tq
    return pl.pallas_call(
        _nsa_cmp_kernel,
        grid=(B, Hk, S // tq),
        in_specs=[pl.BlockSpec((1, 1, G, tq, Dh), lambda b, h, i: (b, h, 0, i, 0)),
                  pl.BlockSpec((1, 1, NC, Dh), lambda b, h, i: (b, h, 0, 0)),
                  pl.BlockSpec((1, 1, NC, Dh), lambda b, h, i: (b, h + Hk, 0, 0)),
                  pl.BlockSpec((1, tq, 1), lambda b, h, i: (b, i, 0)),
                  pl.BlockSpec((1, 1, NC), lambda b, h, i: (b, 0, 0)),
                  pl.BlockSpec((1, R, LANES), lambda b, h, i: (h, 0, 0)),
                  pl.BlockSpec((1, 1, tq, 3 * G), lambda b, h, i: (b, h, i, 0)),
                  pl.BlockSpec((n_sel, NC), lambda b, h, i: (0, 0))],
        out_specs=[pl.BlockSpec((1, 1, G, tq, Dh), lambda b, h, i: (b, h, 0, i, 0)),
                   pl.BlockSpec((1, 1, n_sel, tq), lambda b, h, i: (b, h, 0, i)),
                   pl.BlockSpec((1, 1, n_tiles, tq), lambda b, h, i: (b, h, 0, i))],
        out_shape=[jax.ShapeDtypeStruct((B, Hk, G, S, Dh), F32),
                   jax.ShapeDtypeStruct((B, Hk, n_sel, S), BF16),
                   jax.ShapeDtypeStruct((B, Hk, n_tiles, S), F32)],
        compiler_params=_cparams(("parallel", "parallel", "parallel")),
        name="nsa_cmp",
    )(q, kvc, kvc, pos_col, pc_row, slopes, gate, overlap_t)


class AttnCfg(NamedTuple):
    tq: int
    tk: int = 0
    window: int = 0
    alibi: bool = False
    decay: bool = False
    sink: bool = False
    gate: int = -1
    n_sub: int = 1
    add: bool = False
    selaug: bool = False
    sparse: bool = False
    out_dtype: object = BF16


def _gate_column(gates, branch, G):
    cols = [gates[:, 3 * g + branch:3 * g + branch + 1] for g in range(G)]
    return jax.nn.sigmoid(jnp.concatenate(cols, axis=0))


def _epilogue(cfg, out, gates, add, G, tq):
    R, Dv = out.shape
    if cfg.gate >= 0:
        out = out * _gate_column(gates, cfg.gate, G)
    out = out.reshape(G, tq, Dv)
    if cfg.add:
        out = out + add
    return out


def _flash_kernel(*refs, cfg):
    it = iter(refs)
    cnt_ref = tiles_ref = None
    if cfg.sparse:
        cnt_ref, tiles_ref = next(it), next(it)
    q_ref, k_ref, v_ref, dmask_ref = next(it), next(it), next(it), next(it)
    pq_ref = pk_ref = sl_ref = cq_ref = ck_ref = gate_ref = add_ref = None
    selb_ref = next(it) if cfg.selaug else None
    if cfg.alibi:
        pq_ref, pk_ref, sl_ref = next(it), next(it), next(it)
    if cfg.decay:
        cq_ref, ck_ref = next(it), next(it)
    if cfg.gate >= 0:
        gate_ref = next(it)
    if cfg.add:
        add_ref = next(it)
    o_ref = next(it)
    m_s, acc_s, sa_s, sb_s = (next(it) for _ in range(4))
    pq_s = next(it) if cfg.alibi else None
    cq_s = next(it) if cfg.decay else None

    G, tq, Dq = q_ref.shape[2:]
    Da = v_ref.shape[3]
    Dv = Da // 2
    tk = cfg.tk
    nrep = tk // LANES
    R = G * tq
    i = pl.program_id(2)
    q = q_ref[0, 0].reshape(R, Dq)
    if cfg.selaug:
        bias = jnp.concatenate([selb_ref[0, 0].astype(F32)] * G, axis=0)
        q = jnp.concatenate([q.astype(F32), bias], axis=-1).astype(BF16)

    m_s[...] = jnp.full((R, LANES), NEG_INF, F32)
    acc_s[...] = jnp.zeros((R, Da), F32)
    if cfg.alibi:
        pq_s[...] = jnp.broadcast_to(jnp.concatenate([pq_ref[0]] * G, axis=0), (R, LANES))
    if cfg.decay:
        cq_all = cq_ref[0]
        head = lax.broadcasted_iota(jnp.int32, cq_all.shape, 1) == pl.program_id(1)
        cq_s[...] = jnp.broadcast_to(jnp.sum(jnp.where(head, cq_all, 0.0), axis=-1, keepdims=True), (R, LANES))

    def scores(j, s_ref):
        k0 = pl.multiple_of(j * tk, tk)
        s = _dot_nt(q, k_ref[0, 0, pl.ds(k0, tk), :])
        if cfg.alibi:
            s = s - _rep(sl_ref[0], nrep) * jnp.abs(_rep(pq_s[...], nrep) - pk_ref[0, j])
        if cfg.decay:
            s = s + (_rep(cq_s[...], nrep) - ck_ref[0, 0, j])
        s_ref[...] = s

    def update(j, s_ref, mask_variant=None):
        s = s_ref[...]
        if mask_variant is not None:
            s = s + dmask_ref[mask_variant]
        m_prev = m_s[...]
        m_new = jnp.maximum(m_prev, jnp.max(s, axis=-1, keepdims=True))
        alpha = jnp.exp(m_prev - m_new)
        p = jnp.exp((s - _rep(m_new, nrep)).astype(BF16))
        k0 = pl.multiple_of(j * tk, tk)
        acc_s[...] = _rep(alpha, Da // LANES) * acc_s[...] + _dot(p, v_ref[0, 0, pl.ds(k0, tk), :])
        m_s[...] = m_new

    diag = (i * tq) // tk
    n_diag = max(1, tq // tk)
    variants = [i % (tk // tq)] if tq < tk else list(range(n_diag))
    if cfg.sparse:
        step_id = (pl.program_id(0) * pl.num_programs(1) + pl.program_id(1)) * pl.num_programs(2) + i
        n_vis = cnt_ref[step_id]
        nk = k_ref.shape[2] // tk

        def tile_at(pos):
            return jnp.where(pos < n_vis, tiles_ref[step_id * nk + jnp.minimum(pos, nk - 1)], diag)
    else:
        n_vis = diag

        def tile_at(pos):
            return pos

    scores(tile_at(0), sa_s)

    def pair(pp, c):
        pos = 2 * pp
        scores(tile_at(pos + 1), sb_s)
        update(tile_at(pos), sa_s)
        scores(tile_at(pos + 2), sa_s)
        update(tile_at(pos + 1), sb_s)
        return c
    lax.fori_loop(0, n_vis // 2, pair, 0)

    def tail(leftover):
        items = ([(tile_at(n_vis - 1), None)] if leftover else []) + \
                [(diag + d, variants[d]) for d in range(n_diag)]
        bufs = (sa_s, sb_s)
        for k, (tile, variant) in enumerate(items):
            if k + 1 < len(items):
                scores(items[k + 1][0], bufs[(k + 1) % 2])
            update(tile, bufs[k % 2], variant)

    @pl.when(n_vis % 2 == 1)
    def _():
        tail(True)

    @pl.when(n_vis % 2 == 0)
    def _():
        tail(False)

    acc = acc_s[...]
    out = acc[:, :Dv] / acc[:, Dv:Dv + 1]
    out = _epilogue(cfg, out, gate_ref[0, 0] if cfg.gate >= 0 else None, add_ref[0, 0] if cfg.add else None,
                    G, tq)
    o_ref[0, 0] = out.astype(o_ref.dtype)


def _flash(cfg, q, k, v, *, visit=None, selb=None, pos_col=None, pos_rows=None, slopes=None,
           cum_col=None, cum_rows=None, gate=None, add=None):
    B, Hk, G, S, Dq = q.shape
    Dk, Da = k.shape[3], v.shape[3]
    Dv = Da // 2
    tq, tk = cfg.tq, cfg.tk
    assert S % tq == 0 and S % tk == 0 and (tk % tq == 0 or tq % tk == 0) and tk % LANES == 0
    assert Da % LANES == 0
    assert Dk == Dq + (selb.shape[3] if cfg.selaug else 0) and cfg.sparse == (visit is not None)
    R = G * tq
    nk = S // tk
    qmap = lambda b, h, i, *_: (b, h, 0, i, 0)
    kvmap = lambda b, h, i, *_: (b, h, 0, 0)
    t_local = np.tile(np.arange(tq), G)[None, :, None]
    first = (np.arange(tk // tq) * tq if tq < tk else -np.arange(tq // tk) * tk)[:, None, None]
    dmask = jnp.asarray(np.where(np.arange(tk)[None, None, :] <= first + t_local, 0.0, NEG_INF), dtype=F32)
    args = [q, k, v, dmask]
    specs = [pl.BlockSpec((1, 1, G, tq, Dq), qmap),
             pl.BlockSpec((1, 1, S, Dk), kvmap),
             pl.BlockSpec((1, 1, S, Da), kvmap),
             pl.BlockSpec(dmask.shape, lambda b, h, i, *_: (0, 0, 0), pipeline_mode=pl.Buffered(1))]
    scratch = [pltpu.VMEM((R, LANES), F32), pltpu.VMEM((R, Da), F32),
               pltpu.VMEM((R, tk), F32), pltpu.VMEM((R, tk), F32)]
    if cfg.selaug:
        args += [selb]
        specs += [pl.BlockSpec((1, 1, tq, Dk - Dq), lambda b, h, i, *_: (b, h, i, 0))]
    if cfg.alibi:
        args += [pos_col, pos_rows, slopes]
        specs += [pl.BlockSpec((1, tq, 1), lambda b, h, i, *_: (b, i, 0)),
                  pl.BlockSpec((1, nk, 1, tk), lambda b, h, i, *_: (b, 0, 0, 0)),
                  pl.BlockSpec((1, R, LANES), lambda b, h, i, *_: (h, 0, 0))]
        scratch.append(pltpu.VMEM((R, LANES), F32))
    if cfg.decay:
        args += [cum_col, cum_rows]
        assert G == 1
        specs += [pl.BlockSpec((1, tq, Hk), lambda b, h, i, *_: (b, i, 0)),
                  pl.BlockSpec((1, 1, nk, 1, tk), lambda b, h, i, *_: (b, h, 0, 0, 0))]
        scratch.append(pltpu.VMEM((R, LANES), F32))
    if cfg.gate >= 0:
        args += [gate]
        specs += [pl.BlockSpec((1, 1, tq, 3 * G), lambda b, h, i, *_: (b, h, i, 0))]
    if cfg.add:
        args += [add]
        specs += [pl.BlockSpec((1, 1, G, tq, Dv), qmap)]
    grid_spec = pltpu.PrefetchScalarGridSpec(
        num_scalar_prefetch=2 if cfg.sparse else 0,
        grid=(B, Hk, S // tq),
        in_specs=specs,
        out_specs=pl.BlockSpec((1, 1, G, tq, Dv), qmap),
        scratch_shapes=scratch)
    return pl.pallas_call(
        functools.partial(_flash_kernel, cfg=cfg),
        grid_spec=grid_spec,
        out_shape=jax.ShapeDtypeStruct((B, Hk, G, S, Dv), cfg.out_dtype),
        compiler_params=_cparams(("parallel", "parallel", "parallel")),
        name="flash",
    )(*(list(visit) if cfg.sparse else []), *args)


BAND_VISIBLE, BAND_CUT_BELOW, BAND_CUT_ABOVE, BAND_HIDDEN = range(4)


def _band_kernel(*refs, cfg):
    it = iter(refs)
    q_ref, k_ref, v_ref, pq_ref, pk_ref, sl_ref, mask_ref = (next(it) for _ in range(7))
    sink_ref = next(it) if cfg.sink else None
    gate_ref = next(it) if cfg.gate >= 0 else None
    add_ref = next(it) if cfg.add else None
    o_ref = next(it)

    G, Dq = q_ref.shape[2], q_ref.shape[4]
    tq = cfg.tq
    R = G * tq
    W = cfg.window
    span = W + tq
    slope = sl_ref[0]
    for sub in range(cfg.n_sub):
        i = pl.program_id(2) * cfg.n_sub + sub
        tok = slice(sub * tq, (sub + 1) * tq)
        start = pl.multiple_of(jnp.maximum(i * tq - W, 0), LANES)
        blk0 = lax.shift_right_logical(start, LANES.bit_length() - 1)
        q = q_ref[0, 0, :, tok, :].reshape(R, Dq)
        s = _dot_nt(q, k_ref[0, 0, pl.ds(start, span), :])
        pq = jnp.concatenate([pq_ref[0, tok, :]] * G, axis=0)
        off = i * tq - start
        chunks = []
        for c in range(span // LANES):
            sc = s[:, c * LANES:(c + 1) * LANES]
            sc = sc - slope * jnp.abs(pq - pk_ref[0, blk0 + c])
            lo = c * LANES - off
            kind = jnp.where(lo == -W, BAND_CUT_BELOW,
                             jnp.where(lo == 0, BAND_CUT_ABOVE, jnp.where(lo > 0, BAND_HIDDEN, BAND_VISIBLE)))
            chunks.append(sc + mask_ref[kind])
        s = jnp.concatenate(chunks, axis=1)
        m = jnp.max(s, axis=-1, keepdims=True)
        e = jnp.exp(s - m)
        l = jnp.sum(e, axis=-1, keepdims=True)
        acc = _dot(e.astype(BF16), v_ref[0, 0, pl.ds(start, span), :])
        if cfg.sink:
            sink = sink_ref[0]
            m_f = jnp.maximum(m, sink)
            w = jnp.exp(m - m_f)
            out = acc * (w / (l * w + jnp.exp(sink - m_f)))
        else:
            out = acc / l
        out = _epilogue(cfg, out, gate_ref[0, 0, tok, :] if cfg.gate >= 0 else None,
                        add_ref[0, 0, :, tok, :] if cfg.add else None, G, tq)
        o_ref[0, 0, :, tok, :] = out.astype(o_ref.dtype)


def _band(cfg, q, k, v, *, pos_col, pos_rows, slopes, sinks=None, gate=None, add=None):
    B, Hk, G, S, Dq = q.shape
    Dv = v.shape[3]
    tq = cfg.tq
    T = tq * cfg.n_sub
    assert tq == LANES and cfg.window % LANES == 0 and S % T == 0 and S >= cfg.window + tq
    R = G * tq
    qmap = lambda b, h, i: (b, h, 0, i, 0)
    kvmap = lambda b, h, i: (b, h, 0, 0)
    d = np.tile(np.arange(tq), G)[:, None] - np.arange(LANES)[None, :]
    masks = np.zeros((4, R, LANES), np.float32)
    masks[BAND_CUT_BELOW] = np.where(d < 0, 0.0, NEG_INF)
    masks[BAND_CUT_ABOVE] = np.where(d >= 0, 0.0, NEG_INF)
    masks[BAND_HIDDEN] = NEG_INF
    args = [q, k, v, pos_col, pos_rows, slopes, jnp.asarray(masks)]
    specs = [pl.BlockSpec((1, 1, G, T, Dq), qmap),
             pl.BlockSpec((1, 1, S, Dq), kvmap),
             pl.BlockSpec((1, 1, S, Dv), kvmap),
             pl.BlockSpec((1, T, 1), lambda b, h, i: (b, i, 0)),
             pl.BlockSpec((1, S // LANES, 1, LANES), lambda b, h, i: (b, 0, 0, 0)),
             pl.BlockSpec((1, R, 1), lambda b, h, i: (h, 0, 0)),
             pl.BlockSpec((4, R, LANES), lambda b, h, i: (0, 0, 0))]
    if cfg.sink:
        args += [sinks]
        specs += [pl.BlockSpec((1, R, 1), lambda b, h, i: (h, 0, 0))]
    if cfg.gate >= 0:
        args += [gate]
        specs += [pl.BlockSpec((1, 1, T, 3 * G), lambda b, h, i: (b, h, i, 0))]
    if cfg.add:
        args += [add]
        specs += [pl.BlockSpec((1, 1, G, T, Dv), qmap)]
    return pl.pallas_call(
        functools.partial(_band_kernel, cfg=cfg),
        grid=(B, Hk, S // T),
        in_specs=specs,
        out_specs=pl.BlockSpec((1, 1, G, T, Dv), qmap),
        out_shape=jax.ShapeDtypeStruct((B, Hk, G, S, Dv), cfg.out_dtype),
        compiler_params=_cparams(("parallel", "parallel", "parallel")),
        name="band",
    )(*args)


def _merge_kernel(x_ref, y0_ref, y1_ref, y2_ref, y3_ref, gpre_ref, wg_ref, wb_ref, wo_ref, gpost_ref,
                  o_ref):
    x = x_ref[0]
    h = _rms(x, gpre_ref[...]).astype(BF16)
    merged = None
    for b, y_ref in enumerate((y0_ref, y1_ref, y2_ref, y3_ref)):
        gate = jax.nn.sigmoid(_dot(h, wg_ref[b]))
        y = jnp.concatenate([y_ref[0, hd] for hd in range(y_ref.shape[1])], axis=-1)
        br = _dot(y, wb_ref[b])
        merged = gate * br if merged is None else merged + gate * br
    y = _dot(merged.astype(BF16), wo_ref[...])
    o_ref[0] = x + _rms(y, gpost_ref[...])


def _merge(x, ys, gpre, wg, wb, wo, gpost, tm=512):
    B, S, D = x.shape
    full = lambda a: pl.BlockSpec(a.shape, lambda b, i: (0,) * a.ndim, pipeline_mode=pl.Buffered(1))
    yspec = lambda y: pl.BlockSpec((1, y.shape[1], tm, y.shape[3]), lambda b, i: (b, 0, i, 0))
    return pl.pallas_call(
        _merge_kernel,
        grid=(B, S // tm),
        in_specs=[pl.BlockSpec((1, tm, D), lambda b, i: (b, i, 0))] + [yspec(y) for y in ys]
                 + [full(gpre), full(wg), full(wb), full(wo), full(gpost)],
        out_specs=pl.BlockSpec((1, tm, D), lambda b, i: (b, i, 0)),
        out_shape=jax.ShapeDtypeStruct((B, S, D), F32),
        compiler_params=_cparams(("parallel", "parallel")),
        name="merge",
    )(x, *ys, gpre, wg, wb, wo, gpost)


def _mem_kv_kernel(mem_ref, g_ref, w_ref, o_ref):
    o_ref[0] = _dot(_rms(mem_ref[0], g_ref[...]).astype(BF16), w_ref[...]).astype(BF16)


def _mem_kv(mem, g, w):
    B, Mm, D = mem.shape
    N = w.shape[1]
    return pl.pallas_call(
        _mem_kv_kernel,
        grid=(B,),
        in_specs=[pl.BlockSpec((1, Mm, D), lambda b: (b, 0, 0)),
                  pl.BlockSpec((1, D), lambda b: (0, 0)),
                  pl.BlockSpec((D, N), lambda b: (0, 0))],
        out_specs=pl.BlockSpec((1, Mm, N), lambda b: (b, 0, 0)),
        out_shape=jax.ShapeDtypeStruct((B, Mm, N), BF16),
        compiler_params=_cparams(("parallel",)),
        name="mem_kv",
    )(mem, g, w)


def _xattn_kernel(x_ref, kv_ref, gpre_ref, wq_ref, wo_ref, gpost_ref, o_ref):
    x = x_ref[0]
    h = _rms(x, gpre_ref[...]).astype(BF16)
    q_all = _dot(h, wq_ref[...])
    outs = []
    for hd in range(XA_HEADS):
        q = q_all[:, hd * HEAD_DIM:(hd + 1) * HEAD_DIM].astype(BF16)
        s = _dot_nt(q, kv_ref[0, hd])
        m = jnp.max(s, axis=-1, keepdims=True)
        e = jnp.exp(s - m)
        p = e / jnp.sum(e, axis=-1, keepdims=True)
        outs.append(_dot(p.astype(BF16), kv_ref[0, XA_HEADS + hd]))
    y = _dot(jnp.concatenate(outs, axis=-1).astype(BF16), wo_ref[...])
    o_ref[0] = x + _rms(y, gpost_ref[...])


def _xattn(x, kv, gpre, wq, wo, gpost, tm=1024):
    B, S, D = x.shape
    full = lambda a: pl.BlockSpec(a.shape, lambda b, i: (0,) * a.ndim)
    return pl.pallas_call(
        _xattn_kernel,
        grid=(B, S // tm),
        in_specs=[pl.BlockSpec((1, tm, D), lambda b, i: (b, i, 0)),
                  pl.BlockSpec((1,) + kv.shape[1:], lambda b, i: (b, 0, 0, 0)),
                  full(gpre), full(wq), full(wo), full(gpost)],
        out_specs=pl.BlockSpec((1, tm, D), lambda b, i: (b, i, 0)),
        out_shape=jax.ShapeDtypeStruct((B, S, D), F32),
        compiler_params=_cparams(("parallel", "parallel")),
        name="xattn",
    )(x, kv, gpre, wq, wo, gpost)


FFN_CHUNK = 256


def _ffn_kernel(x_ref, gpre_ref, wg_ref, wu_ref, wo_ref, gpost_ref, o_ref):
    x = x_ref[...]
    h = _rms(x, gpre_ref[...]).astype(BF16)
    y = None
    for c in range(0, D_FF, FFN_CHUNK):
        g = _dot(h, wg_ref[:, c:c + FFN_CHUNK])
        u = _dot(h, wu_ref[:, c:c + FFN_CHUNK])
        a = (g * jax.nn.sigmoid(g) * u).astype(BF16)
        part = _dot(a, wo_ref[c:c + FFN_CHUNK, :])
        y = part if y is None else y + part
    o_ref[...] = x + _rms(y, gpost_ref[...])


def _ffn(x2d, gpre, wg, wu, wo, gpost, tm=512):
    M, D = x2d.shape
    full = lambda a: pl.BlockSpec(a.shape, lambda i: (0,) * a.ndim, pipeline_mode=pl.Buffered(1))
    return pl.pallas_call(
        _ffn_kernel,
        grid=(M // tm,),
        in_specs=[pl.BlockSpec((tm, D), lambda i: (i, 0)),
                  full(gpre), full(wg), full(wu), full(wo), full(gpost)],
        out_specs=pl.BlockSpec((tm, D), lambda i: (i, 0)),
        out_shape=jax.ShapeDtypeStruct((M, D), F32),
        compiler_params=_cparams(("parallel",)),
        name="ffn",
    )(x2d, gpre, wg, wu, wo, gpost)


def _alibi_slopes(n):
    return (2.0 ** (-8.0 * np.arange(1, n + 1) / n)).astype(np.float32)


def _rotate_half_cols(w):
    half = w.shape[-1] // 2
    return jnp.concatenate([-w[..., half:], w[..., :half]], axis=-1)


def _prep_weights(p):
    L = p["w_in"].shape[0]
    offs = np.concatenate([[0], np.cumsum(IN_SPLITS)])
    col = {n: p["w_in"][:, :, offs[k]:offs[k + 1]] for k, n in enumerate(IN_NAMES)}
    qk_scale = HEAD_DIM ** -0.5
    for n in ("nq", "sq", "fq"):
        col[n] = col[n] * qk_scale
    pad = jnp.zeros((L, D_MODEL, SMALL_COLS - 2 * MLA_ROPE_DIM - 24 - FOX_HEADS), F32)
    small = jnp.concatenate([col["mkr"], _rotate_half_cols(col["mkr"]), col["ng"], col["ff"], pad], -1)
    w_all = jnp.concatenate([col["mcq"], col["mckv"], col["nkc"], col["nvc"], small]
                            + [col[n] for n in MAIN_ORDER], axis=-1).astype(BF16)

    wq = p["mla_w_q_up"].reshape(L, MLA_Q_RANK, MLA_HEADS, MLA_QK_DIM)
    wq_nope = wq[..., :MLA_NOPE_DIM].reshape(L, MLA_Q_RANK, -1)
    wq_rot = wq[..., MLA_NOPE_DIM:]
    wq_all = jnp.concatenate([wq_nope, wq_rot.reshape(L, MLA_Q_RANK, -1),
                              _rotate_half_cols(wq_rot).reshape(L, MLA_Q_RANK, -1)], -1).astype(BF16)
    wkv = p["mla_w_kv_up"].reshape(L, MLA_KV_RANK, MLA_HEADS, MLA_NOPE_DIM + MLA_V_DIM)
    wkv_all = jnp.concatenate([wkv[..., :MLA_NOPE_DIM].reshape(L, MLA_KV_RANK, -1),
                               wkv[..., MLA_NOPE_DIM:].reshape(L, MLA_KV_RANK, -1)], -1).astype(BF16)

    half = NSA_CMP_LEN // 2
    cmp_w = p["nsa_cmp_w"].reshape(L, 2, 2, half * HEAD_DIM, HEAD_DIM).astype(BF16)
    cmp_pe = p["nsa_cmp_pe"].reshape(L, 2, 2, half * HEAD_DIM)

    xa_wq = (p["xa_w_q"] * qk_scale).astype(BF16)
    xa_wo = p["xa_w_o"].astype(BF16)

    row = lambda a: a[:, None, :]
    return dict(
        w_all=w_all, mix_norm_pre=row(p["mix_norm_pre"]),
        wq_all=wq_all, wkv_all=wkv_all, mla_q_norm=row(p["mla_q_norm"]), mla_kv_norm=row(p["mla_kv_norm"]),
        cmp_w=cmp_w, cmp_pe=cmp_pe, swa_sinks=p["swa_sinks"], fox_f_bias=p["fox_f_bias"][:, :, None],
        w_gate=p["w_gate"].astype(BF16), w_branch=p["w_branch"].astype(BF16), w_out=p["w_out"].astype(BF16),
        mix_norm_post=row(p["mix_norm_post"]),
        xa_norm_pre=row(p["xa_norm_pre"]), xa_mem_norm=row(p["xa_mem_norm"]), xa_wq=xa_wq,
        xa_wkv=p["xa_w_kv"].astype(BF16), xa_wo=xa_wo, xa_norm_post=row(p["xa_norm_post"]),
        ffn_norm_pre=row(p["ffn_norm_pre"]), ffn_wg=p["ffn_w_in"][:, :, :D_FF].astype(BF16),
        ffn_wu=p["ffn_w_in"][:, :, D_FF:].astype(BF16), ffn_wo=p["ffn_w_out"].astype(BF16),
        ffn_norm_post=row(p["ffn_norm_post"]),
    )


FLASH_TQ = 512
FLASH_TK = 512
BAND_TQ = LANES
BAND_SUB = 4


def _per_row(vals_hg, tq, lanes=1):
    Hk, G = vals_hg.shape
    return jnp.broadcast_to(vals_hg[:, :, None, None], (Hk, G, tq, lanes)).reshape(Hk, G * tq, lanes)


def _visit_lists(needed, tq, tk):
    nq, nk = needed.shape[2:]
    diag = (np.arange(nq)[:, None] * tq) // tk
    needed = needed & jnp.asarray(np.arange(nk)[None, :] < diag)
    count = jnp.sum(needed, axis=-1).astype(jnp.int32)
    tiles = jnp.argsort(~needed, axis=-1, stable=True).astype(jnp.int32)
    return count.reshape(-1), tiles.reshape(-1)


ZERO_WEIGHT_LOGIT = -110.0


def _fox_tiles_needed(q_sq, k_sq, cum, tq, tk):
    B, H, S = cum.shape
    qn = jnp.sqrt(q_sq)
    kmax = jnp.sqrt(jnp.max(k_sq, axis=-1))
    row_bound = cum + 2.02 * qn * kmax[..., None]
    tile_bound = jnp.max(row_bound.reshape(B, H, S // tq, tq), axis=-1)
    cum_last = cum[:, :, tk - 1::tk]
    return tile_bound[..., None] - cum_last[:, :, None, :] > ZERO_WEIGHT_LOGIT


def _layer(x, w, c):
    B, S, D = x.shape
    M = B * S
    Hk, G, Dh = NSA_KV_HEADS, NSA_GROUP, HEAD_DIM

    z_mla, z_small, z_cmp, *heads = _inproj(x, w["mix_norm_pre"], w["w_all"])
    hm = dict(zip(MAIN_ORDER, heads))
    grouped = lambda t: t.reshape(B, Hk, G, S, t.shape[-1])
    single = lambda t: t[:, :, None]
    flat = lambda o: o.reshape(B, -1, S, o.shape[-1])

    ng = z_small[:, :, 64:88].reshape(B, S, Hk, 3 * G).transpose(0, 2, 1, 3)
    nq = grouped(hm["nq"])
    rows = z_cmp.reshape(B, 2 * Hk, S // NSA_CMP_STRIDE, NSA_CMP_STRIDE * Dh)
    kvc = _nsa_compress(rows, w["cmp_pe"], w["cmp_w"])
    nk = S // FLASH_TK
    oc, selt, tile_any = _nsa_cmp(nq, kvc, c["pos_col"], c["pc_row"], c["nsa_slopes_cmp"], ng,
                                  c["overlap_t"], nk)
    selb = selt.transpose(0, 1, 3, 2)
    sel_tq = FLASH_TQ // G
    used = tile_any.reshape(B, Hk, nk, S // sel_tq, sel_tq).max(-1).transpose(0, 1, 3, 2) > 0.0
    o_sel = _flash(
        AttnCfg(tq=sel_tq, tk=FLASH_TK, alibi=True, gate=1, add=True, selaug=True, sparse=True,
                out_dtype=F32),
        nq, hm["nks"], hm["nvs"], visit=_visit_lists(used, sel_tq, FLASH_TK), selb=selb,
        pos_col=c["pos_col"], pos_rows=c["pos_rows_flash"], slopes=c["nsa_slopes_flash"], gate=ng, add=oc)
    y_nsa = _band(
        AttnCfg(tq=BAND_TQ, n_sub=BAND_SUB, window=NSA_WINDOW, alibi=True, gate=2, add=True),
        nq, hm["nkw"], hm["nvw"], pos_col=c["pos_col"], pos_rows=c["pos_rows_band"],
        slopes=c["nsa_slopes_band"], gate=ng, add=o_sel)

    y_swa = _band(
        AttnCfg(tq=BAND_TQ, n_sub=BAND_SUB, window=SWA_WINDOW, alibi=True, sink=True),
        grouped(hm["sq"]), hm["sk"], hm["sv"], pos_col=c["pos_col"],
        pos_rows=c["pos_rows_band"], slopes=c["swa_slopes_band"],
        sinks=_per_row(w["swa_sinks"].reshape(SWA_KV_HEADS, SWA_GROUP), BAND_TQ))

    q_mla, k_mla, v_mla = _mla_prep(z_mla, z_small, c["cos"], c["sin"], w["mla_q_norm"], w["mla_kv_norm"],
                                    w["wq_all"], w["wkv_all"])
    y_mla = _flash(AttnCfg(tq=FLASH_TQ, tk=FLASH_TK), single(q_mla), k_mla, v_mla)

    f_logit = z_small[:, :, 88:96].transpose(0, 2, 1)
    cum, fq_sq, fk_sq = _fox_cumsum(f_logit, w["fox_f_bias"], hm["fq"], hm["fk"])
    y_fox = _flash(
        AttnCfg(tq=FLASH_TQ, tk=FLASH_TK, decay=True, sparse=True), single(hm["fq"]), hm["fk"], hm["fv"],
        visit=_visit_lists(_fox_tiles_needed(fq_sq, fk_sq, cum, FLASH_TQ, FLASH_TK), FLASH_TQ, FLASH_TK),
        cum_col=cum.transpose(0, 2, 1), cum_rows=cum.reshape(B, FOX_HEADS, S // FLASH_TK, 1, FLASH_TK))

    x1 = _merge(x, [flat(y_nsa), flat(y_swa), flat(y_mla), flat(y_fox)], w["mix_norm_pre"], w["w_gate"],
                w["w_branch"], w["w_out"], w["mix_norm_post"])

    kv = _mem_kv(c["mem"], w["xa_mem_norm"], w["xa_wkv"])
    kv = kv.reshape(B, -1, 2 * XA_HEADS, Dh).transpose(0, 2, 1, 3)
    x2 = _xattn(x1, kv, w["xa_norm_pre"], w["xa_wq"], w["xa_wo"], w["xa_norm_post"])

    x3 = _ffn(x2.reshape(M, D), w["ffn_norm_pre"], w["ffn_wg"], w["ffn_wu"], w["ffn_wo"], w["ffn_norm_post"])
    return x3.reshape(B, S, D)


def _constants(mem, positions, S):
    B = positions.shape[0]
    pos = positions.astype(F32)
    half = MLA_ROPE_DIM // 2
    inv = (ROPE_THETA ** (-np.arange(half, dtype=np.float32) / half)).astype(np.float32)
    inv_row = jnp.asarray(np.tile(inv, 2 * MLA_HEADS)[None, :])
    cos, sin = _rope_tables(pos.reshape(B * S, 1), inv_row)
    nc = S // NSA_CMP_STRIDE
    n_valid = (S - NSA_CMP_LEN) // NSA_CMP_STRIDE + 1
    assert n_valid == nc - 1
    c_end = np.minimum(np.arange(nc) * NSA_CMP_STRIDE + NSA_CMP_LEN - 1, S - 1)
    n_sel = S // NSA_SEL_LEN
    c_start = np.arange(nc) * NSA_CMP_STRIDE
    sel_start = np.arange(n_sel) * NSA_SEL_LEN
    overlap = ((c_start[:, None] < sel_start[None, :] + NSA_SEL_LEN)
               & (c_start[:, None] + NSA_CMP_LEN - 1 >= sel_start[None, :])
               & (np.arange(nc)[:, None] < n_valid)).astype(np.float32)
    nsa_slopes = jnp.asarray(_alibi_slopes(NSA_HEADS).reshape(NSA_KV_HEADS, NSA_GROUP))
    swa_slopes = jnp.asarray(_alibi_slopes(SWA_HEADS).reshape(SWA_KV_HEADS, SWA_GROUP))
    return dict(
        mem=mem, cos=cos.reshape(B, S, -1), sin=sin.reshape(B, S, -1),
        pos_col=pos[:, :, None],
        pos_rows_flash=pos.reshape(B, S // FLASH_TK, 1, FLASH_TK),
        pos_rows_band=pos.reshape(B, S // LANES, 1, LANES),
        pc_row=pos[:, c_end][:, None, :],
        overlap_t=jnp.asarray(overlap.T),
        nsa_slopes_cmp=_per_row(nsa_slopes, NSA_CMP_TQ, LANES),
        nsa_slopes_flash=_per_row(nsa_slopes, FLASH_TQ // NSA_GROUP, LANES),
        nsa_slopes_band=_per_row(nsa_slopes, BAND_TQ),
        swa_slopes_band=_per_row(swa_slopes, BAND_TQ),
    )


def kernel(x, mem, positions, mix_norm_pre, w_in, nsa_cmp_w, nsa_cmp_pe, swa_sinks, mla_q_norm, mla_w_q_up,
           mla_kv_norm, mla_w_kv_up, fox_f_bias, w_branch, w_gate, w_out, mix_norm_post, xa_norm_pre,
           xa_mem_norm, xa_w_q, xa_w_kv, xa_w_o, xa_norm_post, ffn_norm_pre, ffn_w_in, ffn_w_out,
           ffn_norm_post):
    params = dict(
        mix_norm_pre=mix_norm_pre, w_in=w_in, nsa_cmp_w=nsa_cmp_w, nsa_cmp_pe=nsa_cmp_pe, swa_sinks=swa_sinks,
        mla_q_norm=mla_q_norm, mla_w_q_up=mla_w_q_up, mla_kv_norm=mla_kv_norm, mla_w_kv_up=mla_w_kv_up,
        fox_f_bias=fox_f_bias, w_branch=w_branch, w_gate=w_gate, w_out=w_out, mix_norm_post=mix_norm_post,
        xa_norm_pre=xa_norm_pre, xa_mem_norm=xa_mem_norm, xa_w_q=xa_w_q, xa_w_kv=xa_w_kv, xa_w_o=xa_w_o,
        xa_norm_post=xa_norm_post, ffn_norm_pre=ffn_norm_pre, ffn_w_in=ffn_w_in, ffn_w_out=ffn_w_out,
        ffn_norm_post=ffn_norm_post)
    S = x.shape[1]
    weights = _prep_weights(params)
    consts = _constants(mem, positions, S)

    def body(xc, w):
        return _layer(xc, w, consts), None

    out, _ = lax.scan(body, x, weights)
    return out
```

```python
import functools
from typing import NamedTuple

import numpy as np
import jax
import jax.numpy as jnp
from jax import lax
from jax.experimental import pallas as pl
from jax.experimental.pallas import tpu as pltpu

F32 = jnp.float32
BF16 = jnp.bfloat16

D_MODEL = 1024
HEAD_DIM = 64
NORM_EPS = 1e-6
NEG_INF = -1e30
FORCE_SCORE = 1e9
LANES = 128

NSA_HEADS = 8
NSA_KV_HEADS = 2
NSA_GROUP = NSA_HEADS // NSA_KV_HEADS
NSA_CMP_LEN = 32
NSA_CMP_STRIDE = 16
NSA_SEL_LEN = 64
NSA_TOP_N = 8
NSA_WINDOW = 512

SWA_HEADS = 8
SWA_KV_HEADS = 2
SWA_GROUP = SWA_HEADS // SWA_KV_HEADS
SWA_WINDOW = 128

MLA_HEADS = 8
MLA_Q_RANK = 256
MLA_KV_RANK = 128
MLA_NOPE_DIM = 64
MLA_ROPE_DIM = 32
MLA_V_DIM = 64
MLA_QK_DIM = MLA_NOPE_DIM + MLA_ROPE_DIM
ROPE_THETA = 10000.0

FOX_HEADS = 8
XA_HEADS = 4
N_BRANCH = 4
BRANCH_WIDTH = 512
D_FF = ((8 * D_MODEL + 3 * 256 - 1) // (3 * 256)) * 256

IN_SPLITS = (512, 128, 128, 128, 128, 128, 128, 24,
             512, 128, 128,
             MLA_Q_RANK, MLA_KV_RANK, MLA_ROPE_DIM,
             512, 512, 512, FOX_HEADS)
IN_NAMES = ("nq", "nkc", "nvc", "nks", "nvs", "nkw", "nvw", "ng",
            "sq", "sk", "sv", "mcq", "mckv", "mkr", "fq", "fk", "fv", "ff")

VMEM_LIMIT_BYTES = 56 * 1024 * 1024

MAIN_ORDER = ("nq", "nks", "nvs", "nkw", "nvw", "sq", "sk", "sv", "fq", "fk", "fv")
MAIN_COLS = 3328
SMALL_COLS = 128


def _cparams(sem):
    return pltpu.CompilerParams(dimension_semantics=sem, vmem_limit_bytes=VMEM_LIMIT_BYTES)


def _rms(x, g):
    return x * lax.rsqrt(jnp.mean(x * x, axis=-1, keepdims=True) + NORM_EPS) * g


def _dot(a, b):
    return jnp.dot(a, b, preferred_element_type=F32)


def _dot_nt(a, b, precision=None):
    return lax.dot_general(a, b, (((1,), (1,)), ((), ())), preferred_element_type=F32,
                           precision=precision)


def _rep(x, n):
    return x if n == 1 else jnp.concatenate([x] * n, axis=1)


F32_COLS = MLA_Q_RANK + MLA_KV_RANK + 4 * HEAD_DIM + SMALL_COLS
MAIN_HEADS = tuple(8 if n in ("nq", "sq", "fq", "fk", "fv") else 2 for n in MAIN_ORDER)
PROJ_CHUNK = 512
FLASH_V_NAMES = ("nvs", "fv")


def _ones_column(rows):
    return jnp.where(lax.broadcasted_iota(jnp.int32, (rows, HEAD_DIM), 1) == 0, 1.0, 0.0)


def _inproj_kernel(x_ref, g_ref, w_ref, zmla_ref, zsmall_ref, cmp_ref, *head_refs):
    tm = x_ref.shape[1]
    h = _rms(x_ref[0], g_ref[...]).astype(BF16)
    n_mla = MLA_Q_RANK + MLA_KV_RANK
    zmla_ref[0] = _dot(h, w_ref[:, :n_mla])
    r = _dot(h, w_ref[:, n_mla:F32_COLS])
    for f in range(4):
        cmp_ref[0, f] = r[:, f * HEAD_DIM:(f + 1) * HEAD_DIM]
    zsmall_ref[0] = r[:, 4 * HEAD_DIM:]

    n_sel = head_refs[MAIN_ORDER.index("nks")].shape[3] - HEAD_DIM
    t = pl.program_id(1) * tm + lax.broadcasted_iota(jnp.int32, (tm, n_sel), 0)
    blk = lax.shift_right_logical(t, NSA_SEL_LEN.bit_length() - 1)
    onehot = jnp.where(blk == lax.broadcasted_iota(jnp.int32, (tm, n_sel), 1), 1.0, 0.0)

    ones_col = _ones_column(tm)
    extra = {"nks": onehot, **{n: ones_col for n in FLASH_V_NAMES}}
    dest = [(ref, hd, extra.get(name)) for ref, nh, name in zip(head_refs, MAIN_HEADS, MAIN_ORDER)
            for hd in range(nh)]
    for c in range(0, MAIN_COLS, PROJ_CHUNK):
        w = min(PROJ_CHUNK, MAIN_COLS - c)
        res = _dot(h, w_ref[:, F32_COLS + c:F32_COLS + c + w])
        for k in range(w // HEAD_DIM):
            ref, hd, appended = dest[c // HEAD_DIM + k]
            piece = res[:, k * HEAD_DIM:(k + 1) * HEAD_DIM]
            if appended is not None:
                piece = jnp.concatenate([piece, appended], axis=-1)
            ref[0, hd] = piece.astype(ref.dtype)


def _inproj(x, g, w, tm=512):
    B, S, D = x.shape
    N = w.shape[1]
    n_sel = S // NSA_SEL_LEN
    assert N == F32_COLS + MAIN_COLS and S % tm == 0
    tok = lambda n: pl.BlockSpec((1, tm, n), lambda b, i: (b, i, 0))
    heads = lambda nh, d: pl.BlockSpec((1, nh, tm, d), lambda b, i: (b, 0, i, 0))
    head_dims = [HEAD_DIM + n_sel if n == "nks" else 2 * HEAD_DIM if n in FLASH_V_NAMES else HEAD_DIM
                 for n in MAIN_ORDER]
    return pl.pallas_call(
        _inproj_kernel,
        grid=(B, S // tm),
        in_specs=[pl.BlockSpec((1, tm, D), lambda b, i: (b, i, 0)),
                  pl.BlockSpec((1, D), lambda b, i: (0, 0)),
                  pl.BlockSpec((D, N), lambda b, i: (0, 0))],
        out_specs=[tok(MLA_Q_RANK + MLA_KV_RANK), tok(SMALL_COLS), heads(4, HEAD_DIM)]
                  + [heads(nh, d) for nh, d in zip(MAIN_HEADS, head_dims)],
        out_shape=[jax.ShapeDtypeStruct((B, S, MLA_Q_RANK + MLA_KV_RANK), F32),
                   jax.ShapeDtypeStruct((B, S, SMALL_COLS), F32),
                   jax.ShapeDtypeStruct((B, 4, S, HEAD_DIM), F32)]
                  + [jax.ShapeDtypeStruct((B, nh, S, d), BF16) for nh, d in zip(MAIN_HEADS, head_dims)],
        compiler_params=_cparams(("parallel", "parallel")),
        name="inproj",
    )(x, g, w)


def _rope_table_kernel(pos_ref, inv_ref, cos_ref, sin_ref):
    ang = pos_ref[...] * inv_ref[...]
    cos_ref[...] = jnp.cos(ang)
    sin_ref[...] = jnp.sin(ang)


def _rope_tables(pos_col, inv_row, tm=512):
    M = pos_col.shape[0]
    N = inv_row.shape[1]
    return pl.pallas_call(
        _rope_table_kernel,
        grid=(M // tm,),
        in_specs=[pl.BlockSpec((tm, 1), lambda i: (i, 0)),
                  pl.BlockSpec((1, N), lambda i: (0, 0))],
        out_specs=[pl.BlockSpec((tm, N), lambda i: (i, 0))] * 2,
        out_shape=[jax.ShapeDtypeStruct((M, N), F32)] * 2,
        compiler_params=_cparams(("parallel",)),
        name="rope_tables",
    )(pos_col, inv_row)


def _mla_prep_kernel(z_ref, zs_ref, cos_ref, sin_ref, qg_ref, kvg_ref, wq_ref, wkv_ref,
                     q_ref, k_ref, v_ref):
    nq = MLA_HEADS * MLA_NOPE_DIM
    nr = MLA_HEADS * MLA_ROPE_DIM
    scale = MLA_QK_DIM ** -0.5
    z = z_ref[0]
    cq = _rms(z[:, :MLA_Q_RANK], qg_ref[...]).astype(BF16)
    ckv = _rms(z[:, MLA_Q_RANK:], kvg_ref[...]).astype(BF16)
    cos = cos_ref[0]
    sin = sin_ref[0]
    q_nope = _dot(cq, wq_ref[:, :nq]) * scale
    q_rot = (_dot(cq, wq_ref[:, nq:nq + nr]) * cos + _dot(cq, wq_ref[:, nq + nr:]) * sin) * scale
    k_nope = _dot(ckv, wkv_ref[:, :nq])
    v = _dot(ckv, wkv_ref[:, nq:])
    zs = zs_ref[0]
    r, d = MLA_ROPE_DIM, MLA_NOPE_DIM
    k_rot = zs[:, :r] * cos[:, :r] + zs[:, r:2 * r] * sin[:, :r]
    ones_col = _ones_column(z.shape[0])
    for h in range(MLA_HEADS):
        q_ref[0, h] = jnp.concatenate([q_nope[:, h * d:(h + 1) * d], q_rot[:, h * r:(h + 1) * r]],
                                      axis=-1).astype(BF16)
        k_ref[0, h] = jnp.concatenate([k_nope[:, h * d:(h + 1) * d], k_rot], axis=-1).astype(BF16)
        v_ref[0, h] = jnp.concatenate([v[:, h * MLA_V_DIM:(h + 1) * MLA_V_DIM], ones_col],
                                      axis=-1).astype(BF16)


def _mla_prep(z_mla, z_small, cos, sin, qg, kvg, wq, wkv, tm=512):
    B, S, _ = z_mla.shape
    nr = MLA_HEADS * MLA_ROPE_DIM
    tok = lambda n: pl.BlockSpec((1, tm, n), lambda b, i: (b, i, 0))
    heads = lambda d: pl.BlockSpec((1, MLA_HEADS, tm, d), lambda b, i: (b, 0, i, 0))
    full = lambda a: pl.BlockSpec(a.shape, lambda b, i: (0,) * a.ndim)
    return pl.pallas_call(
        _mla_prep_kernel,
        grid=(B, S // tm),
        in_specs=[tok(z_mla.shape[2]), tok(SMALL_COLS), tok(nr), tok(nr),
                  full(qg), full(kvg), full(wq), full(wkv)],
        out_specs=[heads(MLA_QK_DIM), heads(MLA_QK_DIM), heads(2 * MLA_V_DIM)],
        out_shape=[jax.ShapeDtypeStruct((B, MLA_HEADS, S, d), BF16)
                   for d in (MLA_QK_DIM, MLA_QK_DIM, 2 * MLA_V_DIM)],
        compiler_params=_cparams(("parallel", "parallel")),
        name="mla_prep",
    )(z_mla, z_small, cos, sin, qg, kvg, wq, wkv)


CUM_CHUNK = 512


def _fox_cumsum_kernel(f_ref, b_ref, u_ref, q_ref, k_ref, o_ref, qn_ref, kn_ref):
    x = f_ref[0] + b_ref[...]
    lf = jnp.minimum(x, 0.0) - jnp.log1p(jnp.exp(-jnp.abs(x)))
    S = lf.shape[1]
    carry = jnp.zeros((lf.shape[0], 1), F32)
    for c in range(0, S, CUM_CHUNK):
        loc = jnp.dot(lf[:, c:c + CUM_CHUNK], u_ref[...], preferred_element_type=F32,
                      precision=lax.Precision.HIGHEST) + carry
        o_ref[0, :, c:c + CUM_CHUNK] = loc
        carry = loc[:, CUM_CHUNK - 1:CUM_CHUNK]
    ones = jnp.ones((8, q_ref.shape[3]), BF16)
    for h in range(q_ref.shape[1]):
        for src, dst in ((q_ref, qn_ref), (k_ref, kn_ref)):
            v = src[0, h]
            dst[0, h:h + 1, :] = _dot_nt(ones, v * v)[0:1]


def _fox_cumsum(f_bhs, bias_col, q, k):
    B, H, S = f_bhs.shape
    Dh = q.shape[3]
    assert S % CUM_CHUNK == 0
    upper = jnp.asarray(np.triu(np.ones((CUM_CHUNK, CUM_CHUNK), np.float32)))
    return pl.pallas_call(
        _fox_cumsum_kernel,
        grid=(B,),
        in_specs=[pl.BlockSpec((1, H, S), lambda b: (b, 0, 0)),
                  pl.BlockSpec((H, 1), lambda b: (0, 0)),
                  pl.BlockSpec((CUM_CHUNK, CUM_CHUNK), lambda b: (0, 0)),
                  pl.BlockSpec((1, H, S, Dh), lambda b: (b, 0, 0, 0)),
                  pl.BlockSpec((1, H, S, Dh), lambda b: (b, 0, 0, 0))],
        out_specs=[pl.BlockSpec((1, H, S), lambda b: (b, 0, 0))] * 3,
        out_shape=[jax.ShapeDtypeStruct((B, H, S), F32)] * 3,
        compiler_params=_cparams(("parallel",)),
        name="fox_cumsum",
    )(f_bhs, bias_col, upper, q, k)


def _nsa_compress_kernel(r_ref, pe_ref, w_ref, o_ref):
    r = r_ref[0, 0]
    first = _dot((r + pe_ref[0, 0:1, :]).astype(BF16), w_ref[0, 0])
    second = _dot((r + pe_ref[0, 1:2, :]).astype(BF16), w_ref[0, 1])
    nc = r.shape[0]
    nxt = pltpu.roll(second, nc - 1, 0)
    row = lax.broadcasted_iota(jnp.int32, (nc, 1), 0)
    o_ref[0, 0] = jnp.where(row < nc - 1, first + nxt, 0.0).astype(o_ref.dtype)


def _nsa_compress(rows, pe, w):
    B, F, NC, K = rows.shape
    return pl.pallas_call(
        _nsa_compress_kernel,
        grid=(B, F),
        in_specs=[pl.BlockSpec((1, 1, NC, K), lambda b, f: (b, f, 0, 0)),
                  pl.BlockSpec((1, 2, K), lambda b, f: (f // NSA_KV_HEADS, 0, 0)),
                  pl.BlockSpec((1, 2, K, HEAD_DIM), lambda b, f: (f // NSA_KV_HEADS, 0, 0, 0))],
        out_specs=pl.BlockSpec((1, 1, NC, HEAD_DIM), lambda b, f: (b, f, 0, 0)),
        out_shape=jax.ShapeDtypeStruct((B, F, NC, HEAD_DIM), BF16),
        compiler_params=_cparams(("parallel", "parallel")),
        name="nsa_compress",
    )(rows, pe, w)


NSA_CMP_TQ = 512


def _nsa_cmp_kernel(q_ref, kc_ref, vc_ref, pq_ref, pc_ref, sl_ref, gate_ref, ovt_ref,
                    oc_ref, selt_ref, any_ref):
    G, tq, Dh = q_ref.shape[2:]
    R = G * tq
    i = pl.program_id(2)
    q = q_ref[0, 0].reshape(R, Dh)
    kc = kc_ref[0, 0]
    nc = kc.shape[0]
    n_sel = ovt_ref.shape[0]
    s = _dot_nt(q, kc)
    pq = jnp.concatenate([jnp.broadcast_to(pq_ref[0], (tq, LANES))] * G, axis=0)
    s = s - _rep(sl_ref[0], nc // LANES) * jnp.abs(_rep(pq, nc // LANES) - pc_ref[0])
    t_row = (i * tq + lax.broadcasted_iota(jnp.int32, (G, tq, nc), 1)).reshape(R, nc)
    c_col = lax.broadcasted_iota(jnp.int32, (1, nc), 1)
    c_end = jnp.where(c_col < nc - 1, c_col * NSA_CMP_STRIDE + (NSA_CMP_LEN - 1), jnp.int32(2 ** 30))
    s = jnp.where(c_end <= t_row, s, NEG_INF)
    m = jnp.max(s, axis=-1, keepdims=True)
    e = jnp.exp(s - m)
    l = jnp.sum(e, axis=-1, keepdims=True)
    p = e * jnp.where(m > 0.5 * NEG_INF, 1.0 / l, 0.0)
    o = _dot(p.astype(BF16), vc_ref[0, 0])
    oc_ref[0, 0] = (_gate_column(gate_ref[0, 0], 0, G) * o).reshape(G, tq, Dh)
    p_sum = jnp.sum(p.reshape(G, tq, nc), axis=0)
    imp = _dot_nt(ovt_ref[...], p_sum, precision=lax.Precision.HIGHEST)
    j = lax.broadcasted_iota(jnp.int32, (n_sel, 1), 0)

    def select(imp_c, t0):
        t_q = t0 + lax.broadcasted_iota(jnp.int32, (1, LANES), 1)
        jt = lax.shift_right_logical(t_q, NSA_SEL_LEN.bit_length() - 1)
        forced = (j == 0) | (j == jt) | (j == jt - 1)
        causal = j <= jt
        val = jnp.where(forced, FORCE_SCORE, imp_c)
        val = jnp.where(causal, val, NEG_INF)
        sel = jnp.zeros((n_sel, LANES), jnp.int32)
        for _ in range(min(NSA_TOP_N, n_sel)):
            best = jnp.max(val, axis=0, keepdims=True)
            idx = jnp.min(jnp.where(val == best, j, n_sel), axis=0, keepdims=True)
            pick = j == idx
            sel = jnp.where(pick, 1, sel)
            val = jnp.where(pick, -jnp.inf, val)
        return (sel > 0) & causal

    keep = jnp.concatenate([select(imp[:, c:c + LANES], i * tq + c) for c in range(0, tq, LANES)], axis=1)
    selt_ref[0, 0] = jnp.where(keep, 0.0, NEG_INF).astype(selt_ref.dtype)
    per_tile = any_ref.shape[2]
    any_ref[0, 0] = jnp.max(jnp.where(keep, 1.0, 0.0).reshape(per_tile, n_sel // per_tile, tq), axis=1)


def _nsa_cmp(q, kvc, pos_col, pc_row, slopes, gate, overlap_t, n_tiles):
    B, Hk, G, S, Dh = q.shape
    NC = kvc.shape[2]
    n_sel = overlap_t.shape[0]
    assert NSA_SEL_LEN & (NSA_SEL_LEN - 1) == 0 and n_sel % n_tiles == 0
    tq = NSA_CMP_TQ
    R = G * tq
    return pl.pallas_call(
        _nsa_cmp_kernel,
        grid=(B, Hk, S // tq),
        in_specs=[pl.BlockSpec((1, 1, G, tq, Dh), lambda b, h, i: (b, h, 0, i, 0)),
                  pl.BlockSpec((1, 1, NC, Dh), lambda b, h, i: (b, h, 0, 0)),
                  pl.BlockSpec((1, 1, NC, Dh), lambda b, h, i: (b, h + Hk, 0, 0)),
                  pl.BlockSpec((1, tq, 1), lambda b, h, i: (b, i, 0)),
                  pl.BlockSpec((1, 1, NC), lambda b, h, i: (b, 0, 0)),
                  pl.BlockSpec((1, R, LANES), lambda b, h, i: (h, 0, 0)),
                  pl.BlockSpec((1, 1, tq, 3 * G), lambda b, h, i: (b, h, i, 0)),
                  pl.BlockSpec((n_sel, NC), lambda b, h, i: (0, 0))],
        out_specs=[pl.BlockSpec((1, 1, G, tq, Dh), lambda b, h, i: (b, h, 0, i, 0)),
                   pl.BlockSpec((1, 1, n_sel, tq), lambda b, h, i: (b, h, 0, i)),
                   pl.BlockSpec((1, 1, n_tiles, tq), lambda b, h, i: (b, h, 0, i))],
        out_shape=[jax.ShapeDtypeStruct((B, Hk, G, S, Dh), F32),
                   jax.ShapeDtypeStruct((B, Hk, n_sel, S), BF16),
                   jax.ShapeDtypeStruct((B, Hk, n_tiles, S), F32)],
        compiler_params=_cparams(("parallel", "parallel", "parallel")),
        name="nsa_cmp",
    )(q, kvc, kvc, pos_col, pc_row, slopes, gate, overlap_t)


class AttnCfg(NamedTuple):
    tq: int
    tk: int = 0
    window: int = 0
    alibi: bool = False
    decay: bool = False
    sink: bool = False
    gate: int = -1
    n_sub: int = 1
    add: bool = False
    selaug: bool = False
    sparse: bool = False
    out_dtype: object = BF16


def _gate_column(gates, branch, G):
    cols = [gates[:, 3 * g + branch:3 * g + branch + 1] for g in range(G)]
    return jax.nn.sigmoid(jnp.concatenate(cols, axis=0))


def _epilogue(cfg, out, gates, add, G, tq):
    R, Dv = out.shape
    if cfg.gate >= 0:
        out = out * _gate_column(gates, cfg.gate, G)
    out = out.reshape(G, tq, Dv)
    if cfg.add:
        out = out + add
    return out


def _flash_kernel(*refs, cfg):
    it = iter(refs)
    cnt_ref = tiles_ref = None
    if cfg.sparse:
        cnt_ref, tiles_ref = next(it), next(it)
    q_ref, k_ref, v_ref, dmask_ref = next(it), next(it), next(it), next(it)
    pq_ref = pk_ref = sl_ref = cq_ref = ck_ref = gate_ref = add_ref = None
    selb_ref = next(it) if cfg.selaug else None
    if cfg.alibi:
        pq_ref, pk_ref, sl_ref = next(it), next(it), next(it)
    if cfg.decay:
        cq_ref, ck_ref = next(it), next(it)
    if cfg.gate >= 0:
        gate_ref = next(it)
    if cfg.add:
        add_ref = next(it)
    o_ref = next(it)
    m_s, acc_s, sa_s, sb_s = (next(it) for _ in range(4))
    pq_s = next(it) if cfg.alibi else None
    cq_s = next(it) if cfg.decay else None

    G, tq, Dq = q_ref.shape[2:]
    Da = v_ref.shape[3]
    Dv = Da // 2
    tk = cfg.tk
    nrep = tk // LANES
    R = G * tq
    i = pl.program_id(2)
    q = q_ref[0, 0].reshape(R, Dq)
    if cfg.selaug:
        bias = jnp.concatenate([selb_ref[0, 0].astype(F32)] * G, axis=0)
        q = jnp.concatenate([q.astype(F32), bias], axis=-1).astype(BF16)

    m_s[...] = jnp.full((R, LANES), NEG_INF, F32)
    acc_s[...] = jnp.zeros((R, Da), F32)
    if cfg.alibi:
        pq_s[...] = jnp.broadcast_to(jnp.concatenate([pq_ref[0]] * G, axis=0), (R, LANES))
    if cfg.decay:
        cq_all = cq_ref[0]
        head = lax.broadcasted_iota(jnp.int32, cq_all.shape, 1) == pl.program_id(1)
        cq_s[...] = jnp.broadcast_to(jnp.sum(jnp.where(head, cq_all, 0.0), axis=-1, keepdims=True), (R, LANES))

    def scores(j, s_ref):
        k0 = pl.multiple_of(j * tk, tk)
        s = _dot_nt(q, k_ref[0, 0, pl.ds(k0, tk), :])
        if cfg.alibi:
            s = s - _rep(sl_ref[0], nrep) * jnp.abs(_rep(pq_s[...], nrep) - pk_ref[0, j])
        if cfg.decay:
            s = s + (_rep(cq_s[...], nrep) - ck_ref[0, 0, j])
        s_ref[...] = s

    def update(j, s_ref, mask_variant=None):
        s = s_ref[...]
        if mask_variant is not None:
            s = s + dmask_ref[mask_variant]
        m_prev = m_s[...]
        m_new = jnp.maximum(m_prev, jnp.max(s, axis=-1, keepdims=True))
        alpha = jnp.exp(m_prev - m_new)
        p = jnp.exp((s - _rep(m_new, nrep)).astype(BF16))
        k0 = pl.multiple_of(j * tk, tk)
        acc_s[...] = _rep(alpha, Da // LANES) * acc_s[...] + _dot(p, v_ref[0, 0, pl.ds(k0, tk), :])
        m_s[...] = m_new

    diag = (i * tq) // tk
    n_diag = max(1, tq // tk)
    variants = [i % (tk // tq)] if tq < tk else list(range(n_diag))
    if cfg.sparse:
        step_id = (pl.program_id(0) * pl.num_programs(1) + pl.program_id(1)) * pl.num_programs(2) + i
        n_vis = cnt_ref[step_id]
        nk = k_ref.shape[2] // tk

        def tile_at(pos):
            return jnp.where(pos < n_vis, tiles_ref[step_id * nk + jnp.minimum(pos, nk - 1)], diag)
    else:
        n_vis = diag

        def tile_at(pos):
            return pos

    scores(tile_at(0), sa_s)

    def pair(pp, c):
        pos = 2 * pp
        scores(tile_at(pos + 1), sb_s)
        update(tile_at(pos), sa_s)
        scores(tile_at(pos + 2), sa_s)
        update(tile_at(pos + 1), sb_s)
        return c
    lax.fori_loop(0, n_vis // 2, pair, 0)

    def tail(leftover):
        items = ([(tile_at(n_vis - 1), None)] if leftover else []) + \
                [(diag + d, variants[d]) for d in range(n_diag)]
        bufs = (sa_s, sb_s)
        for k, (tile, variant) in enumerate(items):
            if k + 1 < len(items):
                scores(items[k + 1][0], bufs[(k + 1) % 2])
            update(tile, bufs[k % 2], variant)

    @pl.when(n_vis % 2 == 1)
    def _():
        tail(True)

    @pl.when(n_vis % 2 == 0)
    def _():
        tail(False)

    acc = acc_s[...]
    out = acc[:, :Dv] / acc[:, Dv:Dv + 1]
    out = _epilogue(cfg, out, gate_ref[0, 0] if cfg.gate >= 0 else None, add_ref[0, 0] if cfg.add else None,
                    G, tq)
    o_ref[0, 0] = out.astype(o_ref.dtype)


def _flash(cfg, q, k, v, *, visit=None, selb=None, pos_col=None, pos_rows=None, slopes=None,
           cum_col=None, cum_rows=None, gate=None, add=None):
    B, Hk, G, S, Dq = q.shape
    Dk, Da = k.shape[3], v.shape[3]
    Dv = Da // 2
    tq, tk = cfg.tq, cfg.tk
    assert S % tq == 0 and S % tk == 0 and (tk % tq == 0 or tq % tk == 0) and tk % LANES == 0
    assert Da % LANES == 0
    assert Dk == Dq + (selb.shape[3] if cfg.selaug else 0) and cfg.sparse == (visit is not None)
    R = G * tq
    nk = S // tk
    qmap = lambda b, h, i, *_: (b, h, 0, i, 0)
    kvmap = lambda b, h, i, *_: (b, h, 0, 0)
    t_local = np.tile(np.arange(tq), G)[None, :, None]
    first = (np.arange(tk // tq) * tq if tq < tk else -np.arange(tq // tk) * tk)[:, None, None]
    dmask = jnp.asarray(np.where(np.arange(tk)[None, None, :] <= first + t_local, 0.0, NEG_INF), dtype=F32)
    args = [q, k, v, dmask]
    specs = [pl.BlockSpec((1, 1, G, tq, Dq), qmap),
             pl.BlockSpec((1, 1, S, Dk), kvmap),
             pl.BlockSpec((1, 1, S, Da), kvmap),
             pl.BlockSpec(dmask.shape, lambda b, h, i, *_: (0, 0, 0), pipeline_mode=pl.Buffered(1))]
    scratch = [pltpu.VMEM((R, LANES), F32), pltpu.VMEM((R, Da), F32),
               pltpu.VMEM((R, tk), F32), pltpu.VMEM((R, tk), F32)]
    if cfg.selaug:
        args += [selb]
        specs += [pl.BlockSpec((1, 1, tq, Dk - Dq), lambda b, h, i, *_: (b, h, i, 0))]
    if cfg.alibi:
        args += [pos_col, pos_rows, slopes]
        specs += [pl.BlockSpec((1, tq, 1), lambda b, h, i, *_: (b, i, 0)),
                  pl.BlockSpec((1, nk, 1, tk), lambda b, h, i, *_: (b, 0, 0, 0)),
                  pl.BlockSpec((1, R, LANES), lambda b, h, i, *_: (h, 0, 0))]
        scratch.append(pltpu.VMEM((R, LANES), F32))
    if cfg.decay:
        args += [cum_col, cum_rows]
        assert G == 1
        specs += [pl.BlockSpec((1, tq, Hk), lambda b, h, i, *_: (b, i, 0)),
                  pl.BlockSpec((1, 1, nk, 1, tk), lambda b, h, i, *_: (b, h, 0, 0, 0))]
        scratch.append(pltpu.VMEM((R, LANES), F32))
    if cfg.gate >= 0:
        args += [gate]
        specs += [pl.BlockSpec((1, 1, tq, 3 * G), lambda b, h, i, *_: (b, h, i, 0))]
    if cfg.add:
        args += [add]
        specs += [pl.BlockSpec((1, 1, G, tq, Dv), qmap)]
    grid_spec = pltpu.PrefetchScalarGridSpec(
        num_scalar_prefetch=2 if cfg.sparse else 0,
        grid=(B, Hk, S // tq),
        in_specs=specs,
        out_specs=pl.BlockSpec((1, 1, G, tq, Dv), qmap),
        scratch_shapes=scratch)
    return pl.pallas_call(
        functools.partial(_flash_kernel, cfg=cfg),
        grid_spec=grid_spec,
        out_shape=jax.ShapeDtypeStruct((B, Hk, G, S, Dv), cfg.out_dtype),
        compiler_params=_cparams(("parallel", "parallel", "parallel")),
        name="flash",
    )(*(list(visit) if cfg.sparse else []), *args)


BAND_VISIBLE, BAND_CUT_BELOW, BAND_CUT_ABOVE, BAND_HIDDEN = range(4)


def _band_kernel(*refs, cfg):
    it = iter(refs)
    q_ref, k_ref, v_ref, pq_ref, pk_ref, sl_ref, mask_ref = (next(it) for _ in range(7))
    sink_ref = next(it) if cfg.sink else None
    gate_ref = next(it) if cfg.gate >= 0 else None
    add_ref = next(it) if cfg.add else None
    o_ref = next(it)

    G, Dq = q_ref.shape[2], q_ref.shape[4]
    tq = cfg.tq
    R = G * tq
    W = cfg.window
    span = W + tq
    slope = sl_ref[0]
    for sub in range(cfg.n_sub):
        i = pl.program_id(2) * cfg.n_sub + sub
        tok = slice(sub * tq, (sub + 1) * tq)
        start = pl.multiple_of(jnp.maximum(i * tq - W, 0), LANES)
        blk0 = lax.shift_right_logical(start, LANES.bit_length() - 1)
        q = q_ref[0, 0, :, tok, :].reshape(R, Dq)
        s = _dot_nt(q, k_ref[0, 0, pl.ds(start, span), :])
        pq = jnp.concatenate([pq_ref[0, tok, :]] * G, axis=0)
        off = i * tq - start
        chunks = []
        for c in range(span // LANES):
            sc = s[:, c * LANES:(c + 1) * LANES]
            sc = sc - slope * jnp.abs(pq - pk_ref[0, blk0 + c])
            lo = c * LANES - off
            kind = jnp.where(lo == -W, BAND_CUT_BELOW,
                             jnp.where(lo == 0, BAND_CUT_ABOVE, jnp.where(lo > 0, BAND_HIDDEN, BAND_VISIBLE)))
            chunks.append(sc + mask_ref[kind])
        s = jnp.concatenate(chunks, axis=1)
        m = jnp.max(s, axis=-1, keepdims=True)
        e = jnp.exp(s - m)
        l = jnp.sum(e, axis=-1, keepdims=True)
        acc = _dot(e.astype(BF16), v_ref[0, 0, pl.ds(start, span), :])
        if cfg.sink:
            sink = sink_ref[0]
            m_f = jnp.maximum(m, sink)
            w = jnp.exp(m - m_f)
            out = acc * (w / (l * w + jnp.exp(sink - m_f)))
        else:
            out = acc / l
        out = _epilogue(cfg, out, gate_ref[0, 0, tok, :] if cfg.gate >= 0 else None,
                        add_ref[0, 0, :, tok, :] if cfg.add else None, G, tq)
        o_ref[0, 0, :, tok, :] = out.astype(o_ref.dtype)


def _band(cfg, q, k, v, *, pos_col, pos_rows, slopes, sinks=None, gate=None, add=None):
    B, Hk, G, S, Dq = q.shape
    Dv = v.shape[3]
    tq = cfg.tq
    T = tq * cfg.n_sub
    assert tq == LANES and cfg.window % LANES == 0 and S % T == 0 and S >= cfg.window + tq
    R = G * tq
    qmap = lambda b, h, i: (b, h, 0, i, 0)
    kvmap = lambda b, h, i: (b, h, 0, 0)
    d = np.tile(np.arange(tq), G)[:, None] - np.arange(LANES)[None, :]
    masks = np.zeros((4, R, LANES), np.float32)
    masks[BAND_CUT_BELOW] = np.where(d < 0, 0.0, NEG_INF)
    masks[BAND_CUT_ABOVE] = np.where(d >= 0, 0.0, NEG_INF)
    masks[BAND_HIDDEN] = NEG_INF
    args = [q, k, v, pos_col, pos_rows, slopes, jnp.asarray(masks)]
    specs = [pl.BlockSpec((1, 1, G, T, Dq), qmap),
             pl.BlockSpec((1, 1, S, Dq), kvmap),
             pl.BlockSpec((1, 1, S, Dv), kvmap),
             pl.BlockSpec((1, T, 1), lambda b, h, i: (b, i, 0)),
             pl.BlockSpec((1, S // LANES, 1, LANES), lambda b, h, i: (b, 0, 0, 0)),
             pl.BlockSpec((1, R, 1), lambda b, h, i: (h, 0, 0)),
             pl.BlockSpec((4, R, LANES), lambda b, h, i: (0, 0, 0))]
    if cfg.sink:
        args += [sinks]
        specs += [pl.BlockSpec((1, R, 1), lambda b, h, i: (h, 0, 0))]
    if cfg.gate >= 0:
        args += [gate]
        specs += [pl.BlockSpec((1, 1, T, 3 * G), lambda b, h, i: (b, h, i, 0))]
    if cfg.add:
        args += [add]
        specs += [pl.BlockSpec((1, 1, G, T, Dv), qmap)]
    return pl.pallas_call(
        functools.partial(_band_kernel, cfg=cfg),
        grid=(B, Hk, S // T),
        in_specs=specs,
        out_specs=pl.BlockSpec((1, 1, G, T, Dv), qmap),
        out_shape=jax.ShapeDtypeStruct((B, Hk, G, S, Dv), cfg.out_dtype),
        compiler_params=_cparams(("parallel", "parallel", "parallel")),
        name="band",
    )(*args)


def _merge_kernel(x_ref, y0_ref, y1_ref, y2_ref, y3_ref, gpre_ref, wg_ref, wb_ref, wo_ref, gpost_ref,
                  o_ref):
    x = x_ref[0]
    h = _rms(x, gpre_ref[...]).astype(BF16)
    merged = None
    for b, y_ref in enumerate((y0_ref, y1_ref, y2_ref, y3_ref)):
        gate = jax.nn.sigmoid(_dot(h, wg_ref[b]))
        y = jnp.concatenate([y_ref[0, hd] for hd in range(y_ref.shape[1])], axis=-1)
        br = _dot(y, wb_ref[b])
        merged = gate * br if merged is None else merged + gate * br
    y = _dot(merged.astype(BF16), wo_ref[...])
    o_ref[0] = x + _rms(y, gpost_ref[...])


def _merge(x, ys, gpre, wg, wb, wo, gpost, tm=512):
    B, S, D = x.shape
    full = lambda a: pl.BlockSpec(a.shape, lambda b, i: (0,) * a.ndim, pipeline_mode=pl.Buffered(1))
    yspec = lambda y: pl.BlockSpec((1, y.shape[1], tm, y.shape[3]), lambda b, i: (b, 0, i, 0))
    return pl.pallas_call(
        _merge_kernel,
        grid=(B, S // tm),
        in_specs=[pl.BlockSpec((1, tm, D), lambda b, i: (b, i, 0))] + [yspec(y) for y in ys]
                 + [full(gpre), full(wg), full(wb), full(wo), full(gpost)],
        out_specs=pl.BlockSpec((1, tm, D), lambda b, i: (b, i, 0)),
        out_shape=jax.ShapeDtypeStruct((B, S, D), F32),
        compiler_params=_cparams(("parallel", "parallel")),
        name="merge",
    )(x, *ys, gpre, wg, wb, wo, gpost)


def _mem_kv_kernel(mem_ref, g_ref, w_ref, o_ref):
    o_ref[0] = _dot(_rms(mem_ref[0], g_ref[...]).astype(BF16), w_ref[...]).astype(BF16)


def _mem_kv(mem, g, w):
    B, Mm, D = mem.shape
    N = w.shape[1]
    return pl.pallas_call(
        _mem_kv_kernel,
        grid=(B,),
        in_specs=[pl.BlockSpec((1, Mm, D), lambda b: (b, 0, 0)),
                  pl.BlockSpec((1, D), lambda b: (0, 0)),
                  pl.BlockSpec((D, N), lambda b: (0, 0))],
        out_specs=pl.BlockSpec((1, Mm, N), lambda b: (b, 0, 0)),
        out_shape=jax.ShapeDtypeStruct((B, Mm, N), BF16),
        compiler_params=_cparams(("parallel",)),
        name="mem_kv",
    )(mem, g, w)


def _xattn_kernel(x_ref, kv_ref, gpre_ref, wq_ref, wo_ref, gpost_ref, o_ref):
    x = x_ref[0]
    h = _rms(x, gpre_ref[...]).astype(BF16)
    q_all = _dot(h, wq_ref[...])
    outs = []
    for hd in range(XA_HEADS):
        q = q_all[:, hd * HEAD_DIM:(hd + 1) * HEAD_DIM].astype(BF16)
        s = _dot_nt(q, kv_ref[0, hd])
        m = jnp.max(s, axis=-1, keepdims=True)
        e = jnp.exp(s - m)
        p = e / jnp.sum(e, axis=-1, keepdims=True)
        outs.append(_dot(p.astype(BF16), kv_ref[0, XA_HEADS + hd]))
    y = _dot(jnp.concatenate(outs, axis=-1).astype(BF16), wo_ref[...])
    o_ref[0] = x + _rms(y, gpost_ref[...])


def _xattn(x, kv, gpre, wq, wo, gpost, tm=1024):
    B, S, D = x.shape
    full = lambda a: pl.BlockSpec(a.shape, lambda b, i: (0,) * a.ndim)
    return pl.pallas_call(
        _xattn_kernel,
        grid=(B, S // tm),
        in_specs=[pl.BlockSpec((1, tm, D), lambda b, i: (b, i, 0)),
                  pl.BlockSpec((1,) + kv.shape[1:], lambda b, i: (b, 0, 0, 0)),
                  full(gpre), full(wq), full(wo), full(gpost)],
        out_specs=pl.BlockSpec((1, tm, D), lambda b, i: (b, i, 0)),
        out_shape=jax.ShapeDtypeStruct((B, S, D), F32),
        compiler_params=_cparams(("parallel", "parallel")),
        name="xattn",
    )(x, kv, gpre, wq, wo, gpost)


FFN_CHUNK = 256


def _ffn_kernel(x_ref, gpre_ref, wg_ref, wu_ref, wo_ref, gpost_ref, o_ref):
    x = x_ref[...]
    h = _rms(x, gpre_ref[...]).astype(BF16)
    y = None
    for c in range(0, D_FF, FFN_CHUNK):
        g = _dot(h, wg_ref[:, c:c + FFN_CHUNK])
        u = _dot(h, wu_ref[:, c:c + FFN_CHUNK])
        a = (g * jax.nn.sigmoid(g) * u).astype(BF16)
        part = _dot(a, wo_ref[c:c + FFN_CHUNK, :])
        y = part if y is None else y + part
    o_ref[...] = x + _rms(y, gpost_ref[...])


def _ffn(x2d, gpre, wg, wu, wo, gpost, tm=512):
    M, D = x2d.shape
    full = lambda a: pl.BlockSpec(a.shape, lambda i: (0,) * a.ndim, pipeline_mode=pl.Buffered(1))
    return pl.pallas_call(
        _ffn_kernel,
        grid=(M // tm,),
        in_specs=[pl.BlockSpec((tm, D), lambda i: (i, 0)),
                  full(gpre), full(wg), full(wu), full(wo), full(gpost)],
        out_specs=pl.BlockSpec((tm, D), lambda i: (i, 0)),
        out_shape=jax.ShapeDtypeStruct((M, D), F32),
        compiler_params=_cparams(("parallel",)),
        name="ffn",
    )(x2d, gpre, wg, wu, wo, gpost)


def _alibi_slopes(n):
    return (2.0 ** (-8.0 * np.arange(1, n + 1) / n)).astype(np.float32)


def _rotate_half_cols(w):
    half = w.shape[-1] // 2
    return jnp.concatenate([-w[..., half:], w[..., :half]], axis=-1)


def _prep_weights(p):
    L = p["w_in"].shape[0]
    offs = np.concatenate([[0], np.cumsum(IN_SPLITS)])
    col = {n: p["w_in"][:, :, offs[k]:offs[k + 1]] for k, n in enumerate(IN_NAMES)}
    qk_scale = HEAD_DIM ** -0.5
    for n in ("nq", "sq", "fq"):
        col[n] = col[n] * qk_scale
    pad = jnp.zeros((L, D_MODEL, SMALL_COLS - 2 * MLA_ROPE_DIM - 24 - FOX_HEADS), F32)
    small = jnp.concatenate([col["mkr"], _rotate_half_cols(col["mkr"]), col["ng"], col["ff"], pad], -1)
    w_all = jnp.concatenate([col["mcq"], col["mckv"], col["nkc"], col["nvc"], small]
                            + [col[n] for n in MAIN_ORDER], axis=-1).astype(BF16)

    wq = p["mla_w_q_up"].reshape(L, MLA_Q_RANK, MLA_HEADS, MLA_QK_DIM)
    wq_nope = wq[..., :MLA_NOPE_DIM].reshape(L, MLA_Q_RANK, -1)
    wq_rot = wq[..., MLA_NOPE_DIM:]
    wq_all = jnp.concatenate([wq_nope, wq_rot.reshape(L, MLA_Q_RANK, -1),
                              _rotate_half_cols(wq_rot).reshape(L, MLA_Q_RANK, -1)], -1).astype(BF16)
    wkv = p["mla_w_kv_up"].reshape(L, MLA_KV_RANK, MLA_HEADS, MLA_NOPE_DIM + MLA_V_DIM)
    wkv_all = jnp.concatenate([wkv[..., :MLA_NOPE_DIM].reshape(L, MLA_KV_RANK, -1),
                               wkv[..., MLA_NOPE_DIM:].reshape(L, MLA_KV_RANK, -1)], -1).astype(BF16)

    half = NSA_CMP_LEN // 2
    cmp_w = p["nsa_cmp_w"].reshape(L, 2, 2, half * HEAD_DIM, HEAD_DIM).astype(BF16)
    cmp_pe = p["nsa_cmp_pe"].reshape(L, 2, 2, half * HEAD_DIM)

    xa_wq = (p["xa_w_q"] * qk_scale).astype(BF16)
    xa_wo = p["xa_w_o"].astype(BF16)

    row = lambda a: a[:, None, :]
    return dict(
        w_all=w_all, mix_norm_pre=row(p["mix_norm_pre"]),
        wq_all=wq_all, wkv_all=wkv_all, mla_q_norm=row(p["mla_q_norm"]), mla_kv_norm=row(p["mla_kv_norm"]),
        cmp_w=cmp_w, cmp_pe=cmp_pe, swa_sinks=p["swa_sinks"], fox_f_bias=p["fox_f_bias"][:, :, None],
        w_gate=p["w_gate"].astype(BF16), w_branch=p["w_branch"].astype(BF16), w_out=p["w_out"].astype(BF16),
        mix_norm_post=row(p["mix_norm_post"]),
        xa_norm_pre=row(p["xa_norm_pre"]), xa_mem_norm=row(p["xa_mem_norm"]), xa_wq=xa_wq,
        xa_wkv=p["xa_w_kv"].astype(BF16), xa_wo=xa_wo, xa_norm_post=row(p["xa_norm_post"]),
        ffn_norm_pre=row(p["ffn_norm_pre"]), ffn_wg=p["ffn_w_in"][:, :, :D_FF].astype(BF16),
        ffn_wu=p["ffn_w_in"][:, :, D_FF:].astype(BF16), ffn_wo=p["ffn_w_out"].astype(BF16),
        ffn_norm_post=row(p["ffn_norm_post"]),
    )


FLASH_TQ = 512
FLASH_TK = 512
BAND_TQ = LANES
NSA_BAND_SUB = 8
SWA_BAND_SUB = 16


def _per_row(vals_hg, tq, lanes=1):
    Hk, G = vals_hg.shape
    return jnp.broadcast_to(vals_hg[:, :, None, None], (Hk, G, tq, lanes)).reshape(Hk, G * tq, lanes)


def _visit_lists(needed, tq, tk):
    nq, nk = needed.shape[2:]
    diag = (np.arange(nq)[:, None] * tq) // tk
    needed = needed & jnp.asarray(np.arange(nk)[None, :] < diag)
    count = jnp.sum(needed, axis=-1).astype(jnp.int32)
    tiles = jnp.argsort(~needed, axis=-1, stable=True).astype(jnp.int32)
    return count.reshape(-1), tiles.reshape(-1)


ZERO_WEIGHT_LOGIT = -110.0


def _fox_tiles_needed(q_sq, k_sq, cum, tq, tk):
    B, H, S = cum.shape
    qn = jnp.sqrt(q_sq)
    kmax = jnp.sqrt(jnp.max(k_sq, axis=-1))
    row_bound = cum + 2.02 * qn * kmax[..., None]
    tile_bound = jnp.max(row_bound.reshape(B, H, S // tq, tq), axis=-1)
    cum_last = cum[:, :, tk - 1::tk]
    return tile_bound[..., None] - cum_last[:, :, None, :] > ZERO_WEIGHT_LOGIT


def _layer(x, w, c):
    B, S, D = x.shape
    M = B * S
    Hk, G, Dh = NSA_KV_HEADS, NSA_GROUP, HEAD_DIM

    z_mla, z_small, z_cmp, *heads = _inproj(x, w["mix_norm_pre"], w["w_all"])
    hm = dict(zip(MAIN_ORDER, heads))
    grouped = lambda t: t.reshape(B, Hk, G, S, t.shape[-1])
    single = lambda t: t[:, :, None]
    flat = lambda o: o.reshape(B, -1, S, o.shape[-1])

    ng = z_small[:, :, 64:88].reshape(B, S, Hk, 3 * G).transpose(0, 2, 1, 3)
    nq = grouped(hm["nq"])
    rows = z_cmp.reshape(B, 2 * Hk, S // NSA_CMP_STRIDE, NSA_CMP_STRIDE * Dh)
    kvc = _nsa_compress(rows, w["cmp_pe"], w["cmp_w"])
    nk = S // FLASH_TK
    oc, selt, tile_any = _nsa_cmp(nq, kvc, c["pos_col"], c["pc_row"], c["nsa_slopes_cmp"], ng,
                                  c["overlap_t"], nk)
    selb = selt.transpose(0, 1, 3, 2)
    sel_tq = FLASH_TQ // G
    used = tile_any.reshape(B, Hk, nk, S // sel_tq, sel_tq).max(-1).transpose(0, 1, 3, 2) > 0.0
    o_sel = _flash(
        AttnCfg(tq=sel_tq, tk=FLASH_TK, alibi=True, gate=1, add=True, selaug=True, sparse=True,
                out_dtype=F32),
        nq, hm["nks"], hm["nvs"], visit=_visit_lists(used, sel_tq, FLASH_TK), selb=selb,
        pos_col=c["pos_col"], pos_rows=c["pos_rows_flash"], slopes=c["nsa_slopes_flash"], gate=ng, add=oc)
    y_nsa = _band(
        AttnCfg(tq=BAND_TQ, n_sub=NSA_BAND_SUB, window=NSA_WINDOW, alibi=True, gate=2, add=True),
        nq, hm["nkw"], hm["nvw"], pos_col=c["pos_col"], pos_rows=c["pos_rows_band"],
        slopes=c["nsa_slopes_band"], gate=ng, add=o_sel)

    y_swa = _band(
        AttnCfg(tq=BAND_TQ, n_sub=SWA_BAND_SUB, window=SWA_WINDOW, alibi=True, sink=True),
        grouped(hm["sq"]), hm["sk"], hm["sv"], pos_col=c["pos_col"],
        pos_rows=c["pos_rows_band"], slopes=c["swa_slopes_band"],
        sinks=_per_row(w["swa_sinks"].reshape(SWA_KV_HEADS, SWA_GROUP), BAND_TQ))

    q_mla, k_mla, v_mla = _mla_prep(z_mla, z_small, c["cos"], c["sin"], w["mla_q_norm"], w["mla_kv_norm"],
                                    w["wq_all"], w["wkv_all"])
    y_mla = _flash(AttnCfg(tq=FLASH_TQ, tk=FLASH_TK), single(q_mla), k_mla, v_mla)

    f_logit = z_small[:, :, 88:96].transpose(0, 2, 1)
    cum, fq_sq, fk_sq = _fox_cumsum(f_logit, w["fox_f_bias"], hm["fq"], hm["fk"])
    y_fox = _flash(
        AttnCfg(tq=FLASH_TQ, tk=FLASH_TK, decay=True, sparse=True), single(hm["fq"]), hm["fk"], hm["fv"],
        visit=_visit_lists(_fox_tiles_needed(fq_sq, fk_sq, cum, FLASH_TQ, FLASH_TK), FLASH_TQ, FLASH_TK),
        cum_col=cum.transpose(0, 2, 1), cum_rows=cum.reshape(B, FOX_HEADS, S // FLASH_TK, 1, FLASH_TK))

    x1 = _merge(x, [flat(y_nsa), flat(y_swa), flat(y_mla), flat(y_fox)], w["mix_norm_pre"], w["w_gate"],
                w["w_branch"], w["w_out"], w["mix_norm_post"])

    kv = _mem_kv(c["mem"], w["xa_mem_norm"], w["xa_wkv"])
    kv = kv.reshape(B, -1, 2 * XA_HEADS, Dh).transpose(0, 2, 1, 3)
    x2 = _xattn(x1, kv, w["xa_norm_pre"], w["xa_wq"], w["xa_wo"], w["xa_norm_post"])

    x3 = _ffn(x2.reshape(M, D), w["ffn_norm_pre"], w["ffn_wg"], w["ffn_wu"], w["ffn_wo"], w["ffn_norm_post"])
    return x3.reshape(B, S, D)


def _constants(mem, positions, S):
    B = positions.shape[0]
    pos = positions.astype(F32)
    half = MLA_ROPE_DIM // 2
    inv = (ROPE_THETA ** (-np.arange(half, dtype=np.float32) / half)).astype(np.float32)
    inv_row = jnp.asarray(np.tile(inv, 2 * MLA_HEADS)[None, :])
    cos, sin = _rope_tables(pos.reshape(B * S, 1), inv_row)
    nc = S // NSA_CMP_STRIDE
    n_valid = (S - NSA_CMP_LEN) // NSA_CMP_STRIDE + 1
    assert n_valid == nc - 1
    c_end = np.minimum(np.arange(nc) * NSA_CMP_STRIDE + NSA_CMP_LEN - 1, S - 1)
    n_sel = S // NSA_SEL_LEN
    c_start = np.arange(nc) * NSA_CMP_STRIDE
    sel_start = np.arange(n_sel) * NSA_SEL_LEN
    overlap = ((c_start[:, None] < sel_start[None, :] + NSA_SEL_LEN)
               & (c_start[:, None] + NSA_CMP_LEN - 1 >= sel_start[None, :])
               & (np.arange(nc)[:, None] < n_valid)).astype(np.float32)
    nsa_slopes = jnp.asarray(_alibi_slopes(NSA_HEADS).reshape(NSA_KV_HEADS, NSA_GROUP))
    swa_slopes = jnp.asarray(_alibi_slopes(SWA_HEADS).reshape(SWA_KV_HEADS, SWA_GROUP))
    return dict(
        mem=mem, cos=cos.reshape(B, S, -1), sin=sin.reshape(B, S, -1),
        pos_col=pos[:, :, None],
        pos_rows_flash=pos.reshape(B, S // FLASH_TK, 1, FLASH_TK),
        pos_rows_band=pos.reshape(B, S // LANES, 1, LANES),
        pc_row=pos[:, c_end][:, None, :],
        overlap_t=jnp.asarray(overlap.T),
        nsa_slopes_cmp=_per_row(nsa_slopes, NSA_CMP_TQ, LANES),
        nsa_slopes_flash=_per_row(nsa_slopes, FLASH_TQ // NSA_GROUP, LANES),
        nsa_slopes_band=_per_row(nsa_slopes, BAND_TQ),
        swa_slopes_band=_per_row(swa_slopes, BAND_TQ),
    )


def kernel(x, mem, positions, mix_norm_pre, w_in, nsa_cmp_w, nsa_cmp_pe, swa_sinks, mla_q_norm, mla_w_q_up,
           mla_kv_norm, mla_w_kv_up, fox_f_bias, w_branch, w_gate, w_out, mix_norm_post, xa_norm_pre,
           xa_mem_norm, xa_w_q, xa_w_kv, xa_w_o, xa_norm_post, ffn_norm_pre, ffn_w_in, ffn_w_out,
           ffn_norm_post):
    params = dict(
        mix_norm_pre=mix_norm_pre, w_in=w_in, nsa_cmp_w=nsa_cmp_w, nsa_cmp_pe=nsa_cmp_pe, swa_sinks=swa_sinks,
        mla_q_norm=mla_q_norm, mla_w_q_up=mla_w_q_up, mla_kv_norm=mla_kv_norm, mla_w_kv_up=mla_w_kv_up,
        fox_f_bias=fox_f_bias, w_branch=w_branch, w_gate=w_gate, w_out=w_out, mix_norm_post=mix_norm_post,
        xa_norm_pre=xa_norm_pre, xa_mem_norm=xa_mem_norm, xa_w_q=xa_w_q, xa_w_kv=xa_w_kv, xa_w_o=xa_w_o,
        xa_norm_post=xa_norm_post, ffn_norm_pre=ffn_norm_pre, ffn_w_in=ffn_w_in, ffn_w_out=ffn_w_out,
        ffn_norm_post=ffn_norm_post)
    S = x.shape[1]
    weights = _prep_weights(params)
    consts = _constants(mem, positions, S)

    def body(xc, w):
        return _layer(xc, w, consts), None

    out, _ = lax.scan(body, x, weights)
    return out
```
